```python
import math
import jax, jax.numpy as jnp
from jax import lax
import numpy as np

D_MODEL = 2048
BATCH = 2
SEQ = 4096
DEPTH = 4

HEAD_DIM = 128
GRID_W = 64
BLOCK = 128
EPS = 1e-6
A_HEADS = 8
NA_ROWS = 8
NA_COLS = 16
B_Q_HEADS = 8
B_KV_HEADS = 2
AXIAL_THETA = 10000.0
C_Q_HEADS = 12
C_KV_HEADS = 4
C_WINDOW = 128
C_SIDE_BLOCKS = -(-C_WINDOW // BLOCK)
D_DILATIONS = ((128, 1), (512, 4), (2048, 16))
D_GROUPS = len(D_DILATIONS)
D_SLOTS = 4
D_Q_HEADS = D_GROUPS * D_SLOTS
ROPE_THETA = 500000.0
ROPE_DIMS = HEAD_DIM // 4
FFN_HIDDEN = ((-(-8 * D_MODEL // 3)) + 255) // 256 * 256
N_EVEN = (DEPTH + 1) // 2
N_ODD = DEPTH // 2
AB_IN = (3 * A_HEADS + B_Q_HEADS + 2 * B_KV_HEADS) * HEAD_DIM
AB_OUT = (A_HEADS + B_Q_HEADS) * HEAD_DIM
CD_IN = (C_Q_HEADS + 2 * C_KV_HEADS + D_Q_HEADS + 2 * D_SLOTS) * HEAD_DIM
CD_OUT = (C_Q_HEADS + D_SLOTS) * HEAD_DIM

kernel_name = "hybrid_natten_axial_window_dilated_encoder"


def rms_norm(x, g):
    xf = x.astype(jnp.float32)
    y = xf * lax.rsqrt(jnp.mean(xf * xf, axis=-1, keepdims=True) + EPS)
    return (y * g.astype(jnp.float32)).astype(x.dtype)


def rope(x, pos, theta):
    r = x.shape[-1]
    half = r // 2
    inv = jnp.exp(-math.log(theta) * jnp.arange(half, dtype=jnp.float32) * (2.0 / r))
    ang = pos.astype(jnp.float32)[:, None] * inv[None, :]
    cos, sin = jnp.cos(ang), jnp.sin(ang)
    xf = x.astype(jnp.float32)
    x1, x2 = xf[..., :half], xf[..., half:]
    return jnp.concatenate([x1 * cos - x2 * sin, x1 * sin + x2 * cos], axis=-1).astype(x.dtype)


def partial_rope(x, pos):
    return jnp.concatenate([rope(x[..., :ROPE_DIMS], pos, ROPE_THETA), x[..., ROPE_DIMS:]], axis=-1)


def axial_rope(x, rows, cols):
    h = HEAD_DIM // 2
    return jnp.concatenate([rope(x[..., :h], rows, AXIAL_THETA), rope(x[..., h:], cols, AXIAL_THETA)], axis=-1)


def split_cols(h, widths):
    outs, start = [], 0
    for w in widths:
        outs.append(h[..., start:start + w])
        start += w
    return outs


def split_heads(h):
    b, s, c = h.shape
    return h.reshape(b, s, c // HEAD_DIM, HEAD_DIM).transpose(0, 2, 1, 3)


def merge_heads(o):
    b, n, s, dh = o.shape
    return o.transpose(0, 2, 1, 3).reshape(b, s, n * dh)


def neighbourhood_attention(q, k, v, rpb):
    b, h, s, dh = q.shape
    rows = s // GRID_W
    kr = min(NA_ROWS, rows)
    qg = q.reshape(b, h, rows, GRID_W, dh)
    kg = k.reshape(b, h, rows, GRID_W, dh)
    vg = v.reshape(b, h, rows, GRID_W, dh)
    cols = jnp.arange(GRID_W)
    col_start = jnp.clip(cols - NA_COLS // 2, 0, GRID_W - NA_COLS)
    col_idx = col_start[:, None] + jnp.arange(NA_COLS)[None, :]
    dc = col_idx - cols[:, None]
    scale = dh ** -0.5

    def row_block(r):
        r0 = jnp.clip(r - kr // 2, 0, rows - kr)
        kb = lax.dynamic_slice_in_dim(kg, r0, kr, axis=2)
        vb = lax.dynamic_slice_in_dim(vg, r0, kr, axis=2)
        kn = kb[:, :, :, col_idx]
        vn = vb[:, :, :, col_idx]
        qr = lax.dynamic_index_in_dim(qg, r, axis=2, keepdims=False)
        logits = jnp.einsum('bhwd,bhiwjd->bhwij', qr, kn,
                            preferred_element_type=jnp.float32) * scale
        dr = r0 + jnp.arange(kr) - r
        bias = rpb[:, dr[:, None, None] + NA_ROWS - 1, dc[None, :, :] + NA_COLS - 1]
        logits = logits + bias.transpose(0, 2, 1, 3)[None].astype(jnp.float32)
        p = jax.nn.softmax(logits.reshape(b, h, GRID_W, kr * NA_COLS), axis=-1)
        p = p.reshape(b, h, GRID_W, kr, NA_COLS).astype(v.dtype)
        return jnp.einsum('bhwij,bhiwjd->bhwd', p, vn)

    out = lax.map(row_block, jnp.arange(rows))
    return out.transpose(1, 2, 0, 3, 4).reshape(b, h, s, dh)


def dense_gqa(q, k, v):
    b, hq, s, dh = q.shape
    hkv = k.shape[1]
    g = hq // hkv
    nb = s // BLOCK
    scale = dh ** -0.5
    qb = q.reshape(b, hkv, g, nb, BLOCK, dh).transpose(3, 0, 1, 2, 4, 5)

    def blk(qi):
        logits = jnp.einsum('bkgqd,bksd->bkgqs', qi, k, preferred_element_type=jnp.float32) * scale
        p = jax.nn.softmax(logits, axis=-1).astype(v.dtype)
        return jnp.einsum('bkgqs,bksd->bkgqd', p, v)

    out = lax.map(blk, qb)
    return out.transpose(1, 2, 3, 0, 4, 5).reshape(b, hq, s, dh)


def window_gqa_sink(q, k, v, sink):
    b, hq, s, dh = q.shape
    hkv = k.shape[1]
    g = hq // hkv
    nb = s // BLOCK
    n_side = C_SIDE_BLOCKS
    n_band = 2 * n_side + 1
    scale = dh ** -0.5

    def band(t):
        tp = jnp.pad(t, ((0, 0), (0, 0), (n_side * BLOCK, n_side * BLOCK), (0, 0)))
        tb = tp.reshape(b, hkv, nb + 2 * n_side, BLOCK, dh)
        return jnp.concatenate([tb[:, :, j:j + nb] for j in range(n_band)], axis=3)

    kb, vb = band(k), band(v)
    qb = q.reshape(b, hkv, g, nb, BLOCK, dh)
    logits = jnp.einsum('bkgnqd,bknjd->bkgnqj', qb, kb,
                        preferred_element_type=jnp.float32) * scale
    blk = jnp.arange(nb)[:, None, None] * BLOCK
    qpos = blk + jnp.arange(BLOCK)[None, :, None]
    kpos = blk - n_side * BLOCK + jnp.arange(n_band * BLOCK)[None, None, :]
    valid = (jnp.abs(kpos - qpos) <= C_WINDOW) & (kpos >= 0) & (kpos < s)
    logits = jnp.where(valid, logits, -jnp.inf)
    sink_l = jnp.broadcast_to(sink.astype(jnp.float32).reshape(1, hkv, g, 1, 1, 1), logits.shape[:-1] + (1,))
    p = jax.nn.softmax(jnp.concatenate([logits, sink_l], axis=-1), axis=-1)[..., :-1]
    out = jnp.einsum('bkgnqj,bknjd->bkgnqd', p.astype(v.dtype), vb)
    return out.reshape(b, hq, s, dh)


def dilated_attention(q, k, v):
    b, _, s, dh = q.shape
    nb = s // BLOCK
    scale = dh ** -0.5
    offsets = []
    for w, d in D_DILATIONS:
        half = (w // 2) // d
        offsets.append(d * jnp.arange(-half, half + 1))
    qb = q.reshape(b, D_GROUPS, D_SLOTS, nb, BLOCK, dh).transpose(3, 0, 1, 2, 4, 5)

    def blk(args):
        i, qi = args
        t = i * BLOCK + jnp.arange(BLOCK)
        ms, dens, nums = [], [], []
        for gi, off in enumerate(offsets):
            idx = t[:, None] + off[None, :]
            valid = (idx >= 0) & (idx < s)
            idc = jnp.clip(idx, 0, s - 1)
            kg = k[:, :, idc]
            vg = v[:, :, idc]
            logits = jnp.einsum('bnqd,bnqjd->bnqj', qi[:, gi], kg,
                                preferred_element_type=jnp.float32) * scale
            logits = jnp.where(valid, logits, -jnp.inf)
            m = jnp.max(logits, axis=-1, keepdims=True)
            p = jnp.exp(logits - m)
            ms.append(m)
            dens.append(jnp.sum(p, axis=-1, keepdims=True))
            nums.append(jnp.einsum('bnqj,bnqjd->bnqd', p, vg.astype(jnp.float32)))
        m_all = jnp.max(jnp.stack(ms), axis=0)
        num = sum(jnp.exp(m - m_all) * n for m, n in zip(ms, nums))
        den = sum(jnp.exp(m - m_all) * dd for m, dd in zip(ms, dens))
        return (num / den).astype(v.dtype)

    out = lax.map(blk, (jnp.arange(nb), qb))
    return out.transpose(1, 2, 0, 3, 4).reshape(b, D_SLOTS, s, dh)


def mixer_ab(xn, w_in, w_out, a_qn, a_kn, a_rpb, b_qn, b_kn):
    s = xn.shape[1]
    h = xn @ w_in
    qa, ka, va, qb, kb, vb = [split_heads(t) for t in split_cols(
        h, [A_HEADS * HEAD_DIM] * 3 + [B_Q_HEADS * HEAD_DIM, B_KV_HEADS * HEAD_DIM, B_KV_HEADS * HEAD_DIM])]
    oa = neighbourhood_attention(rms_norm(qa, a_qn), rms_norm(ka, a_kn), va, a_rpb)
    pos = jnp.arange(s)
    rows, cols = pos // GRID_W, pos % GRID_W
    qb = axial_rope(rms_norm(qb, b_qn), rows, cols)
    kb = axial_rope(rms_norm(kb, b_kn), rows, cols)
    ob = dense_gqa(qb, kb, vb)
    return jnp.concatenate([merge_heads(oa), merge_heads(ob)], axis=-1) @ w_out


def mixer_cd(xn, w_in, w_out, c_qn, c_kn, c_sink, d_qn, d_kn):
    s = xn.shape[1]
    h = xn @ w_in
    qc, kc, vc, qd, kd, vd = [split_heads(t) for t in split_cols(
        h, [C_Q_HEADS * HEAD_DIM, C_KV_HEADS * HEAD_DIM, C_KV_HEADS * HEAD_DIM,
            D_Q_HEADS * HEAD_DIM, D_SLOTS * HEAD_DIM, D_SLOTS * HEAD_DIM])]
    pos = jnp.arange(s)
    oc = window_gqa_sink(partial_rope(rms_norm(qc, c_qn), pos), partial_rope(rms_norm(kc, c_kn), pos), vc, c_sink)
    od = dilated_attention(partial_rope(rms_norm(qd, d_qn), pos), partial_rope(rms_norm(kd, d_kn), pos), vd)
    return jnp.concatenate([merge_heads(oc), merge_heads(od)], axis=-1) @ w_out


def swiglu(xn, w_gate, w_up, w_down):
    return (jax.nn.silu(xn @ w_gate) * (xn @ w_up)) @ w_down


def setup_inputs(seed: int = 0) -> dict:
    key = jax.random.key(seed)
    ks = jax.random.split(key, 20)
    nrm = lambda k, shape, sc: jax.random.normal(k, shape, jnp.float32) * sc
    gain = lambda k, shape: 1.0 + 0.05 * jax.random.normal(k, shape, jnp.float32)
    return {
        "x": nrm(ks[0], (BATCH, SEQ, D_MODEL), 1.0),
        "attn_norm": gain(ks[1], (DEPTH, D_MODEL)),
        "ffn_norm": gain(ks[2], (DEPTH, D_MODEL)),
        "ab_w_in": nrm(ks[3], (N_EVEN, D_MODEL, AB_IN), D_MODEL ** -0.5),
        "ab_w_out": nrm(ks[4], (N_EVEN, AB_OUT, D_MODEL), AB_OUT ** -0.5),
        "a_q_norm": gain(ks[5], (N_EVEN, HEAD_DIM)),
        "a_k_norm": gain(ks[6], (N_EVEN, HEAD_DIM)),
        "a_rpb": nrm(ks[7], (N_EVEN, A_HEADS, 2 * NA_ROWS - 1, 2 * NA_COLS - 1), 0.5),
        "b_q_norm": gain(ks[8], (N_EVEN, HEAD_DIM)),
        "b_k_norm": gain(ks[9], (N_EVEN, HEAD_DIM)),
        "cd_w_in": nrm(ks[10], (N_ODD, D_MODEL, CD_IN), D_MODEL ** -0.5),
        "cd_w_out": nrm(ks[11], (N_ODD, CD_OUT, D_MODEL), CD_OUT ** -0.5),
        "c_q_norm": gain(ks[12], (N_ODD, HEAD_DIM)),
        "c_k_norm": gain(ks[13], (N_ODD, HEAD_DIM)),
        "c_sink": nrm(ks[14], (N_ODD, C_Q_HEADS), 0.5),
        "d_q_norm": gain(ks[15], (N_ODD, HEAD_DIM)),
        "d_k_norm": gain(ks[16], (N_ODD, HEAD_DIM)),
        "w_gate": nrm(ks[17], (DEPTH, D_MODEL, FFN_HIDDEN), D_MODEL ** -0.5),
        "w_up": nrm(ks[18], (DEPTH, D_MODEL, FFN_HIDDEN), D_MODEL ** -0.5),
        "w_down": nrm(ks[19], (DEPTH, FFN_HIDDEN, D_MODEL), FFN_HIDDEN ** -0.5),
    }


def reference(x, attn_norm, ffn_norm, ab_w_in, ab_w_out, a_q_norm, a_k_norm, a_rpb, b_q_norm, b_k_norm,
              cd_w_in, cd_w_out, c_q_norm, c_k_norm, c_sink, d_q_norm, d_k_norm, w_gate, w_up, w_down):
    for layer in range(DEPTH):
        j = layer // 2
        xn = rms_norm(x, attn_norm[layer])
        if layer % 2 == 0:
            x = x + mixer_ab(xn, ab_w_in[j], ab_w_out[j], a_q_norm[j], a_k_norm[j], a_rpb[j],
                             b_q_norm[j], b_k_norm[j])
        else:
            x = x + mixer_cd(xn, cd_w_in[j], cd_w_out[j], c_q_norm[j], c_k_norm[j], c_sink[j],
                             d_q_norm[j], d_k_norm[j])
        x = x + swiglu(rms_norm(x, ffn_norm[layer]), w_gate[layer], w_up[layer], w_down[layer])
    return x
```

```python
import functools
import math

import numpy as np
import jax
import jax.numpy as jnp
from jax import lax
from jax.experimental import pallas as pl
from jax.experimental.pallas import tpu as pltpu

D_MODEL = 2048
BATCH = 2
SEQ = 4096
DEPTH = 4
HEAD_DIM = 128
GRID_W = 64
GRID_ROWS = SEQ // GRID_W
EPS = 1e-6
A_HEADS = 8
NA_ROWS = 8
NA_COLS = 16
B_Q_HEADS = 8
B_KV_HEADS = 2
AXIAL_THETA = 10000.0
C_Q_HEADS = 12
C_KV_HEADS = 4
C_WINDOW = 128
D_DILATIONS = ((128, 1), (512, 4), (2048, 16))
D_GROUPS = len(D_DILATIONS)
D_SLOTS = 4
D_Q_HEADS = D_GROUPS * D_SLOTS
ROPE_THETA = 500000.0
ROPE_DIMS = HEAD_DIM // 4
FFN_HIDDEN = ((-(-8 * D_MODEL // 3)) + 255) // 256 * 256
M_ROWS = BATCH * SEQ
SCALE = HEAD_DIM ** -0.5
NEG = -1e30

VMEM_LIMIT_BYTES = 52 * 1024 * 1024

F32 = jnp.float32
BF16 = jnp.bfloat16
NT_DIMS = (((1,), (1,)), ((), ()))


def _params(**kw):
    return pltpu.CompilerParams(vmem_limit_bytes=VMEM_LIMIT_BYTES, **kw)


def _rmsnorm_body(x_ref, g_ref, o_ref):
    x = x_ref[...]
    ms = jnp.mean(x * x, axis=-1, keepdims=True)
    o_ref[...] = (x * lax.rsqrt(ms + EPS) * g_ref[...]).astype(o_ref.dtype)


def rmsnorm(x, g, tm=512):
    m, d = x.shape
    return pl.pallas_call(
        _rmsnorm_body,
        grid=(m // tm,),
        in_specs=[pl.BlockSpec((tm, d), lambda i: (i, 0)), pl.BlockSpec((1, d), lambda i: (0, 0))],
        out_specs=pl.BlockSpec((tm, d), lambda i: (i, 0)),
        out_shape=jax.ShapeDtypeStruct((m, d), BF16),
        compiler_params=_params(),
        name="rmsnorm",
    )(x, g.reshape(1, d))


def _proj_body(*refs, norm, rope_half, tn):
    if rope_half is not None:
        x_ref, w_ref, g_ref, cos_ref, sa_ref, sb_ref, o_ref = refs
    elif norm:
        x_ref, w_ref, g_ref, o_ref = refs
    else:
        x_ref, w_ref, o_ref = refs
    acc = jnp.dot(x_ref[...], w_ref[...], preferred_element_type=F32)
    if not norm:
        o_ref[...] = acc.astype(o_ref.dtype)
        return
    for h in range(tn // HEAD_DIM):
        sl = slice(h * HEAD_DIM, (h + 1) * HEAD_DIM)
        y = acc[:, sl]
        ms = jnp.mean(y * y, axis=-1, keepdims=True)
        y = y * lax.rsqrt(ms + EPS) * g_ref[:, sl]
        if rope_half is not None:
            y = (y * cos_ref[...]
                 + pltpu.roll(y, HEAD_DIM - rope_half, 1) * sa_ref[...]
                 + pltpu.roll(y, rope_half, 1) * sb_ref[...])
        o_ref[:, sl] = y.astype(o_ref.dtype)


def proj(xn, w, col_off, width, *, gain=None, rope=None, tm=1024, tn=256):
    m, k = xn.shape
    assert col_off % tn == 0 and width % tn == 0 and m % tm == 0 and SEQ % tm == 0
    off_blocks = col_off // tn
    norm = gain is not None
    in_specs = [pl.BlockSpec((tm, k), lambda i, j: (i, 0)),
                pl.BlockSpec((k, tn), lambda i, j: (0, off_blocks + j))]
    args = [xn, w]
    rope_half = None
    if norm:
        in_specs.append(pl.BlockSpec((1, tn), lambda i, j: (0, j)))
        args.append(gain.reshape(1, width).astype(F32))
    if rope is not None:
        rope_half, tables = rope
        pos_blocks = SEQ // tm
        for t in tables:
            in_specs.append(pl.BlockSpec((tm, HEAD_DIM), lambda i, j: (i % pos_blocks, 0)))
            args.append(t)
    return pl.pallas_call(
        functools.partial(_proj_body, norm=norm, rope_half=rope_half, tn=tn),
        grid=(m // tm, width // tn),
        in_specs=in_specs,
        out_specs=pl.BlockSpec((tm, tn), lambda i, j: (i, j)),
        out_shape=jax.ShapeDtypeStruct((m, width), BF16),
        compiler_params=_params(),
        name="in_proj",
    )(*args)


def _mm_res_body(*refs, n_in):
    a_refs = refs[:n_in]
    w_refs = refs[n_in:2 * n_in]
    x_ref = refs[2 * n_in]
    o_ref = refs[2 * n_in + 1]
    acc = x_ref[...]
    for a_ref, w_ref in zip(a_refs, w_refs):
        acc = acc + jnp.dot(a_ref[...], w_ref[...], preferred_element_type=F32)
    o_ref[...] = acc


def matmul_residual(a_list, w, x, tm=512, tn=512):
    m, n = x.shape
    in_specs, w_specs = [], []
    row = 0
    for a in a_list:
        ka = a.shape[1]
        assert row % ka == 0
        in_specs.append(pl.BlockSpec((tm, ka), lambda i, j: (i, 0)))
        w_specs.append(pl.BlockSpec((ka, tn), lambda i, j, rb=row // ka: (rb, j)))
        row += ka
    assert row == w.shape[0]
    in_specs = in_specs + w_specs + [pl.BlockSpec((tm, tn), lambda i, j: (i, j))]
    return pl.pallas_call(
        functools.partial(_mm_res_body, n_in=len(a_list)),
        grid=(m // tm, n // tn),
        in_specs=in_specs,
        out_specs=pl.BlockSpec((tm, tn), lambda i, j: (i, j)),
        out_shape=jax.ShapeDtypeStruct((m, n), F32),
        compiler_params=_params(),
        name="matmul_residual",
    )(*a_list, *([w] * len(a_list)), x)


def _gate_up_body(x_ref, wg_ref, wu_ref, o_ref):
    x = x_ref[...]
    g = jnp.dot(x, wg_ref[...], preferred_element_type=F32)
    u = jnp.dot(x, wu_ref[...], preferred_element_type=F32)
    o_ref[...] = (g * (1.0 / (1.0 + jnp.exp(-g))) * u).astype(o_ref.dtype)


def gate_up(xn, wg, wu, tm=1024, tn=512):
    m, k = xn.shape
    n = wg.shape[1]
    return pl.pallas_call(
        _gate_up_body,
        grid=(m // tm, n // tn),
        in_specs=[pl.BlockSpec((tm, k), lambda i, j: (i, 0)),
                  pl.BlockSpec((k, tn), lambda i, j: (0, j)),
                  pl.BlockSpec((k, tn), lambda i, j: (0, j))],
        out_specs=pl.BlockSpec((tm, tn), lambda i, j: (i, j)),
        out_shape=jax.ShapeDtypeStruct((m, n), BF16),
        compiler_params=_params(),
        name="gate_up",
    )(xn, wg, wu)


NA_QROWS = 2
NA_WIN_ROWS = 10
NA_Q = NA_QROWS * GRID_W
NA_KEYS = NA_WIN_ROWS * GRID_W
NA_BLOCKS = GRID_ROWS // NA_QROWS
NA_VARIANT_BLOCKS = (0, 1, 2, NA_BLOCKS - 2, NA_BLOCKS - 1)


def _na_win_start(rb):
    return np.clip(NA_QROWS * rb - NA_ROWS // 2, 0, GRID_ROWS - NA_WIN_ROWS)


def _na_bias_index():
    qi = np.arange(NA_Q)
    kj = np.arange(NA_KEYS)
    idx_r = np.zeros((len(NA_VARIANT_BLOCKS), NA_Q, NA_KEYS), np.int32)
    idx_c = np.zeros_like(idx_r)
    valid = np.zeros(idx_r.shape, bool)
    for v, rb in enumerate(NA_VARIANT_BLOCKS):
        ws = _na_win_start(rb)
        qrow = (NA_QROWS * rb + qi // GRID_W)[:, None]
        qcol = (qi % GRID_W)[:, None]
        krow = (ws + kj // GRID_W)[None, :]
        kcol = (kj % GRID_W)[None, :]
        r0 = np.clip(qrow - NA_ROWS // 2, 0, GRID_ROWS - NA_ROWS)
        c0 = np.clip(qcol - NA_COLS // 2, 0, GRID_W - NA_COLS)
        ok = (krow >= r0) & (krow < r0 + NA_ROWS) & (kcol >= c0) & (kcol < c0 + NA_COLS)
        valid[v] = ok
        idx_r[v] = np.where(ok, krow - qrow + NA_ROWS - 1, 0)
        idx_c[v] = np.where(ok, kcol - qcol + NA_COLS - 1, 0)
    return idx_r, idx_c, valid


def _na_body(q_ref, k_ref, v_ref, t_ref, o_ref):
    def step(rb, carry):
        ws = jnp.clip(NA_QROWS * rb - NA_ROWS // 2, 0, GRID_ROWS - NA_WIN_ROWS)
        koff = pl.multiple_of(ws * GRID_W, GRID_W)
        qoff = pl.multiple_of(rb * NA_Q, NA_Q)
        variant = jnp.where(rb < 2, rb, jnp.where(rb >= NA_BLOCKS - 2, rb - (NA_BLOCKS - 5), 2))
        q = q_ref[pl.ds(qoff, NA_Q), :]
        k = k_ref[pl.ds(koff, NA_KEYS), :]
        v = v_ref[pl.ds(koff, NA_KEYS), :]
        s = lax.dot_general(q, k, NT_DIMS, preferred_element_type=F32) + t_ref[variant]
        m = jnp.max(s, axis=-1, keepdims=True)
        p = jnp.exp(s - m)
        l = jnp.sum(p, axis=-1, keepdims=True)
        o = jnp.dot(p.astype(BF16), v, preferred_element_type=F32)
        o_ref[pl.ds(qoff, NA_Q), :] = (o / l).astype(o_ref.dtype)
        return carry

    lax.fori_loop(0, NA_BLOCKS, step, 0)


def neighbourhood_attention(qk, v, table):
    nv = len(NA_VARIANT_BLOCKS)
    return pl.pallas_call(
        _na_body,
        grid=(BATCH, A_HEADS),
        in_specs=[pl.BlockSpec((SEQ, HEAD_DIM), lambda b, h: (b, h)),
                  pl.BlockSpec((SEQ, HEAD_DIM), lambda b, h: (b, A_HEADS + h)),
                  pl.BlockSpec((SEQ, HEAD_DIM), lambda b, h: (b, h)),
                  pl.BlockSpec((None, nv, NA_Q, NA_KEYS), lambda b, h: (h, 0, 0, 0))],
        out_specs=pl.BlockSpec((SEQ, HEAD_DIM), lambda b, h: (b, h)),
        out_shape=jax.ShapeDtypeStruct((M_ROWS, A_HEADS * HEAD_DIM), BF16),
        compiler_params=_params(),
        name="neighbourhood_attention",
    )(qk, qk, v, table)


B_GROUP = B_Q_HEADS // B_KV_HEADS


def _dense_body(q_ref, k_ref, v_ref, o_ref):
    k = k_ref[...]
    v = v_ref[...]
    for h in range(B_GROUP):
        sl = slice(h * HEAD_DIM, (h + 1) * HEAD_DIM)
        s = lax.dot_general(q_ref[:, sl], k, NT_DIMS, preferred_element_type=F32)
        m = jnp.max(s, axis=-1, keepdims=True)
        p = jnp.exp(s - m)
        l = jnp.sum(p, axis=-1, keepdims=True)
        o = jnp.dot(p.astype(BF16), v, preferred_element_type=F32)
        o_ref[:, sl] = (o / l).astype(o_ref.dtype)


def dense_gqa(qk, v, tq=256):
    nq = SEQ // tq
    gw = B_GROUP * HEAD_DIM
    return pl.pallas_call(
        _dense_body,
        grid=(BATCH, B_KV_HEADS, nq),
        in_specs=[pl.BlockSpec((tq, gw), lambda b, kv, i: (b * nq + i, kv)),
                  pl.BlockSpec((SEQ, HEAD_DIM), lambda b, kv, i: (b, B_Q_HEADS + kv)),
                  pl.BlockSpec((SEQ, HEAD_DIM), lambda b, kv, i: (b, kv))],
        out_specs=pl.BlockSpec((tq, gw), lambda b, kv, i: (b * nq + i, kv)),
        out_shape=jax.ShapeDtypeStruct((M_ROWS, B_Q_HEADS * HEAD_DIM), BF16),
        compiler_params=_params(),
        name="dense_gqa",
    )(qk, qk, v)


QBLK = 128
N_QBLK = SEQ // QBLK


def _band_logits(q, k_ref, i, span, stride):
    win = QBLK + 2 * span
    ws = jnp.clip(i * QBLK - span, 0, SEQ - win)
    koff = pl.multiple_of(ws, 64)
    s = lax.dot_general(q, k_ref[pl.ds(koff, win), :], NT_DIMS, preferred_element_type=F32)
    col = lax.broadcasted_iota(jnp.int32, (QBLK, win), 1)
    row = lax.broadcasted_iota(jnp.int32, (QBLK, win), 0)
    dist = col - row + (ws - i * QBLK)
    ok = jnp.abs(dist) <= span
    if stride > 1:
        ok = ok & ((dist & (stride - 1)) == 0)
    return jnp.where(ok, s, NEG), koff, win


C_GROUP = C_Q_HEADS // C_KV_HEADS


def _window_sink_body(sink_ref, q_ref, k_ref, v_ref, o_ref):
    kv = pl.program_id(1)

    def step(i, carry):
        qoff = pl.multiple_of(i * QBLK, QBLK)
        for g in range(C_GROUP):
            sl = slice(g * HEAD_DIM, (g + 1) * HEAD_DIM)
            s, koff, win = _band_logits(q_ref[pl.ds(qoff, QBLK), sl], k_ref, i, C_WINDOW, 1)
            sink = sink_ref[kv * C_GROUP + g]
            m = jnp.maximum(jnp.max(s, axis=-1, keepdims=True), sink)
            p = jnp.exp(s - m)
            l = jnp.sum(p, axis=-1, keepdims=True) + jnp.exp(sink - m)
            o = jnp.dot(p.astype(BF16), v_ref[pl.ds(koff, win), :], preferred_element_type=F32)
            o_ref[pl.ds(qoff, QBLK), sl] = (o / l).astype(o_ref.dtype)
        return carry

    lax.fori_loop(0, N_QBLK, step, 0)


def window_gqa_sink(qk, v, sink):
    gw = C_GROUP * HEAD_DIM
    return pl.pallas_call(
        _window_sink_body,
        grid=(BATCH, C_KV_HEADS),
        in_specs=[pl.BlockSpec(memory_space=pltpu.SMEM),
                  pl.BlockSpec((SEQ, gw), lambda b, kv: (b, kv)),
                  pl.BlockSpec((SEQ, HEAD_DIM), lambda b, kv: (b, C_Q_HEADS + kv)),
                  pl.BlockSpec((SEQ, HEAD_DIM), lambda b, kv: (b, kv))],
        out_specs=pl.BlockSpec((SEQ, gw), lambda b, kv: (b, kv)),
        out_shape=jax.ShapeDtypeStruct((M_ROWS, C_Q_HEADS * HEAD_DIM), BF16),
        compiler_params=_params(),
        name="window_gqa_sink",
    )(sink.astype(F32), qk, qk, v)


def _dilated_body(q0_ref, q1_ref, q2_ref, k_ref, v_ref, o_ref):
    q_refs = (q0_ref, q1_ref, q2_ref)

    def step(i, carry):
        qoff = pl.multiple_of(i * QBLK, QBLK)
        ms, dens, nums = [], [], []
        for q_ref, (w, d) in zip(q_refs, D_DILATIONS):
            span = ((w // 2) // d) * d
            s, koff, win = _band_logits(q_ref[pl.ds(qoff, QBLK), :], k_ref, i, span, d)
            m = jnp.max(s, axis=-1, keepdims=True)
            p = jnp.exp(s - m)
            ms.append(m)
            dens.append(jnp.sum(p, axis=-1, keepdims=True))
            nums.append(jnp.dot(p.astype(BF16), v_ref[pl.ds(koff, win), :], preferred_element_type=F32))
        m_all = jnp.maximum(jnp.maximum(ms[0], ms[1]), ms[2])
        ws = [jnp.exp(m - m_all) for m in ms]
        num = ws[0] * nums[0] + ws[1] * nums[1] + ws[2] * nums[2]
        den = ws[0] * dens[0] + ws[1] * dens[1] + ws[2] * dens[2]
        o_ref[pl.ds(qoff, QBLK), :] = (num / den).astype(o_ref.dtype)
        return carry

    lax.fori_loop(0, N_QBLK, step, 0)


def dilated_attention(qk, v):
    blk = (SEQ, HEAD_DIM)
    return pl.pallas_call(
        _dilated_body,
        grid=(BATCH, D_SLOTS),
        in_specs=[pl.BlockSpec(blk, lambda b, s: (b, s)),
                  pl.BlockSpec(blk, lambda b, s: (b, D_SLOTS + s)),
                  pl.BlockSpec(blk, lambda b, s: (b, 2 * D_SLOTS + s)),
                  pl.BlockSpec(blk, lambda b, s: (b, D_Q_HEADS + s)),
                  pl.BlockSpec(blk, lambda b, s: (b, s))],
        out_specs=pl.BlockSpec(blk, lambda b, s: (b, s)),
        out_shape=jax.ShapeDtypeStruct((M_ROWS, D_SLOTS * HEAD_DIM), BF16),
        compiler_params=_params(),
        name="dilated_attention",
    )(qk, qk, qk, qk, v)


def _rope_tables(kind):
    pos = np.arange(SEQ, dtype=np.float64)
    cos = np.ones((SEQ, HEAD_DIM), np.float64)
    sa = np.zeros((SEQ, HEAD_DIM), np.float64)
    sb = np.zeros((SEQ, HEAD_DIM), np.float64)

    def fill(start, r, p, theta):
        half = r // 2
        inv = np.exp(-math.log(theta) * np.arange(half, dtype=np.float64) * (2.0 / r))
        ang = p[:, None] * inv[None, :]
        cos[:, start:start + half] = np.cos(ang)
        cos[:, start + half:start + r] = np.cos(ang)
        sa[:, start:start + half] = -np.sin(ang)
        sb[:, start + half:start + r] = np.sin(ang)

    if kind == "axial":
        h = HEAD_DIM // 2
        fill(0, h, np.floor(pos / GRID_W), AXIAL_THETA)
        fill(h, h, pos % GRID_W, AXIAL_THETA)
        half = h // 2
    else:
        fill(0, ROPE_DIMS, pos, ROPE_THETA)
        half = ROPE_DIMS // 2
    return half, tuple(jnp.asarray(t, dtype=F32) for t in (cos, sa, sb))


def _tile_gain(g, heads, scale=1.0):
    return jnp.tile(g.astype(F32) * scale, heads)


def _layer_ab(x, xn, w_in, w_out, a_qn, a_kn, a_rpb, b_qn, b_kn, axial, na_index):
    hd = HEAD_DIM
    a_w = A_HEADS * hd
    qk_a = proj(xn, w_in, 0, 2 * a_w,
                gain=jnp.concatenate([_tile_gain(a_qn, A_HEADS, SCALE), _tile_gain(a_kn, A_HEADS)]))
    v_a = proj(xn, w_in, 2 * a_w, a_w)
    b_off = 3 * a_w
    qk_b = proj(xn, w_in, b_off, (B_Q_HEADS + B_KV_HEADS) * hd,
                gain=jnp.concatenate([_tile_gain(b_qn, B_Q_HEADS, SCALE), _tile_gain(b_kn, B_KV_HEADS)]),
                rope=axial)
    v_b = proj(xn, w_in, b_off + (B_Q_HEADS + B_KV_HEADS) * hd, B_KV_HEADS * hd)
    idx_r, idx_c, valid = na_index
    table = jnp.where(valid[None], a_rpb.astype(F32)[:, idx_r, idx_c], NEG)
    oa = neighbourhood_attention(qk_a, v_a, table)
    ob = dense_gqa(qk_b, v_b)
    return matmul_residual([oa, ob], w_out, x)


def _layer_cd(x, xn, w_in, w_out, c_qn, c_kn, c_sink, d_qn, d_kn, partial):
    hd = HEAD_DIM
    c_qk_w = (C_Q_HEADS + C_KV_HEADS) * hd
    qk_c = proj(xn, w_in, 0, c_qk_w,
                gain=jnp.concatenate([_tile_gain(c_qn, C_Q_HEADS, SCALE), _tile_gain(c_kn, C_KV_HEADS)]),
                rope=partial)
    v_c = proj(xn, w_in, c_qk_w, C_KV_HEADS * hd)
    d_off = c_qk_w + C_KV_HEADS * hd
    d_qk_w = (D_Q_HEADS + D_SLOTS) * hd
    qk_d = proj(xn, w_in, d_off, d_qk_w,
                gain=jnp.concatenate([_tile_gain(d_qn, D_Q_HEADS, SCALE), _tile_gain(d_kn, D_SLOTS)]),
                rope=partial)
    v_d = proj(xn, w_in, d_off + d_qk_w, D_SLOTS * hd)
    oc = window_gqa_sink(qk_c, v_c, c_sink)
    od = dilated_attention(qk_d, v_d)
    return matmul_residual([oc, od], w_out, x)


def kernel(x, attn_norm, ffn_norm, ab_w_in, ab_w_out, a_q_norm, a_k_norm, a_rpb, b_q_norm, b_k_norm,
           cd_w_in, cd_w_out, c_q_norm, c_k_norm, c_sink, d_q_norm, d_k_norm, w_gate, w_up, w_down):
    axial = _rope_tables("axial")
    partial = _rope_tables("partial")
    na_index = _na_bias_index()
    x = x.reshape(M_ROWS, D_MODEL)
    for layer in range(DEPTH):
        j = layer // 2
        xn = rmsnorm(x, attn_norm[layer])
        if layer % 2 == 0:
            x = _layer_ab(x, xn, ab_w_in[j].astype(BF16), ab_w_out[j].astype(BF16), a_q_norm[j], a_k_norm[j],
                          a_rpb[j], b_q_norm[j], b_k_norm[j], axial, na_index)
        else:
            x = _layer_cd(x, xn, cd_w_in[j].astype(BF16), cd_w_out[j].astype(BF16), c_q_norm[j], c_k_norm[j],
                          c_sink[j], d_q_norm[j], d_k_norm[j], partial)
        xn = rmsnorm(x, ffn_norm[layer])
        h = gate_up(xn, w_gate[layer].astype(BF16), w_up[layer].astype(BF16))
        x = matmul_residual([h], w_down[layer].astype(BF16), x)
    return x.reshape(BATCH, SEQ, D_MODEL)
```

```python
import functools
import math

import numpy as np
import jax
import jax.numpy as jnp
from jax import lax
from jax.experimental import pallas as pl
from jax.experimental.pallas import tpu as pltpu

D_MODEL = 2048
BATCH = 2
SEQ = 4096
DEPTH = 4
HEAD_DIM = 128
GRID_W = 64
GRID_ROWS = SEQ // GRID_W
EPS = 1e-6
A_HEADS = 8
NA_ROWS = 8
NA_COLS = 16
B_Q_HEADS = 8
B_KV_HEADS = 2
AXIAL_THETA = 10000.0
C_Q_HEADS = 12
C_KV_HEADS = 4
C_WINDOW = 128
D_DILATIONS = ((128, 1), (512, 4), (2048, 16))
D_GROUPS = len(D_DILATIONS)
D_SLOTS = 4
D_Q_HEADS = D_GROUPS * D_SLOTS
ROPE_THETA = 500000.0
ROPE_DIMS = HEAD_DIM // 4
FFN_HIDDEN = ((-(-8 * D_MODEL // 3)) + 255) // 256 * 256
M_ROWS = BATCH * SEQ
SCALE = HEAD_DIM ** -0.5
NEG = -1e30

VMEM_LIMIT_BYTES = 52 * 1024 * 1024
MXU_N = 256

F32 = jnp.float32
BF16 = jnp.bfloat16
NT_DIMS = (((1,), (1,)), ((), ()))


def _params(**kw):
    return pltpu.CompilerParams(vmem_limit_bytes=VMEM_LIMIT_BYTES, **kw)


def _rmsnorm_body(x_ref, g_ref, o_ref):
    x = x_ref[...]
    ms = jnp.mean(x * x, axis=-1, keepdims=True)
    o_ref[...] = (x * lax.rsqrt(ms + EPS) * g_ref[...]).astype(o_ref.dtype)


def rmsnorm(x, g, tm=512):
    m, d = x.shape
    return pl.pallas_call(
        _rmsnorm_body,
        grid=(m // tm,),
        in_specs=[pl.BlockSpec((tm, d), lambda i: (i, 0)), pl.BlockSpec((1, d), lambda i: (0, 0))],
        out_specs=pl.BlockSpec((tm, d), lambda i: (i, 0)),
        out_shape=jax.ShapeDtypeStruct((m, d), BF16),
        compiler_params=_params(),
        name="rmsnorm",
    )(x, g.reshape(1, d))


HALF_LANES = HEAD_DIM // 2
AXIAL_MOVES = ((96, 32, 64), (32, 64, 96))
PARTIAL_MOVES = ((112, 16, 64), (48, 64, 80))


def _moves_to_perm(moves):
    perm = np.arange(HEAD_DIM)
    for shift, lo, hi in moves:
        perm[lo:hi] = (np.arange(lo, hi) - shift) % HEAD_DIM
    assert sorted(perm.tolist()) == list(range(HEAD_DIM))
    return perm


def _permute_lanes(w, moves):
    lane = lax.broadcasted_iota(jnp.int32, w.shape, 1)
    out = w
    for shift, lo, hi in moves:
        out = jnp.where((lane >= lo) & (lane < hi), pltpu.roll(w, shift, 1), out)
    return out


PROJ_TM = 1024
PROJ_TN = 512
PROJ_SUB = 256


def _in_proj_body(sig_ref, x_ref, w_ref, g_ref, cos_ref, sin_ref, o_ref, wbf_ref, *, sigs, moves):
    n = pl.program_id(0)
    m = pl.program_id(1)
    heads = PROJ_TN // HEAD_DIM

    def cast_weights(kinds):
        for h in range(heads):
            cols = slice(h * HEAD_DIM, (h + 1) * HEAD_DIM)
            w = w_ref[:, cols]
            if kinds[h] == "rope":
                w = _permute_lanes(w, moves)
            wbf_ref[:, cols] = w.astype(BF16)

    def compute(kinds):
        for r in range(PROJ_TM // PROJ_SUB):
            rows = slice(r * PROJ_SUB, (r + 1) * PROJ_SUB)
            x = x_ref[rows, :]
            for p in range(PROJ_TN // MXU_N):
                acc = jnp.dot(x, wbf_ref[:, p * MXU_N:(p + 1) * MXU_N], preferred_element_type=F32)
                for hh in range(MXU_N // HEAD_DIM):
                    h = p * (MXU_N // HEAD_DIM) + hh
                    cols = slice(h * HEAD_DIM, (h + 1) * HEAD_DIM)
                    y = acc[:, hh * HEAD_DIM:(hh + 1) * HEAD_DIM]
                    if kinds[h] != "plain":
                        ms = jnp.mean(y * y, axis=-1, keepdims=True)
                        y = y * lax.rsqrt(ms + EPS) * g_ref[:, cols]
                    if kinds[h] == "rope":
                        y = y * cos_ref[rows, :] + pltpu.roll(y, HALF_LANES, 1) * sin_ref[rows, :]
                    o_ref[rows, cols] = y.astype(o_ref.dtype)

    for sid, kinds in enumerate(sigs):
        @pl.when(sig_ref[n] == sid)
        def _(kinds=kinds):
            @pl.when(m == 0)
            def _():
                cast_weights(kinds)

            compute(kinds)


def in_proj(xn, w, li, kinds, gains, rope):
    m, k = xn.shape
    n = w.shape[2]
    heads = PROJ_TN // HEAD_DIM
    tiles = [tuple(kinds[i:i + heads]) for i in range(0, len(kinds), heads)]
    sigs = tuple(dict.fromkeys(tiles))
    sig_ids = jnp.asarray([sigs.index(t) for t in tiles], jnp.int32)
    moves, cos, sin = rope
    pos_blocks = SEQ // PROJ_TM
    return pl.pallas_call(
        functools.partial(_in_proj_body, sigs=sigs, moves=moves),
        grid=(n // PROJ_TN, m // PROJ_TM),
        in_specs=[pl.BlockSpec(memory_space=pltpu.SMEM),
                  pl.BlockSpec((PROJ_TM, k), lambda j, i: (i, 0)),
                  pl.BlockSpec((None, k, PROJ_TN), lambda j, i: (li, 0, j)),
                  pl.BlockSpec((1, PROJ_TN), lambda j, i: (0, j)),
                  pl.BlockSpec((PROJ_TM, HEAD_DIM), lambda j, i: (i % pos_blocks, 0)),
                  pl.BlockSpec((PROJ_TM, HEAD_DIM), lambda j, i: (i % pos_blocks, 0))],
        out_specs=pl.BlockSpec((PROJ_TM, PROJ_TN), lambda j, i: (i, j)),
        out_shape=jax.ShapeDtypeStruct((m, n), BF16),
        scratch_shapes=[pltpu.VMEM((k, PROJ_TN), BF16)],
        compiler_params=_params(),
        name="in_proj",
    )(sig_ids, xn, w, gains, cos, sin)


RES_TM = 512
RES_TN = 512
RES_SUB = 256


def _mm_res_body(*refs, widths):
    n_in = len(widths)
    a_refs = refs[:n_in]
    w_ref, x_ref, o_ref, wbf_ref = refs[n_in:]

    @pl.when(pl.program_id(1) == 0)
    def _():
        wbf_ref[...] = w_ref[...].astype(BF16)

    for r in range(RES_TM // RES_SUB):
        rows = slice(r * RES_SUB, (r + 1) * RES_SUB)
        for p in range(RES_TN // MXU_N):
            cols = slice(p * MXU_N, (p + 1) * MXU_N)
            acc = x_ref[rows, cols]
            k0 = 0
            for a_ref, ka in zip(a_refs, widths):
                acc = acc + jnp.dot(a_ref[rows, :], wbf_ref[k0:k0 + ka, cols], preferred_element_type=F32)
                k0 += ka
            o_ref[rows, cols] = acc


def matmul_residual(a_list, w, li, x):
    m, n = x.shape
    widths = tuple(a.shape[1] for a in a_list)
    kt = sum(widths)
    assert kt == w.shape[1]
    in_specs = [pl.BlockSpec((RES_TM, ka), lambda j, i: (i, 0)) for ka in widths]
    in_specs += [pl.BlockSpec((None, kt, RES_TN), lambda j, i: (li, 0, j)),
                 pl.BlockSpec((RES_TM, RES_TN), lambda j, i: (i, j))]
    return pl.pallas_call(
        functools.partial(_mm_res_body, widths=widths),
        grid=(n // RES_TN, m // RES_TM),
        in_specs=in_specs,
        out_specs=pl.BlockSpec((RES_TM, RES_TN), lambda j, i: (i, j)),
        out_shape=jax.ShapeDtypeStruct((m, n), F32),
        scratch_shapes=[pltpu.VMEM((kt, RES_TN), BF16)],
        compiler_params=_params(),
        name="matmul_residual",
    )(*a_list, w, x)


FFN_TM = 1024
FFN_TN = 512
FFN_SUB = 256


def _gate_up_body(x_ref, wg_ref, wu_ref, o_ref, wgb_ref, wub_ref):
    @pl.when(pl.program_id(1) == 0)
    def _():
        wgb_ref[...] = wg_ref[...].astype(BF16)
        wub_ref[...] = wu_ref[...].astype(BF16)

    for r in range(FFN_TM // FFN_SUB):
        rows = slice(r * FFN_SUB, (r + 1) * FFN_SUB)
        x = x_ref[rows, :]
        for p in range(FFN_TN // MXU_N):
            cols = slice(p * MXU_N, (p + 1) * MXU_N)
            g = jnp.dot(x, wgb_ref[:, cols], preferred_element_type=F32)
            u = jnp.dot(x, wub_ref[:, cols], preferred_element_type=F32)
            o_ref[rows, cols] = (g * (1.0 / (1.0 + jnp.exp(-g))) * u).astype(o_ref.dtype)


def gate_up(xn, wg, wu, li):
    m, k = xn.shape
    n = wg.shape[2]
    return pl.pallas_call(
        _gate_up_body,
        grid=(n // FFN_TN, m // FFN_TM),
        in_specs=[pl.BlockSpec((FFN_TM, k), lambda j, i: (i, 0)),
                  pl.BlockSpec((None, k, FFN_TN), lambda j, i: (li, 0, j)),
                  pl.BlockSpec((None, k, FFN_TN), lambda j, i: (li, 0, j))],
        out_specs=pl.BlockSpec((FFN_TM, FFN_TN), lambda j, i: (i, j)),
        out_shape=jax.ShapeDtypeStruct((m, n), BF16),
        scratch_shapes=[pltpu.VMEM((k, FFN_TN), BF16), pltpu.VMEM((k, FFN_TN), BF16)],
        compiler_params=_params(),
        name="gate_up",
    )(xn, wg, wu)


AB_QA, AB_KA, AB_VA = 0, A_HEADS, 2 * A_HEADS
AB_QB = 3 * A_HEADS
AB_KB = AB_QB + B_Q_HEADS
AB_VB = AB_KB + B_KV_HEADS
AB_KINDS = ("norm",) * (2 * A_HEADS) + ("plain",) * A_HEADS + ("rope",) * (B_Q_HEADS + B_KV_HEADS) \
    + ("plain",) * B_KV_HEADS
CD_QC = 0
CD_KC = C_Q_HEADS
CD_VC = CD_KC + C_KV_HEADS
CD_QD = CD_VC + C_KV_HEADS
CD_KD = CD_QD + D_Q_HEADS
CD_VD = CD_KD + D_SLOTS
CD_KINDS = ("rope",) * (C_Q_HEADS + C_KV_HEADS) + ("plain",) * C_KV_HEADS \
    + ("rope",) * (D_Q_HEADS + D_SLOTS) + ("plain",) * D_SLOTS


NA_QROWS = 2
NA_WIN_ROWS = 10
NA_Q = NA_QROWS * GRID_W
NA_KEYS = NA_WIN_ROWS * GRID_W
NA_BLOCKS = GRID_ROWS // NA_QROWS
NA_VARIANT_BLOCKS = (0, 1, 2, NA_BLOCKS - 2, NA_BLOCKS - 1)


def _na_win_start(rb):
    return np.clip(NA_QROWS * rb - NA_ROWS // 2, 0, GRID_ROWS - NA_WIN_ROWS)


def _na_bias_table(rpb):
    n_dr = 2 * NA_ROWS - 1
    n_dc = 2 * NA_COLS - 1
    rpb = rpb.astype(F32)
    padded = jnp.pad(rpb, ((0, 0), (0, 0), (GRID_W, GRID_W)))
    toep = jnp.stack([padded[:, :, GRID_W + NA_COLS - 1 - c: 2 * GRID_W + NA_COLS - 1 - c]
                      for c in range(GRID_W)], axis=2)
    cols = np.arange(GRID_W)
    c0 = np.clip(cols - NA_COLS // 2, 0, GRID_W - NA_COLS)
    col_ok = (cols[None, :] >= c0[:, None]) & (cols[None, :] < c0[:, None] + NA_COLS)
    toep = jnp.where(col_ok[None, None], toep, NEG)
    toep = jnp.concatenate([toep, jnp.full((A_HEADS, 1, GRID_W, GRID_W), NEG, F32)], axis=1)
    dr_idx = np.full((len(NA_VARIANT_BLOCKS), NA_QROWS, NA_WIN_ROWS), n_dr, np.int32)
    for v, rb in enumerate(NA_VARIANT_BLOCKS):
        ws = _na_win_start(rb)
        for qr in range(NA_QROWS):
            qrow = NA_QROWS * rb + qr
            r0 = np.clip(qrow - NA_ROWS // 2, 0, GRID_ROWS - NA_ROWS)
            for kr in range(NA_WIN_ROWS):
                krow = ws + kr
                if r0 <= krow < r0 + NA_ROWS:
                    dr_idx[v, qr, kr] = krow - qrow + NA_ROWS - 1
    assert n_dc == rpb.shape[2] and n_dr == rpb.shape[1]
    slabs = [toep[:, int(d)] for d in dr_idx.reshape(-1)]
    t = jnp.stack(slabs, axis=1).reshape(A_HEADS, len(NA_VARIANT_BLOCKS), NA_QROWS, NA_WIN_ROWS, GRID_W, GRID_W)
    t = t.transpose(0, 1, 2, 4, 3, 5)
    return t.reshape(A_HEADS, len(NA_VARIANT_BLOCKS), NA_Q, NA_KEYS)


def _na_body(q_ref, k_ref, v_ref, t_ref, o_ref):
    def step(rb, carry):
        ws = jnp.clip(NA_QROWS * rb - NA_ROWS // 2, 0, GRID_ROWS - NA_WIN_ROWS)
        koff = pl.multiple_of(ws * GRID_W, GRID_W)
        qoff = pl.multiple_of(rb * NA_Q, NA_Q)
        variant = jnp.where(rb < 2, rb, jnp.where(rb >= NA_BLOCKS - 2, rb - (NA_BLOCKS - 5), 2))
        q = q_ref[pl.ds(qoff, NA_Q), :]
        k = k_ref[pl.ds(koff, NA_KEYS), :]
        v = v_ref[pl.ds(koff, NA_KEYS), :]
        s = lax.dot_general(q, k, NT_DIMS, preferred_element_type=F32) + t_ref[variant]
        m = jnp.max(s, axis=-1, keepdims=True)
        p = jnp.exp(s - m)
        l = jnp.sum(p, axis=-1, keepdims=True)
        o = jnp.dot(p.astype(BF16), v, preferred_element_type=F32)
        o_ref[pl.ds(qoff, NA_Q), :] = (o / l).astype(o_ref.dtype)
        return carry

    lax.fori_loop(0, NA_BLOCKS, step, 0)


def neighbourhood_attention(h, table):
    nv = len(NA_VARIANT_BLOCKS)
    blk = (SEQ, HEAD_DIM)
    return pl.pallas_call(
        _na_body,
        grid=(BATCH, A_HEADS),
        in_specs=[pl.BlockSpec(blk, lambda b, hd: (b, AB_QA + hd)),
                  pl.BlockSpec(blk, lambda b, hd: (b, AB_KA + hd)),
                  pl.BlockSpec(blk, lambda b, hd: (b, AB_VA + hd)),
                  pl.BlockSpec((None, nv, NA_Q, NA_KEYS), lambda b, hd: (hd, 0, 0, 0))],
        out_specs=pl.BlockSpec(blk, lambda b, hd: (b, hd)),
        out_shape=jax.ShapeDtypeStruct((M_ROWS, A_HEADS * HEAD_DIM), BF16),
        compiler_params=_params(),
        name="neighbourhood_attention",
    )(h, h, h, table)


B_GROUP = B_Q_HEADS // B_KV_HEADS


def _dense_body(q_ref, k_ref, v_ref, o_ref):
    k = k_ref[...]
    v = v_ref[...]
    for g in range(B_GROUP):
        sl = slice(g * HEAD_DIM, (g + 1) * HEAD_DIM)
        s = lax.dot_general(q_ref[:, sl], k, NT_DIMS, preferred_element_type=F32)
        m = jnp.max(s, axis=-1, keepdims=True)
        p = jnp.exp(s - m)
        l = jnp.sum(p, axis=-1, keepdims=True)
        o = jnp.dot(p.astype(BF16), v, preferred_element_type=F32)
        o_ref[:, sl] = (o / l).astype(o_ref.dtype)


def dense_gqa(h, tq=256):
    nq = SEQ // tq
    gw = B_GROUP * HEAD_DIM
    assert AB_QB % B_GROUP == 0
    q_blk0 = AB_QB // B_GROUP
    blk = (SEQ, HEAD_DIM)
    return pl.pallas_call(
        _dense_body,
        grid=(BATCH, B_KV_HEADS, nq),
        in_specs=[pl.BlockSpec((tq, gw), lambda b, kv, i: (b * nq + i, q_blk0 + kv)),
                  pl.BlockSpec(blk, lambda b, kv, i: (b, AB_KB + kv)),
                  pl.BlockSpec(blk, lambda b, kv, i: (b, AB_VB + kv))],
        out_specs=pl.BlockSpec((tq, gw), lambda b, kv, i: (b * nq + i, kv)),
        out_shape=jax.ShapeDtypeStruct((M_ROWS, B_Q_HEADS * HEAD_DIM), BF16),
        compiler_params=_params(),
        name="dense_gqa",
    )(h, h, h)


QBLK = 128
N_QBLK = SEQ // QBLK


def _band_logits(q, k_ref, i, span, stride):
    win = QBLK + 2 * span
    ws = jnp.clip(i * QBLK - span, 0, SEQ - win)
    koff = pl.multiple_of(ws, 64)
    s = lax.dot_general(q, k_ref[pl.ds(koff, win), :], NT_DIMS, preferred_element_type=F32)
    col = lax.broadcasted_iota(jnp.int32, (QBLK, win), 1)
    row = lax.broadcasted_iota(jnp.int32, (QBLK, win), 0)
    dist = col - row + (ws - i * QBLK)
    ok = jnp.abs(dist) <= span
    if stride > 1:
        ok = ok & ((dist & (stride - 1)) == 0)
    return jnp.where(ok, s, NEG), koff, win


C_GROUP = C_Q_HEADS // C_KV_HEADS


def _window_sink_body(sink_ref, q_ref, k_ref, v_ref, o_ref):
    kv = pl.program_id(1)

    def step(i, carry):
        qoff = pl.multiple_of(i * QBLK, QBLK)
        for g in range(C_GROUP):
            sl = slice(g * HEAD_DIM, (g + 1) * HEAD_DIM)
            s, koff, win = _band_logits(q_ref[pl.ds(qoff, QBLK), sl], k_ref, i, C_WINDOW, 1)
            sink = sink_ref[kv * C_GROUP + g]
            m = jnp.maximum(jnp.max(s, axis=-1, keepdims=True), sink)
            p = jnp.exp(s - m)
            l = jnp.sum(p, axis=-1, keepdims=True) + jnp.exp(sink - m)
            o = jnp.dot(p.astype(BF16), v_ref[pl.ds(koff, win), :], preferred_element_type=F32)
            o_ref[pl.ds(qoff, QBLK), sl] = (o / l).astype(o_ref.dtype)
        return carry

    lax.fori_loop(0, N_QBLK, step, 0)


def window_gqa_sink(h, sink):
    gw = C_GROUP * HEAD_DIM
    assert CD_QC % C_GROUP == 0
    q_blk0 = CD_QC // C_GROUP
    blk = (SEQ, HEAD_DIM)
    return pl.pallas_call(
        _window_sink_body,
        grid=(BATCH, C_KV_HEADS),
        in_specs=[pl.BlockSpec(memory_space=pltpu.SMEM),
                  pl.BlockSpec((SEQ, gw), lambda b, kv: (b, q_blk0 + kv)),
                  pl.BlockSpec(blk, lambda b, kv: (b, CD_KC + kv)),
                  pl.BlockSpec(blk, lambda b, kv: (b, CD_VC + kv))],
        out_specs=pl.BlockSpec((SEQ, gw), lambda b, kv: (b, kv)),
        out_shape=jax.ShapeDtypeStruct((M_ROWS, C_Q_HEADS * HEAD_DIM), BF16),
        compiler_params=_params(),
        name="window_gqa_sink",
    )(sink.astype(F32), h, h, h)


def _dilated_body(q0_ref, q1_ref, q2_ref, k_ref, v_ref, o_ref):
    q_refs = (q0_ref, q1_ref, q2_ref)

    def step(i, carry):
        qoff = pl.multiple_of(i * QBLK, QBLK)
        ms, dens, nums = [], [], []
        for q_ref, (w, d) in zip(q_refs, D_DILATIONS):
            span = ((w // 2) // d) * d
            s, koff, win = _band_logits(q_ref[pl.ds(qoff, QBLK), :], k_ref, i, span, d)
            m = jnp.max(s, axis=-1, keepdims=True)
            p = jnp.exp(s - m)
            ms.append(m)
            dens.append(jnp.sum(p, axis=-1, keepdims=True))
            nums.append(jnp.dot(p.astype(BF16), v_ref[pl.ds(koff, win), :], preferred_element_type=F32))
        m_all = jnp.maximum(jnp.maximum(ms[0], ms[1]), ms[2])
        ws = [jnp.exp(m - m_all) for m in ms]
        num = ws[0] * nums[0] + ws[1] * nums[1] + ws[2] * nums[2]
        den = ws[0] * dens[0] + ws[1] * dens[1] + ws[2] * dens[2]
        o_ref[pl.ds(qoff, QBLK), :] = (num / den).astype(o_ref.dtype)
        return carry

    lax.fori_loop(0, N_QBLK, step, 0)


def dilated_attention(h):
    blk = (SEQ, HEAD_DIM)
    return pl.pallas_call(
        _dilated_body,
        grid=(BATCH, D_SLOTS),
        in_specs=[pl.BlockSpec(blk, lambda b, s: (b, CD_QD + s)),
                  pl.BlockSpec(blk, lambda b, s: (b, CD_QD + D_SLOTS + s)),
                  pl.BlockSpec(blk, lambda b, s: (b, CD_QD + 2 * D_SLOTS + s)),
                  pl.BlockSpec(blk, lambda b, s: (b, CD_KD + s)),
                  pl.BlockSpec(blk, lambda b, s: (b, CD_VD + s))],
        out_specs=pl.BlockSpec(blk, lambda b, s: (b, s)),
        out_shape=jax.ShapeDtypeStruct((M_ROWS, D_SLOTS * HEAD_DIM), BF16),
        compiler_params=_params(),
        name="dilated_attention",
    )(h, h, h, h, h)


def _rope_tables(kind):
    pos = np.arange(SEQ, dtype=np.float64)
    cos = np.ones((SEQ, HEAD_DIM), np.float64)
    sin = np.zeros((SEQ, HEAD_DIM), np.float64)

    def fill(start, r, p, theta):
        half = r // 2
        inv = np.exp(-math.log(theta) * np.arange(half, dtype=np.float64) * (2.0 / r))
        ang = p[:, None] * inv[None, :]
        cos[:, start:start + half] = np.cos(ang)
        cos[:, start + half:start + r] = np.cos(ang)
        sin[:, start:start + half] = -np.sin(ang)
        sin[:, start + half:start + r] = np.sin(ang)

    if kind == "axial":
        hw = HEAD_DIM // 2
        fill(0, hw, np.floor(pos / GRID_W), AXIAL_THETA)
        fill(hw, hw, pos % GRID_W, AXIAL_THETA)
        moves = AXIAL_MOVES
    else:
        fill(0, ROPE_DIMS, pos, ROPE_THETA)
        moves = PARTIAL_MOVES
    perm = _moves_to_perm(moves)
    return moves, jnp.asarray(cos[:, perm], dtype=F32), jnp.asarray(sin[:, perm], dtype=F32)


def _gain_row(parts):
    rows = []
    for g, heads, scale, perm in parts:
        if g is None:
            g = jnp.ones((HEAD_DIM,), F32)
        g = g.astype(F32) * scale
        if perm is not None:
            g = g[perm]
        rows.append(jnp.tile(g, heads))
    return jnp.concatenate(rows).reshape(1, -1)


def _layer_ab(x, xn, w_in, w_out, li, a_qn, a_kn, a_rpb, b_qn, b_kn, axial):
    perm = _moves_to_perm(axial[0])
    gains = _gain_row([(a_qn, A_HEADS, SCALE, None), (a_kn, A_HEADS, 1.0, None), (None, A_HEADS, 1.0, None),
                       (b_qn, B_Q_HEADS, SCALE, perm), (b_kn, B_KV_HEADS, 1.0, perm),
                       (None, B_KV_HEADS, 1.0, None)])
    h = in_proj(xn, w_in, li, AB_KINDS, gains, axial)
    oa = neighbourhood_attention(h, _na_bias_table(a_rpb))
    ob = dense_gqa(h)
    return matmul_residual([oa, ob], w_out, li, x)


def _layer_cd(x, xn, w_in, w_out, li, c_qn, c_kn, c_sink, d_qn, d_kn, partial):
    perm = _moves_to_perm(partial[0])
    gains = _gain_row([(c_qn, C_Q_HEADS, SCALE, perm), (c_kn, C_KV_HEADS, 1.0, perm), (None, C_KV_HEADS, 1.0, None),
                       (d_qn, D_Q_HEADS, SCALE, perm), (d_kn, D_SLOTS, 1.0, perm), (None, D_SLOTS, 1.0, None)])
    h = in_proj(xn, w_in, li, CD_KINDS, gains, partial)
    oc = window_gqa_sink(h, c_sink)
    od = dilated_attention(h)
    return matmul_residual([oc, od], w_out, li, x)


def kernel(x, attn_norm, ffn_norm, ab_w_in, ab_w_out, a_q_norm, a_k_norm, a_rpb, b_q_norm, b_k_norm,
           cd_w_in, cd_w_out, c_q_norm, c_k_norm, c_sink, d_q_norm, d_k_norm, w_gate, w_up, w_down):
    axial = _rope_tables("axial")
    partial = _rope_tables("partial")
    x = x.reshape(M_ROWS, D_MODEL)
    for layer in range(DEPTH):
        j = layer // 2
        xn = rmsnorm(x, attn_norm[layer])
        if layer % 2 == 0:
            x = _layer_ab(x, xn, ab_w_in, ab_w_out, j, a_q_norm[j], a_k_norm[j], a_rpb[j],
                          b_q_norm[j], b_k_norm[j], axial)
        else:
            x = _layer_cd(x, xn, cd_w_in, cd_w_out, j, c_q_norm[j], c_k_norm[j], c_sink[j],
                          d_q_norm[j], d_k_norm[j], partial)
        xn = rmsnorm(x, ffn_norm[layer])
        hidden = gate_up(xn, w_gate, w_up, layer)
        x = matmul_residual([hidden], w_down, layer, x)
    return x.reshape(BATCH, SEQ, D_MODEL)
```

```python
import functools
import math

import numpy as np
import jax
import jax.numpy as jnp
from jax import lax
from jax.experimental import pallas as pl
from jax.experimental.pallas import tpu as pltpu

D_MODEL = 2048
BATCH = 2
SEQ = 4096
DEPTH = 4
HEAD_DIM = 128
GRID_W = 64
GRID_ROWS = SEQ // GRID_W
EPS = 1e-6
A_HEADS = 8
NA_ROWS = 8
NA_COLS = 16
B_Q_HEADS = 8
B_KV_HEADS = 2
AXIAL_THETA = 10000.0
C_Q_HEADS = 12
C_KV_HEADS = 4
C_WINDOW = 128
D_DILATIONS = ((128, 1), (512, 4), (2048, 16))
D_GROUPS = len(D_DILATIONS)
D_SLOTS = 4
D_Q_HEADS = D_GROUPS * D_SLOTS
ROPE_THETA = 500000.0
ROPE_DIMS = HEAD_DIM // 4
FFN_HIDDEN = ((-(-8 * D_MODEL // 3)) + 255) // 256 * 256
M_ROWS = BATCH * SEQ
SCALE = HEAD_DIM ** -0.5
NEG = -1e30

VMEM_LIMIT_BYTES = 52 * 1024 * 1024
MXU_N = 256

F32 = jnp.float32
BF16 = jnp.bfloat16
NT_DIMS = (((1,), (1,)), ((), ()))


def _params(**kw):
    return pltpu.CompilerParams(vmem_limit_bytes=VMEM_LIMIT_BYTES, **kw)


def _rmsnorm_body(x_ref, g_ref, o_ref):
    x = x_ref[...]
    ms = jnp.mean(x * x, axis=-1, keepdims=True)
    o_ref[...] = (x * lax.rsqrt(ms + EPS) * g_ref[...]).astype(o_ref.dtype)


def rmsnorm(x, g, tm=512):
    m, d = x.shape
    return pl.pallas_call(
        _rmsnorm_body,
        grid=(m // tm,),
        in_specs=[pl.BlockSpec((tm, d), lambda i: (i, 0)), pl.BlockSpec((1, d), lambda i: (0, 0))],
        out_specs=pl.BlockSpec((tm, d), lambda i: (i, 0)),
        out_shape=jax.ShapeDtypeStruct((m, d), BF16),
        compiler_params=_params(),
        name="rmsnorm",
    )(x, g.reshape(1, d))


HALF_LANES = HEAD_DIM // 2
AXIAL_MOVES = ((96, 32, 64), (32, 64, 96))
PARTIAL_MOVES = ((112, 16, 64), (48, 64, 80))


def _moves_to_perm(moves):
    perm = np.arange(HEAD_DIM)
    for shift, lo, hi in moves:
        perm[lo:hi] = (np.arange(lo, hi) - shift) % HEAD_DIM
    assert sorted(perm.tolist()) == list(range(HEAD_DIM))
    return perm


def _permute_lanes(w, moves):
    lane = lax.broadcasted_iota(jnp.int32, w.shape, 1)
    out = w
    for shift, lo, hi in moves:
        out = jnp.where((lane >= lo) & (lane < hi), pltpu.roll(w, shift, 1), out)
    return out


PROJ_TM = 1024
PROJ_TN = 512
PROJ_SUB = 256


def _in_proj_body(sig_ref, x_ref, w_ref, g_ref, cos_ref, sin_ref, o_ref, wbf_ref, *, sigs, moves):
    n = pl.program_id(0)
    m = pl.program_id(1)
    heads = PROJ_TN // HEAD_DIM

    def cast_weights(kinds):
        for h in range(heads):
            cols = slice(h * HEAD_DIM, (h + 1) * HEAD_DIM)
            w = w_ref[:, cols]
            if kinds[h] == "rope":
                w = _permute_lanes(w, moves)
            wbf_ref[:, cols] = w.astype(BF16)

    def compute(kinds):
        for r in range(PROJ_TM // PROJ_SUB):
            rows = slice(r * PROJ_SUB, (r + 1) * PROJ_SUB)
            x = x_ref[rows, :]
            for p in range(PROJ_TN // MXU_N):
                acc = jnp.dot(x, wbf_ref[:, p * MXU_N:(p + 1) * MXU_N], preferred_element_type=F32)
                for hh in range(MXU_N // HEAD_DIM):
                    h = p * (MXU_N // HEAD_DIM) + hh
                    cols = slice(h * HEAD_DIM, (h + 1) * HEAD_DIM)
                    y = acc[:, hh * HEAD_DIM:(hh + 1) * HEAD_DIM]
                    if kinds[h] != "plain":
                        ms = jnp.mean(y * y, axis=-1, keepdims=True)
                        y = y * lax.rsqrt(ms + EPS) * g_ref[:, cols]
                    if kinds[h] == "rope":
                        y = y * cos_ref[rows, :] + pltpu.roll(y, HALF_LANES, 1) * sin_ref[rows, :]
                    o_ref[rows, cols] = y.astype(o_ref.dtype)

    for sid, kinds in enumerate(sigs):
        @pl.when(sig_ref[n] == sid)
        def _(kinds=kinds):
            @pl.when(m == 0)
            def _():
                cast_weights(kinds)

            compute(kinds)


def in_proj(xn, w, li, kinds, gains, rope):
    m, k = xn.shape
    n = w.shape[2]
    heads = PROJ_TN // HEAD_DIM
    tiles = [tuple(kinds[i:i + heads]) for i in range(0, len(kinds), heads)]
    sigs = tuple(dict.fromkeys(tiles))
    sig_ids = jnp.asarray([sigs.index(t) for t in tiles], jnp.int32)
    moves, cos, sin = rope
    pos_blocks = SEQ // PROJ_TM
    return pl.pallas_call(
        functools.partial(_in_proj_body, sigs=sigs, moves=moves),
        grid=(n // PROJ_TN, m // PROJ_TM),
        in_specs=[pl.BlockSpec(memory_space=pltpu.SMEM),
                  pl.BlockSpec((PROJ_TM, k), lambda j, i: (i, 0)),
                  pl.BlockSpec((None, k, PROJ_TN), lambda j, i: (li, 0, j)),
                  pl.BlockSpec((1, PROJ_TN), lambda j, i: (0, j)),
                  pl.BlockSpec((PROJ_TM, HEAD_DIM), lambda j, i: (i % pos_blocks, 0)),
                  pl.BlockSpec((PROJ_TM, HEAD_DIM), lambda j, i: (i % pos_blocks, 0))],
        out_specs=pl.BlockSpec((PROJ_TM, PROJ_TN), lambda j, i: (i, j)),
        out_shape=jax.ShapeDtypeStruct((m, n), BF16),
        scratch_shapes=[pltpu.VMEM((k, PROJ_TN), BF16)],
        compiler_params=_params(),
        name="in_proj",
    )(sig_ids, xn, w, gains, cos, sin)


RES_TM = 512
RES_SUB = 256
RES_VMEM_WEIGHT_BYTES = 12 * 1024 * 1024


def _res_tn(kt, n):
    tn = n
    while kt * tn * 4 > RES_VMEM_WEIGHT_BYTES and tn % (2 * MXU_N) == 0:
        tn //= 2
    return tn


def _mm_res_body(*refs, widths, tn):
    n_in = len(widths)
    a_refs = refs[:n_in]
    w_ref, x_ref, o_ref, wbf_ref = refs[n_in:]

    @pl.when(pl.program_id(1) == 0)
    def _():
        wbf_ref[...] = w_ref[...].astype(BF16)

    for r in range(RES_TM // RES_SUB):
        rows = slice(r * RES_SUB, (r + 1) * RES_SUB)
        for p in range(tn // MXU_N):
            cols = slice(p * MXU_N, (p + 1) * MXU_N)
            acc = x_ref[rows, cols]
            k0 = 0
            for a_ref, ka in zip(a_refs, widths):
                acc = acc + jnp.dot(a_ref[rows, :], wbf_ref[k0:k0 + ka, cols], preferred_element_type=F32)
                k0 += ka
            o_ref[rows, cols] = acc


def matmul_residual(a_list, w, li, x):
    m, n = x.shape
    widths = tuple(a.shape[1] for a in a_list)
    kt = sum(widths)
    assert kt == w.shape[1]
    tn = _res_tn(kt, n)
    in_specs = [pl.BlockSpec((RES_TM, ka), lambda j, i: (i, 0)) for ka in widths]
    in_specs += [pl.BlockSpec((None, kt, tn), lambda j, i: (li, 0, j)),
                 pl.BlockSpec((RES_TM, tn), lambda j, i: (i, j))]
    return pl.pallas_call(
        functools.partial(_mm_res_body, widths=widths, tn=tn),
        grid=(n // tn, m // RES_TM),
        in_specs=in_specs,
        out_specs=pl.BlockSpec((RES_TM, tn), lambda j, i: (i, j)),
        out_shape=jax.ShapeDtypeStruct((m, n), F32),
        scratch_shapes=[pltpu.VMEM((kt, tn), BF16)],
        compiler_params=_params(),
        name="matmul_residual",
    )(*a_list, w, x)


FFN_TM = 1024
FFN_TN = 512
FFN_SUB = 256


def _gate_up_body(x_ref, wg_ref, wu_ref, o_ref, wgb_ref, wub_ref):
    @pl.when(pl.program_id(1) == 0)
    def _():
        wgb_ref[...] = wg_ref[...].astype(BF16)
        wub_ref[...] = wu_ref[...].astype(BF16)

    for r in range(FFN_TM // FFN_SUB):
        rows = slice(r * FFN_SUB, (r + 1) * FFN_SUB)
        x = x_ref[rows, :]
        for p in range(FFN_TN // MXU_N):
            cols = slice(p * MXU_N, (p + 1) * MXU_N)
            g = jnp.dot(x, wgb_ref[:, cols], preferred_element_type=F32)
            u = jnp.dot(x, wub_ref[:, cols], preferred_element_type=F32)
            o_ref[rows, cols] = (g * (1.0 / (1.0 + jnp.exp(-g))) * u).astype(o_ref.dtype)


def gate_up(xn, wg, wu, li):
    m, k = xn.shape
    n = wg.shape[2]
    return pl.pallas_call(
        _gate_up_body,
        grid=(n // FFN_TN, m // FFN_TM),
        in_specs=[pl.BlockSpec((FFN_TM, k), lambda j, i: (i, 0)),
                  pl.BlockSpec((None, k, FFN_TN), lambda j, i: (li, 0, j)),
                  pl.BlockSpec((None, k, FFN_TN), lambda j, i: (li, 0, j))],
        out_specs=pl.BlockSpec((FFN_TM, FFN_TN), lambda j, i: (i, j)),
        out_shape=jax.ShapeDtypeStruct((m, n), BF16),
        scratch_shapes=[pltpu.VMEM((k, FFN_TN), BF16), pltpu.VMEM((k, FFN_TN), BF16)],
        compiler_params=_params(),
        name="gate_up",
    )(xn, wg, wu)


AB_QA, AB_KA, AB_VA = 0, A_HEADS, 2 * A_HEADS
AB_QB = 3 * A_HEADS
AB_KB = AB_QB + B_Q_HEADS
AB_VB = AB_KB + B_KV_HEADS
AB_KINDS = ("norm",) * (2 * A_HEADS) + ("plain",) * A_HEADS + ("rope",) * (B_Q_HEADS + B_KV_HEADS) \
    + ("plain",) * B_KV_HEADS
CD_QC = 0
CD_KC = C_Q_HEADS
CD_VC = CD_KC + C_KV_HEADS
CD_QD = CD_VC + C_KV_HEADS
CD_KD = CD_QD + D_Q_HEADS
CD_VD = CD_KD + D_SLOTS
CD_KINDS = ("rope",) * (C_Q_HEADS + C_KV_HEADS) + ("plain",) * C_KV_HEADS \
    + ("rope",) * (D_Q_HEADS + D_SLOTS) + ("plain",) * D_SLOTS


NA_QROWS = 2
NA_WIN_ROWS = 10
NA_Q = NA_QROWS * GRID_W
NA_KEYS = NA_WIN_ROWS * GRID_W
NA_BLOCKS = GRID_ROWS // NA_QROWS
NA_VARIANT_BLOCKS = (0, 1, 2, NA_BLOCKS - 2, NA_BLOCKS - 1)


def _na_win_start(rb):
    return np.clip(NA_QROWS * rb - NA_ROWS // 2, 0, GRID_ROWS - NA_WIN_ROWS)


def _na_bias_table(rpb):
    n_dr = 2 * NA_ROWS - 1
    n_dc = 2 * NA_COLS - 1
    rpb = rpb.astype(F32)
    padded = jnp.pad(rpb, ((0, 0), (0, 0), (GRID_W, GRID_W)))
    toep = jnp.stack([padded[:, :, GRID_W + NA_COLS - 1 - c: 2 * GRID_W + NA_COLS - 1 - c]
                      for c in range(GRID_W)], axis=2)
    cols = np.arange(GRID_W)
    c0 = np.clip(cols - NA_COLS // 2, 0, GRID_W - NA_COLS)
    col_ok = (cols[None, :] >= c0[:, None]) & (cols[None, :] < c0[:, None] + NA_COLS)
    toep = jnp.where(col_ok[None, None], toep, NEG)
    toep = jnp.concatenate([toep, jnp.full((A_HEADS, 1, GRID_W, GRID_W), NEG, F32)], axis=1)
    dr_idx = np.full((len(NA_VARIANT_BLOCKS), NA_QROWS, NA_WIN_ROWS), n_dr, np.int32)
    for v, rb in enumerate(NA_VARIANT_BLOCKS):
        ws = _na_win_start(rb)
        for qr in range(NA_QROWS):
            qrow = NA_QROWS * rb + qr
            r0 = np.clip(qrow - NA_ROWS // 2, 0, GRID_ROWS - NA_ROWS)
            for kr in range(NA_WIN_ROWS):
                krow = ws + kr
                if r0 <= krow < r0 + NA_ROWS:
                    dr_idx[v, qr, kr] = krow - qrow + NA_ROWS - 1
    assert n_dc == rpb.shape[2] and n_dr == rpb.shape[1]
    slabs = [toep[:, int(d)] for d in dr_idx.reshape(-1)]
    t = jnp.stack(slabs, axis=1).reshape(A_HEADS, len(NA_VARIANT_BLOCKS), NA_QROWS, NA_WIN_ROWS, GRID_W, GRID_W)
    t = t.transpose(0, 1, 2, 4, 3, 5)
    return t.reshape(A_HEADS, len(NA_VARIANT_BLOCKS), NA_Q, NA_KEYS)


NA_UNROLL = 4


def _na_body(q_ref, k_ref, v_ref, t_ref, o_ref):
    def block(rb):
        ws = jnp.clip(NA_QROWS * rb - NA_ROWS // 2, 0, GRID_ROWS - NA_WIN_ROWS)
        koff = pl.multiple_of(ws * GRID_W, GRID_W)
        qoff = pl.multiple_of(rb * NA_Q, NA_Q)
        variant = jnp.where(rb < 2, rb, jnp.where(rb >= NA_BLOCKS - 2, rb - (NA_BLOCKS - 5), 2))
        q = q_ref[pl.ds(qoff, NA_Q), :]
        k = k_ref[pl.ds(koff, NA_KEYS), :]
        v = v_ref[pl.ds(koff, NA_KEYS), :]
        s = lax.dot_general(q, k, NT_DIMS, preferred_element_type=F32) + t_ref[variant]
        m = jnp.max(s, axis=-1, keepdims=True)
        p = jnp.exp(s - m)
        l = jnp.sum(p, axis=-1, keepdims=True)
        o = jnp.dot(p.astype(BF16), v, preferred_element_type=F32)
        o_ref[pl.ds(qoff, NA_Q), :] = (o / l).astype(o_ref.dtype)

    def step(it, carry):
        for u in range(NA_UNROLL):
            block(it * NA_UNROLL + u)
        return carry

    lax.fori_loop(0, NA_BLOCKS // NA_UNROLL, step, 0)


def neighbourhood_attention(h, table):
    nv = len(NA_VARIANT_BLOCKS)
    blk = (SEQ, HEAD_DIM)
    return pl.pallas_call(
        _na_body,
        grid=(BATCH, A_HEADS),
        in_specs=[pl.BlockSpec(blk, lambda b, hd: (b, AB_QA + hd)),
                  pl.BlockSpec(blk, lambda b, hd: (b, AB_KA + hd)),
                  pl.BlockSpec(blk, lambda b, hd: (b, AB_VA + hd)),
                  pl.BlockSpec((None, nv, NA_Q, NA_KEYS), lambda b, hd: (hd, 0, 0, 0))],
        out_specs=pl.BlockSpec(blk, lambda b, hd: (b, hd)),
        out_shape=jax.ShapeDtypeStruct((M_ROWS, A_HEADS * HEAD_DIM), BF16),
        compiler_params=_params(),
        name="neighbourhood_attention",
    )(h, h, h, table)


B_GROUP = B_Q_HEADS // B_KV_HEADS


def _dense_body(q_ref, k_ref, v_ref, o_ref):
    k = k_ref[...]
    v = v_ref[...]
    for g in range(B_GROUP):
        sl = slice(g * HEAD_DIM, (g + 1) * HEAD_DIM)
        s = lax.dot_general(q_ref[:, sl], k, NT_DIMS, preferred_element_type=F32)
        m = jnp.max(s, axis=-1, keepdims=True)
        p = jnp.exp(s - m)
        l = jnp.sum(p, axis=-1, keepdims=True)
        o = jnp.dot(p.astype(BF16), v, preferred_element_type=F32)
        o_ref[:, sl] = (o / l).astype(o_ref.dtype)


def dense_gqa(h, tq=256):
    nq = SEQ // tq
    gw = B_GROUP * HEAD_DIM
    assert AB_QB % B_GROUP == 0
    q_blk0 = AB_QB // B_GROUP
    blk = (SEQ, HEAD_DIM)
    return pl.pallas_call(
        _dense_body,
        grid=(BATCH, B_KV_HEADS, nq),
        in_specs=[pl.BlockSpec((tq, gw), lambda b, kv, i: (b * nq + i, q_blk0 + kv)),
                  pl.BlockSpec(blk, lambda b, kv, i: (b, AB_KB + kv)),
                  pl.BlockSpec(blk, lambda b, kv, i: (b, AB_VB + kv))],
        out_specs=pl.BlockSpec((tq, gw), lambda b, kv, i: (b * nq + i, kv)),
        out_shape=jax.ShapeDtypeStruct((M_ROWS, B_Q_HEADS * HEAD_DIM), BF16),
        compiler_params=_params(),
        name="dense_gqa",
    )(h, h, h)


QBLK = 128
N_QBLK = SEQ // QBLK


def _band_mask_table(span):
    assert span <= QBLK
    col = np.arange(QBLK + 2 * span)[None, :]
    row = np.arange(QBLK)[:, None]
    t = np.stack([np.where(np.abs(col - row + shift) <= span, 0.0, NEG) for shift in (0, -span, -2 * span)])
    return jnp.asarray(t, F32)


def _band_window(ub, blocks, span):
    start = jnp.clip(ub * QBLK - span, 0, blocks * QBLK - (QBLK + 2 * span))
    variant = jnp.where(ub == 0, 0, jnp.where(ub == blocks - 1, 2, 1))
    return start, variant


C_GROUP = C_Q_HEADS // C_KV_HEADS


C_WIN = QBLK + 2 * C_WINDOW
C_UNROLL = 2


def _window_sink_body(sink_ref, q_ref, k_ref, v_ref, mask_ref, o_ref):
    kv = pl.program_id(1)

    def block(i):
        qoff = pl.multiple_of(i * QBLK, QBLK)
        start, variant = _band_window(i, N_QBLK, C_WINDOW)
        koff = pl.multiple_of(start, QBLK)
        mask = mask_ref[variant]
        q = q_ref[pl.ds(qoff, QBLK), :]
        qs = jnp.concatenate([q[:, g * HEAD_DIM:(g + 1) * HEAD_DIM] for g in range(C_GROUP)], axis=0)
        s = lax.dot_general(qs, k_ref[pl.ds(koff, C_WIN), :], NT_DIMS, preferred_element_type=F32)
        ps, ls = [], []
        for g in range(C_GROUP):
            sg = s[g * QBLK:(g + 1) * QBLK] + mask
            sink = sink_ref[kv * C_GROUP + g]
            m = jnp.maximum(jnp.max(sg, axis=-1, keepdims=True), sink)
            p = jnp.exp(sg - m)
            ls.append(jnp.sum(p, axis=-1, keepdims=True) + jnp.exp(sink - m))
            ps.append(p.astype(BF16))
        o = jnp.dot(jnp.concatenate(ps, axis=0), v_ref[pl.ds(koff, C_WIN), :], preferred_element_type=F32)
        for g in range(C_GROUP):
            o_ref[pl.ds(qoff, QBLK), g * HEAD_DIM:(g + 1) * HEAD_DIM] = (
                o[g * QBLK:(g + 1) * QBLK] / ls[g]).astype(o_ref.dtype)

    def step(it, carry):
        for u in range(C_UNROLL):
            block(it * C_UNROLL + u)
        return carry

    lax.fori_loop(0, N_QBLK // C_UNROLL, step, 0)


def window_gqa_sink(h, sink):
    gw = C_GROUP * HEAD_DIM
    assert CD_QC % C_GROUP == 0
    q_blk0 = CD_QC // C_GROUP
    blk = (SEQ, HEAD_DIM)
    return pl.pallas_call(
        _window_sink_body,
        grid=(BATCH, C_KV_HEADS),
        in_specs=[pl.BlockSpec(memory_space=pltpu.SMEM),
                  pl.BlockSpec((SEQ, gw), lambda b, kv: (b, q_blk0 + kv)),
                  pl.BlockSpec(blk, lambda b, kv: (b, CD_KC + kv)),
                  pl.BlockSpec(blk, lambda b, kv: (b, CD_VC + kv)),
                  pl.BlockSpec((3, QBLK, C_WIN), lambda b, kv: (0, 0, 0))],
        out_specs=pl.BlockSpec((SEQ, gw), lambda b, kv: (b, kv)),
        out_shape=jax.ShapeDtypeStruct((M_ROWS, C_Q_HEADS * HEAD_DIM), BF16),
        compiler_params=_params(),
        name="window_gqa_sink",
    )(sink.astype(F32), h, h, h, _band_mask_table(C_WINDOW))


D_DILS = tuple(d for _, d in D_DILATIONS)
D_SPAN = D_DILATIONS[0][0] // 2
assert all((w // 2) // d == D_SPAN and N_QBLK % d == 0 for w, d in D_DILATIONS) and D_DILS[0] == 1
D_WIN = QBLK + 2 * D_SPAN


def _dilated_body(q0_ref, q1_ref, q2_ref, k_ref, v_ref, mask_ref, o_ref,
                  stage_ref, qc_ref, kc_ref, vc_ref, og_ref, lse_ref):
    def to_class_major(dst_ref, slot, d):
        run = SEQ // d
        for rho in range(d):
            dst_ref[slot, pl.ds(rho * run, run), :] = stage_ref[pl.ds(rho, run, stride=d), :].astype(BF16)

    for src_ref, dst_ref in ((k_ref, kc_ref), (v_ref, vc_ref)):
        stage_ref[...] = src_ref[...].astype(F32)
        for g in range(1, D_GROUPS):
            to_class_major(dst_ref, g - 1, D_DILS[g])
    for g, src_ref in ((1, q1_ref), (2, q2_ref)):
        stage_ref[...] = src_ref[...].astype(F32)
        to_class_major(qc_ref, g - 1, D_DILS[g])

    def block(g, i):
        d = D_DILS[g]
        run_blocks = N_QBLK // d
        rho = i // run_blocks
        ub = i % run_blocks
        qoff = pl.multiple_of(i * QBLK, QBLK)
        start, variant = _band_window(ub, run_blocks, D_SPAN)
        koff = pl.multiple_of(rho * (run_blocks * QBLK) + start, D_SPAN)
        if g == 0:
            q, k, v = q0_ref[pl.ds(qoff, QBLK), :], k_ref[pl.ds(koff, D_WIN), :], v_ref[pl.ds(koff, D_WIN), :]
        else:
            q = qc_ref[g - 1, pl.ds(qoff, QBLK), :]
            k = kc_ref[g - 1, pl.ds(koff, D_WIN), :]
            v = vc_ref[g - 1, pl.ds(koff, D_WIN), :]
        s = lax.dot_general(q, k, NT_DIMS, preferred_element_type=F32) + mask_ref[variant]
        m = jnp.max(s, axis=-1, keepdims=True)
        p = jnp.exp(s - m)
        l = jnp.sum(p, axis=-1, keepdims=True)
        o = jnp.dot(p.astype(BF16), v, preferred_element_type=F32) / l
        lse = jnp.broadcast_to(m + jnp.log(l), (QBLK, HEAD_DIM))
        if d == 1:
            rows = pl.ds(qoff, QBLK)
        else:
            rows = pl.ds(rho + d * ub * QBLK, QBLK, stride=d)
        og_ref[g, rows, :] = o
        lse_ref[g, rows, :] = lse

    def step(i, carry):
        for g in range(D_GROUPS):
            block(g, i)
        return carry

    lax.fori_loop(0, N_QBLK, step, 0)

    def merge(c, carry):
        rows = pl.ds(pl.multiple_of(c * QBLK, QBLK), QBLK)
        lses = [lse_ref[g, rows, :] for g in range(D_GROUPS)]
        top = functools.reduce(jnp.maximum, lses)
        ws = [jnp.exp(x - top) for x in lses]
        num = sum(w * og_ref[g, rows, :] for g, w in enumerate(ws))
        o_ref[rows, :] = (num / sum(ws)).astype(o_ref.dtype)
        return carry

    lax.fori_loop(0, N_QBLK, merge, 0)


def dilated_attention(h):
    blk = (SEQ, HEAD_DIM)
    return pl.pallas_call(
        _dilated_body,
        grid=(BATCH, D_SLOTS),
        in_specs=[pl.BlockSpec(blk, lambda b, s: (b, CD_QD + s)),
                  pl.BlockSpec(blk, lambda b, s: (b, CD_QD + D_SLOTS + s)),
                  pl.BlockSpec(blk, lambda b, s: (b, CD_QD + 2 * D_SLOTS + s)),
                  pl.BlockSpec(blk, lambda b, s: (b, CD_KD + s)),
                  pl.BlockSpec(blk, lambda b, s: (b, CD_VD + s)),
                  pl.BlockSpec((3, QBLK, D_WIN), lambda b, s: (0, 0, 0))],
        out_specs=pl.BlockSpec(blk, lambda b, s: (b, s)),
        out_shape=jax.ShapeDtypeStruct((M_ROWS, D_SLOTS * HEAD_DIM), BF16),
        scratch_shapes=[pltpu.VMEM((SEQ, HEAD_DIM), F32),
                        pltpu.VMEM((D_GROUPS - 1, SEQ, HEAD_DIM), BF16),
                        pltpu.VMEM((D_GROUPS - 1, SEQ, HEAD_DIM), BF16),
                        pltpu.VMEM((D_GROUPS - 1, SEQ, HEAD_DIM), BF16),
                        pltpu.VMEM((D_GROUPS, SEQ, HEAD_DIM), F32),
                        pltpu.VMEM((D_GROUPS, SEQ, HEAD_DIM), F32)],
        compiler_params=_params(),
        name="dilated_attention",
    )(h, h, h, h, h, _band_mask_table(D_SPAN))


def _rope_tables(kind):
    pos = np.arange(SEQ, dtype=np.float64)
    cos = np.ones((SEQ, HEAD_DIM), np.float64)
    sin = np.zeros((SEQ, HEAD_DIM), np.float64)

    def fill(start, r, p, theta):
        half = r // 2
        inv = np.exp(-math.log(theta) * np.arange(half, dtype=np.float64) * (2.0 / r))
        ang = p[:, None] * inv[None, :]
        cos[:, start:start + half] = np.cos(ang)
        cos[:, start + half:start + r] = np.cos(ang)
        sin[:, start:start + half] = -np.sin(ang)
        sin[:, start + half:start + r] = np.sin(ang)

    if kind == "axial":
        hw = HEAD_DIM // 2
        fill(0, hw, np.floor(pos / GRID_W), AXIAL_THETA)
        fill(hw, hw, pos % GRID_W, AXIAL_THETA)
        moves = AXIAL_MOVES
    else:
        fill(0, ROPE_DIMS, pos, ROPE_THETA)
        moves = PARTIAL_MOVES
    perm = _moves_to_perm(moves)
    return moves, jnp.asarray(cos[:, perm], dtype=F32), jnp.asarray(sin[:, perm], dtype=F32)


def _gain_row(parts):
    rows = []
    for g, heads, scale, perm in parts:
        if g is None:
            g = jnp.ones((HEAD_DIM,), F32)
        g = g.astype(F32) * scale
        if perm is not None:
            g = g[perm]
        rows.append(jnp.tile(g, heads))
    return jnp.concatenate(rows).reshape(1, -1)


def _layer_ab(x, xn, w_in, w_out, li, a_qn, a_kn, a_rpb, b_qn, b_kn, axial):
    perm = _moves_to_perm(axial[0])
    gains = _gain_row([(a_qn, A_HEADS, SCALE, None), (a_kn, A_HEADS, 1.0, None), (None, A_HEADS, 1.0, None),
                       (b_qn, B_Q_HEADS, SCALE, perm), (b_kn, B_KV_HEADS, 1.0, perm),
                       (None, B_KV_HEADS, 1.0, None)])
    h = in_proj(xn, w_in, li, AB_KINDS, gains, axial)
    oa = neighbourhood_attention(h, _na_bias_table(a_rpb))
    ob = dense_gqa(h)
    return matmul_residual([oa, ob], w_out, li, x)


def _layer_cd(x, xn, w_in, w_out, li, c_qn, c_kn, c_sink, d_qn, d_kn, partial):
    perm = _moves_to_perm(partial[0])
    gains = _gain_row([(c_qn, C_Q_HEADS, SCALE, perm), (c_kn, C_KV_HEADS, 1.0, perm), (None, C_KV_HEADS, 1.0, None),
                       (d_qn, D_Q_HEADS, SCALE, perm), (d_kn, D_SLOTS, 1.0, perm), (None, D_SLOTS, 1.0, None)])
    h = in_proj(xn, w_in, li, CD_KINDS, gains, partial)
    oc = window_gqa_sink(h, c_sink)
    od = dilated_attention(h)
    return matmul_residual([oc, od], w_out, li, x)


def kernel(x, attn_norm, ffn_norm, ab_w_in, ab_w_out, a_q_norm, a_k_norm, a_rpb, b_q_norm, b_k_norm,
           cd_w_in, cd_w_out, c_q_norm, c_k_norm, c_sink, d_q_norm, d_k_norm, w_gate, w_up, w_down):
    axial = _rope_tables("axial")
    partial = _rope_tables("partial")
    x = x.reshape(M_ROWS, D_MODEL)
    for layer in range(DEPTH):
        j = layer // 2
        xn = rmsnorm(x, attn_norm[layer])
        if layer % 2 == 0:
            x = _layer_ab(x, xn, ab_w_in, ab_w_out, j, a_q_norm[j], a_k_norm[j], a_rpb[j],
                          b_q_norm[j], b_k_norm[j], axial)
        else:
            x = _layer_cd(x, xn, cd_w_in, cd_w_out, j, c_q_norm[j], c_k_norm[j], c_sink[j],
                          d_q_norm[j], d_k_norm[j], partial)
        xn = rmsnorm(x, ffn_norm[layer])
        hidden = gate_up(xn, w_gate, w_up, layer)
        x = matmul_residual([hidden], w_down, layer, x)
    return x.reshape(BATCH, SEQ, D_MODEL)
```

```python
import functools
import math

import numpy as np
import jax
import jax.numpy as jnp
from jax import lax
from jax.experimental import pallas as pl
from jax.experimental.pallas import tpu as pltpu

D_MODEL = 2048
BATCH = 2
SEQ = 4096
DEPTH = 4
HEAD_DIM = 128
GRID_W = 64
GRID_ROWS = SEQ // GRID_W
EPS = 1e-6
A_HEADS = 8
NA_ROWS = 8
NA_COLS = 16
B_Q_HEADS = 8
B_KV_HEADS = 2
AXIAL_THETA = 10000.0
C_Q_HEADS = 12
C_KV_HEADS = 4
C_WINDOW = 128
D_DILATIONS = ((128, 1), (512, 4), (2048, 16))
D_GROUPS = len(D_DILATIONS)
D_SLOTS = 4
D_Q_HEADS = D_GROUPS * D_SLOTS
ROPE_THETA = 500000.0
ROPE_DIMS = HEAD_DIM // 4
FFN_HIDDEN = ((-(-8 * D_MODEL // 3)) + 255) // 256 * 256
M_ROWS = BATCH * SEQ
SCALE = HEAD_DIM ** -0.5
NEG = -1e30

VMEM_LIMIT_BYTES = 52 * 1024 * 1024
MXU_N = 256

F32 = jnp.float32
BF16 = jnp.bfloat16
NT_DIMS = (((1,), (1,)), ((), ()))


def _params(**kw):
    return pltpu.CompilerParams(vmem_limit_bytes=VMEM_LIMIT_BYTES, **kw)


def _rmsnorm_body(x_ref, g_ref, o_ref):
    x = x_ref[...]
    ms = jnp.mean(x * x, axis=-1, keepdims=True)
    o_ref[...] = (x * lax.rsqrt(ms + EPS) * g_ref[...]).astype(o_ref.dtype)


def rmsnorm(x, g, tm=512):
    m, d = x.shape
    return pl.pallas_call(
        _rmsnorm_body,
        grid=(m // tm,),
        in_specs=[pl.BlockSpec((tm, d), lambda i: (i, 0)), pl.BlockSpec((1, d), lambda i: (0, 0))],
        out_specs=pl.BlockSpec((tm, d), lambda i: (i, 0)),
        out_shape=jax.ShapeDtypeStruct((m, d), BF16),
        compiler_params=_params(),
        name="rmsnorm",
    )(x, g.reshape(1, d))


HALF_LANES = HEAD_DIM // 2
AXIAL_MOVES = ((96, 32, 64), (32, 64, 96))
PARTIAL_MOVES = ((112, 16, 64), (48, 64, 80))


def _moves_to_perm(moves):
    perm = np.arange(HEAD_DIM)
    for shift, lo, hi in moves:
        perm[lo:hi] = (np.arange(lo, hi) - shift) % HEAD_DIM
    assert sorted(perm.tolist()) == list(range(HEAD_DIM))
    return perm


def _permute_lanes(w, moves):
    lane = lax.broadcasted_iota(jnp.int32, w.shape, 1)
    out = w
    for shift, lo, hi in moves:
        out = jnp.where((lane >= lo) & (lane < hi), pltpu.roll(w, shift, 1), out)
    return out


PROJ_TM = 2048
PROJ_TN = 512
PROJ_SUB = 256


def _in_proj_body(sig_ref, x_ref, w_ref, g_ref, cos_ref, sin_ref, o_ref, wbf_ref, *, sigs, moves):
    n = pl.program_id(0)
    m = pl.program_id(1)
    heads = PROJ_TN // HEAD_DIM

    def cast_weights(kinds):
        for h in range(heads):
            cols = slice(h * HEAD_DIM, (h + 1) * HEAD_DIM)
            w = w_ref[:, cols]
            if kinds[h] == "rope":
                w = _permute_lanes(w, moves)
            wbf_ref[:, cols] = w.astype(BF16)

    def compute(kinds):
        for r in range(PROJ_TM // PROJ_SUB):
            rows = slice(r * PROJ_SUB, (r + 1) * PROJ_SUB)
            x = x_ref[rows, :]
            for p in range(PROJ_TN // MXU_N):
                acc = jnp.dot(x, wbf_ref[:, p * MXU_N:(p + 1) * MXU_N], preferred_element_type=F32)
                for hh in range(MXU_N // HEAD_DIM):
                    h = p * (MXU_N // HEAD_DIM) + hh
                    cols = slice(h * HEAD_DIM, (h + 1) * HEAD_DIM)
                    y = acc[:, hh * HEAD_DIM:(hh + 1) * HEAD_DIM]
                    if kinds[h] != "plain":
                        ms = jnp.mean(y * y, axis=-1, keepdims=True)
                        y = y * lax.rsqrt(ms + EPS) * g_ref[:, cols]
                    if kinds[h] == "rope":
                        y = y * cos_ref[rows, :] + pltpu.roll(y, HALF_LANES, 1) * sin_ref[rows, :]
                    o_ref[rows, cols] = y.astype(o_ref.dtype)

    for sid, kinds in enumerate(sigs):
        @pl.when(sig_ref[n] == sid)
        def _(kinds=kinds):
            @pl.when(m == 0)
            def _():
                cast_weights(kinds)

            compute(kinds)


def in_proj(xn, w, li, kinds, gains, rope):
    m, k = xn.shape
    n = w.shape[2]
    heads = PROJ_TN // HEAD_DIM
    tiles = [tuple(kinds[i:i + heads]) for i in range(0, len(kinds), heads)]
    sigs = tuple(dict.fromkeys(tiles))
    sig_ids = jnp.asarray([sigs.index(t) for t in tiles], jnp.int32)
    moves, cos, sin = rope
    pos_blocks = SEQ // PROJ_TM
    return pl.pallas_call(
        functools.partial(_in_proj_body, sigs=sigs, moves=moves),
        grid=(n // PROJ_TN, m // PROJ_TM),
        in_specs=[pl.BlockSpec(memory_space=pltpu.SMEM),
                  pl.BlockSpec((PROJ_TM, k), lambda j, i: (i, 0)),
                  pl.BlockSpec((None, k, PROJ_TN), lambda j, i: (li, 0, j)),
                  pl.BlockSpec((1, PROJ_TN), lambda j, i: (0, j)),
                  pl.BlockSpec((PROJ_TM, HEAD_DIM), lambda j, i: (i % pos_blocks, 0)),
                  pl.BlockSpec((PROJ_TM, HEAD_DIM), lambda j, i: (i % pos_blocks, 0))],
        out_specs=pl.BlockSpec((PROJ_TM, PROJ_TN), lambda j, i: (i, j)),
        out_shape=jax.ShapeDtypeStruct((m, n), BF16),
        scratch_shapes=[pltpu.VMEM((k, PROJ_TN), BF16)],
        compiler_params=_params(),
        name="in_proj",
    )(sig_ids, xn, w, gains, cos, sin)


RES_TM = 512
RES_SUB = 256
RES_VMEM_WEIGHT_BYTES = 12 * 1024 * 1024


def _res_tn(kt, n):
    tn = n
    while kt * tn * 4 > RES_VMEM_WEIGHT_BYTES and tn % (2 * MXU_N) == 0:
        tn //= 2
    return tn


def _mm_res_body(*refs, widths, tn):
    n_in = len(widths)
    a_refs = refs[:n_in]
    w_ref, x_ref, o_ref, wbf_ref = refs[n_in:]

    @pl.when(pl.program_id(1) == 0)
    def _():
        wbf_ref[...] = w_ref[...].astype(BF16)

    for r in range(RES_TM // RES_SUB):
        rows = slice(r * RES_SUB, (r + 1) * RES_SUB)
        for p in range(tn // MXU_N):
            cols = slice(p * MXU_N, (p + 1) * MXU_N)
            acc = x_ref[rows, cols]
            k0 = 0
            for a_ref, ka in zip(a_refs, widths):
                acc = acc + jnp.dot(a_ref[rows, :], wbf_ref[k0:k0 + ka, cols], preferred_element_type=F32)
                k0 += ka
            o_ref[rows, cols] = acc


def matmul_residual(a_list, w, li, x):
    m, n = x.shape
    widths = tuple(a.shape[1] for a in a_list)
    kt = sum(widths)
    assert kt == w.shape[1]
    tn = _res_tn(kt, n)
    in_specs = [pl.BlockSpec((RES_TM, ka), lambda j, i: (i, 0)) for ka in widths]
    in_specs += [pl.BlockSpec((None, kt, tn), lambda j, i: (li, 0, j)),
                 pl.BlockSpec((RES_TM, tn), lambda j, i: (i, j))]
    return pl.pallas_call(
        functools.partial(_mm_res_body, widths=widths, tn=tn),
        grid=(n // tn, m // RES_TM),
        in_specs=in_specs,
        out_specs=pl.BlockSpec((RES_TM, tn), lambda j, i: (i, j)),
        out_shape=jax.ShapeDtypeStruct((m, n), F32),
        scratch_shapes=[pltpu.VMEM((kt, tn), BF16)],
        compiler_params=_params(),
        name="matmul_residual",
    )(*a_list, w, x)


FFN_TM = 1024
FFN_TN = 512
FFN_SUB = 256


def _gate_up_body(x_ref, wg_ref, wu_ref, o_ref, wgb_ref, wub_ref):
    @pl.when(pl.program_id(1) == 0)
    def _():
        wgb_ref[...] = wg_ref[...].astype(BF16)
        wub_ref[...] = wu_ref[...].astype(BF16)

    for r in range(FFN_TM // FFN_SUB):
        rows = slice(r * FFN_SUB, (r + 1) * FFN_SUB)
        x = x_ref[rows, :]
        for p in range(FFN_TN // MXU_N):
            cols = slice(p * MXU_N, (p + 1) * MXU_N)
            g = jnp.dot(x, wgb_ref[:, cols], preferred_element_type=F32)
            u = jnp.dot(x, wub_ref[:, cols], preferred_element_type=F32)
            o_ref[rows, cols] = (g * (1.0 / (1.0 + jnp.exp(-g))) * u).astype(o_ref.dtype)


def gate_up(xn, wg, wu, li):
    m, k = xn.shape
    n = wg.shape[2]
    return pl.pallas_call(
        _gate_up_body,
        grid=(n // FFN_TN, m // FFN_TM),
        in_specs=[pl.BlockSpec((FFN_TM, k), lambda j, i: (i, 0)),
                  pl.BlockSpec((None, k, FFN_TN), lambda j, i: (li, 0, j)),
                  pl.BlockSpec((None, k, FFN_TN), lambda j, i: (li, 0, j))],
        out_specs=pl.BlockSpec((FFN_TM, FFN_TN), lambda j, i: (i, j)),
        out_shape=jax.ShapeDtypeStruct((m, n), BF16),
        scratch_shapes=[pltpu.VMEM((k, FFN_TN), BF16), pltpu.VMEM((k, FFN_TN), BF16)],
        compiler_params=_params(),
        name="gate_up",
    )(xn, wg, wu)


AB_QA, AB_KA, AB_VA = 0, A_HEADS, 2 * A_HEADS
AB_QB = 3 * A_HEADS
AB_KB = AB_QB + B_Q_HEADS
AB_VB = AB_KB + B_KV_HEADS
AB_KINDS = ("norm",) * (2 * A_HEADS) + ("plain",) * A_HEADS + ("rope",) * (B_Q_HEADS + B_KV_HEADS) \
    + ("plain",) * B_KV_HEADS
CD_QC = 0
CD_KC = C_Q_HEADS
CD_VC = CD_KC + C_KV_HEADS
CD_QD = CD_VC + C_KV_HEADS
CD_KD = CD_QD + D_Q_HEADS
CD_VD = CD_KD + D_SLOTS
CD_KINDS = ("rope",) * (C_Q_HEADS + C_KV_HEADS) + ("plain",) * C_KV_HEADS \
    + ("rope",) * (D_Q_HEADS + D_SLOTS) + ("plain",) * D_SLOTS


NA_QROWS = 2
NA_WIN_ROWS = 10
NA_Q = NA_QROWS * GRID_W
NA_KEYS = NA_WIN_ROWS * GRID_W
NA_BLOCKS = GRID_ROWS // NA_QROWS
NA_VARIANT_BLOCKS = (0, 1, 2, NA_BLOCKS - 2, NA_BLOCKS - 1)


def _na_win_start(rb):
    return np.clip(NA_QROWS * rb - NA_ROWS // 2, 0, GRID_ROWS - NA_WIN_ROWS)


N_DR = 2 * NA_ROWS - 1


def _na_row_slabs():
    dr_idx = np.full((len(NA_VARIANT_BLOCKS), NA_QROWS, NA_WIN_ROWS), N_DR, np.int32)
    for v, rb in enumerate(NA_VARIANT_BLOCKS):
        ws = _na_win_start(rb)
        for qr in range(NA_QROWS):
            qrow = NA_QROWS * rb + qr
            r0 = np.clip(qrow - NA_ROWS // 2, 0, GRID_ROWS - NA_ROWS)
            for kr in range(NA_WIN_ROWS):
                krow = ws + kr
                if r0 <= krow < r0 + NA_ROWS:
                    dr_idx[v, qr, kr] = krow - qrow + NA_ROWS - 1
    return dr_idx


def _na_column_table(rpb):
    assert rpb.shape == (A_HEADS, N_DR, 2 * NA_COLS - 1)
    lanes = 2 * GRID_W
    ext = jnp.concatenate([rpb.astype(F32), jnp.full((A_HEADS, N_DR, lanes - rpb.shape[2]), NEG, F32)], axis=-1)
    ext = jnp.roll(ext, -(NA_COLS - 1), axis=-1)
    toep = jnp.tile(ext, (1, 1, GRID_W))[:, :, :GRID_W * (lanes - 1)].reshape(A_HEADS, N_DR, GRID_W, lanes - 1)
    toep = toep[..., :GRID_W]
    cols = np.arange(GRID_W)
    c0 = np.clip(cols - NA_COLS // 2, 0, GRID_W - NA_COLS)
    col_ok = (cols[None, :] >= c0[:, None]) & (cols[None, :] < c0[:, None] + NA_COLS)
    toep = jnp.where(col_ok[None, None], toep, NEG)
    toep = jnp.concatenate([toep, jnp.full((A_HEADS, 1, GRID_W, GRID_W), NEG, F32)], axis=1)
    return jnp.concatenate([toep, toep], axis=-1)


NA_UNROLL = 4


def _na_body(q_ref, k_ref, v_ref, c_ref, o_ref, t_ref):
    left_half = lax.broadcasted_iota(jnp.int32, (GRID_W, 2 * GRID_W), 1) < GRID_W
    for v, per_q in enumerate(_na_row_slabs()):
        for qr, slabs in enumerate(per_q):
            for pair in range(NA_WIN_ROWS // 2):
                tile = jnp.where(left_half, c_ref[int(slabs[2 * pair])], c_ref[int(slabs[2 * pair + 1])])
                t_ref[v, qr * GRID_W:(qr + 1) * GRID_W, pair * 2 * GRID_W:(pair + 1) * 2 * GRID_W] = tile

    def block(rb):
        ws = jnp.clip(NA_QROWS * rb - NA_ROWS // 2, 0, GRID_ROWS - NA_WIN_ROWS)
        koff = pl.multiple_of(ws * GRID_W, GRID_W)
        qoff = pl.multiple_of(rb * NA_Q, NA_Q)
        variant = jnp.where(rb < 2, rb, jnp.where(rb >= NA_BLOCKS - 2, rb - (NA_BLOCKS - 5), 2))
        q = q_ref[pl.ds(qoff, NA_Q), :]
        k = k_ref[pl.ds(koff, NA_KEYS), :]
        v = v_ref[pl.ds(koff, NA_KEYS), :]
        s = lax.dot_general(q, k, NT_DIMS, preferred_element_type=F32) + t_ref[variant]
        m = jnp.max(s, axis=-1, keepdims=True)
        p = jnp.exp(s - m)
        l = jnp.sum(p, axis=-1, keepdims=True)
        o = jnp.dot(p.astype(BF16), v, preferred_element_type=F32)
        o_ref[pl.ds(qoff, NA_Q), :] = (o / l).astype(o_ref.dtype)

    def step(it, carry):
        for u in range(NA_UNROLL):
            block(it * NA_UNROLL + u)
        return carry

    lax.fori_loop(0, NA_BLOCKS // NA_UNROLL, step, 0)


def neighbourhood_attention(h, rpb):
    nv = len(NA_VARIANT_BLOCKS)
    blk = (SEQ, HEAD_DIM)
    return pl.pallas_call(
        _na_body,
        grid=(BATCH, A_HEADS),
        in_specs=[pl.BlockSpec(blk, lambda b, hd: (b, AB_QA + hd)),
                  pl.BlockSpec(blk, lambda b, hd: (b, AB_KA + hd)),
                  pl.BlockSpec(blk, lambda b, hd: (b, AB_VA + hd)),
                  pl.BlockSpec((None, N_DR + 1, GRID_W, 2 * GRID_W), lambda b, hd: (hd, 0, 0, 0))],
        out_specs=pl.BlockSpec(blk, lambda b, hd: (b, hd)),
        out_shape=jax.ShapeDtypeStruct((M_ROWS, A_HEADS * HEAD_DIM), BF16),
        scratch_shapes=[pltpu.VMEM((nv, NA_Q, NA_KEYS), F32)],
        compiler_params=_params(),
        name="neighbourhood_attention",
    )(h, h, h, _na_column_table(rpb))


B_GROUP = B_Q_HEADS // B_KV_HEADS


DENSE_TQ = 256
DENSE_TK = 512


def _dense_body(q_ref, k_ref, v_ref, o_ref):
    tq = DENSE_TQ
    q = jnp.concatenate([q_ref[:, g * HEAD_DIM:(g + 1) * HEAD_DIM] for g in range(B_GROUP)], axis=0)
    m = l = acc = None
    for c in range(SEQ // DENSE_TK):
        keys = slice(c * DENSE_TK, (c + 1) * DENSE_TK)
        s = lax.dot_general(q, k_ref[keys, :], NT_DIMS, preferred_element_type=F32)
        row_max = jnp.max(s, axis=-1, keepdims=True)
        if c == 0:
            m = row_max
            p = jnp.exp(s - m)
            l = jnp.sum(p, axis=-1, keepdims=True)
            acc = jnp.dot(p.astype(BF16), v_ref[keys, :], preferred_element_type=F32)
        else:
            m_new = jnp.maximum(m, row_max)
            alpha = jnp.exp(m - m_new)
            p = jnp.exp(s - m_new)
            l = alpha * l + jnp.sum(p, axis=-1, keepdims=True)
            acc = alpha * acc + jnp.dot(p.astype(BF16), v_ref[keys, :], preferred_element_type=F32)
            m = m_new
    o = acc / l
    for g in range(B_GROUP):
        o_ref[:, g * HEAD_DIM:(g + 1) * HEAD_DIM] = o[g * tq:(g + 1) * tq].astype(o_ref.dtype)


def dense_gqa(h):
    tq = DENSE_TQ
    nq = SEQ // tq
    gw = B_GROUP * HEAD_DIM
    assert AB_QB % B_GROUP == 0
    q_blk0 = AB_QB // B_GROUP
    blk = (SEQ, HEAD_DIM)
    return pl.pallas_call(
        _dense_body,
        grid=(BATCH, B_KV_HEADS, nq),
        in_specs=[pl.BlockSpec((tq, gw), lambda b, kv, i: (b * nq + i, q_blk0 + kv)),
                  pl.BlockSpec(blk, lambda b, kv, i: (b, AB_KB + kv)),
                  pl.BlockSpec(blk, lambda b, kv, i: (b, AB_VB + kv))],
        out_specs=pl.BlockSpec((tq, gw), lambda b, kv, i: (b * nq + i, kv)),
        out_shape=jax.ShapeDtypeStruct((M_ROWS, B_Q_HEADS * HEAD_DIM), BF16),
        compiler_params=_params(),
        name="dense_gqa",
    )(h, h, h)


QBLK = 128
N_QBLK = SEQ // QBLK


def _band_mask_table(span):
    assert span <= QBLK
    col = np.arange(QBLK + 2 * span)[None, :]
    row = np.arange(QBLK)[:, None]
    t = np.stack([np.where(np.abs(col - row + shift) <= span, 0.0, NEG) for shift in (0, -span, -2 * span)])
    return jnp.asarray(t, F32)


def _band_window(ub, blocks, span):
    start = jnp.clip(ub * QBLK - span, 0, blocks * QBLK - (QBLK + 2 * span))
    variant = jnp.where(ub == 0, 0, jnp.where(ub == blocks - 1, 2, 1))
    return start, variant


C_GROUP = C_Q_HEADS // C_KV_HEADS


C_WIN = QBLK + 2 * C_WINDOW
C_UNROLL = 2


def _window_sink_body(sink_ref, q_ref, k_ref, v_ref, mask_ref, o_ref):
    kv = pl.program_id(1)

    def block(i):
        qoff = pl.multiple_of(i * QBLK, QBLK)
        start, variant = _band_window(i, N_QBLK, C_WINDOW)
        koff = pl.multiple_of(start, QBLK)
        mask = mask_ref[variant]
        q = q_ref[pl.ds(qoff, QBLK), :]
        qs = jnp.concatenate([q[:, g * HEAD_DIM:(g + 1) * HEAD_DIM] for g in range(C_GROUP)], axis=0)
        s = lax.dot_general(qs, k_ref[pl.ds(koff, C_WIN), :], NT_DIMS, preferred_element_type=F32)
        ps, ls = [], []
        for g in range(C_GROUP):
            sg = s[g * QBLK:(g + 1) * QBLK] + mask
            sink = sink_ref[kv * C_GROUP + g]
            m = jnp.maximum(jnp.max(sg, axis=-1, keepdims=True), sink)
            p = jnp.exp(sg - m)
            ls.append(jnp.sum(p, axis=-1, keepdims=True) + jnp.exp(sink - m))
            ps.append(p.astype(BF16))
        o = jnp.dot(jnp.concatenate(ps, axis=0), v_ref[pl.ds(koff, C_WIN), :], preferred_element_type=F32)
        for g in range(C_GROUP):
            o_ref[pl.ds(qoff, QBLK), g * HEAD_DIM:(g + 1) * HEAD_DIM] = (
                o[g * QBLK:(g + 1) * QBLK] / ls[g]).astype(o_ref.dtype)

    def step(it, carry):
        for u in range(C_UNROLL):
            block(it * C_UNROLL + u)
        return carry

    lax.fori_loop(0, N_QBLK // C_UNROLL, step, 0)


def window_gqa_sink(h, sink):
    gw = C_GROUP * HEAD_DIM
    assert CD_QC % C_GROUP == 0
    q_blk0 = CD_QC // C_GROUP
    blk = (SEQ, HEAD_DIM)
    return pl.pallas_call(
        _window_sink_body,
        grid=(BATCH, C_KV_HEADS),
        in_specs=[pl.BlockSpec(memory_space=pltpu.SMEM),
                  pl.BlockSpec((SEQ, gw), lambda b, kv: (b, q_blk0 + kv)),
                  pl.BlockSpec(blk, lambda b, kv: (b, CD_KC + kv)),
                  pl.BlockSpec(blk, lambda b, kv: (b, CD_VC + kv)),
                  pl.BlockSpec((3, QBLK, C_WIN), lambda b, kv: (0, 0, 0))],
        out_specs=pl.BlockSpec((SEQ, gw), lambda b, kv: (b, kv)),
        out_shape=jax.ShapeDtypeStruct((M_ROWS, C_Q_HEADS * HEAD_DIM), BF16),
        compiler_params=_params(),
        name="window_gqa_sink",
    )(sink.astype(F32), h, h, h, _band_mask_table(C_WINDOW))


D_DILS = tuple(d for _, d in D_DILATIONS)
D_SPAN = D_DILATIONS[0][0] // 2
assert all((w // 2) // d == D_SPAN and N_QBLK % d == 0 for w, d in D_DILATIONS) and D_DILS[0] == 1
D_WIN = QBLK + 2 * D_SPAN
D_UNROLL = 2


def _dilated_body(q0_ref, q1_ref, q2_ref, k_ref, v_ref, mask_ref, o_ref,
                  stage_ref, qc_ref, kc_ref, vc_ref, og_ref, lse_ref):
    def to_class_major(dst_ref, slot, d):
        run = SEQ // d
        for rho in range(d):
            dst_ref[slot, pl.ds(rho * run, run), :] = stage_ref[pl.ds(rho, run, stride=d), :].astype(BF16)

    for src_ref, dst_ref in ((k_ref, kc_ref), (v_ref, vc_ref)):
        stage_ref[...] = src_ref[...].astype(F32)
        for g in range(1, D_GROUPS):
            to_class_major(dst_ref, g - 1, D_DILS[g])
    for g, src_ref in ((1, q1_ref), (2, q2_ref)):
        stage_ref[...] = src_ref[...].astype(F32)
        to_class_major(qc_ref, g - 1, D_DILS[g])

    def block(g, i):
        d = D_DILS[g]
        run_blocks = N_QBLK // d
        rho = i // run_blocks
        ub = i % run_blocks
        qoff = pl.multiple_of(i * QBLK, QBLK)
        start, variant = _band_window(ub, run_blocks, D_SPAN)
        koff = pl.multiple_of(rho * (run_blocks * QBLK) + start, D_SPAN)
        if g == 0:
            q, k, v = q0_ref[pl.ds(qoff, QBLK), :], k_ref[pl.ds(koff, D_WIN), :], v_ref[pl.ds(koff, D_WIN), :]
        else:
            q = qc_ref[g - 1, pl.ds(qoff, QBLK), :]
            k = kc_ref[g - 1, pl.ds(koff, D_WIN), :]
            v = vc_ref[g - 1, pl.ds(koff, D_WIN), :]
        s = lax.dot_general(q, k, NT_DIMS, preferred_element_type=F32) + mask_ref[variant]
        m = jnp.max(s, axis=-1, keepdims=True)
        p = jnp.exp(s - m)
        l = jnp.sum(p, axis=-1, keepdims=True)
        o = jnp.dot(p.astype(BF16), v, preferred_element_type=F32) / l
        lse = jnp.broadcast_to(m + jnp.log(l), (QBLK, HEAD_DIM))
        if d == 1:
            rows = pl.ds(qoff, QBLK)
        else:
            rows = pl.ds(rho + d * ub * QBLK, QBLK, stride=d)
        og_ref[g, rows, :] = o
        lse_ref[g, rows, :] = lse

    def step(it, carry):
        for u in range(D_UNROLL):
            for g in range(D_GROUPS):
                block(g, it * D_UNROLL + u)
        return carry

    lax.fori_loop(0, N_QBLK // D_UNROLL, step, 0)

    def merge(c, carry):
        rows = pl.ds(pl.multiple_of(c * QBLK, QBLK), QBLK)
        lses = [lse_ref[g, rows, :] for g in range(D_GROUPS)]
        top = functools.reduce(jnp.maximum, lses)
        ws = [jnp.exp(x - top) for x in lses]
        num = sum(w * og_ref[g, rows, :] for g, w in enumerate(ws))
        o_ref[rows, :] = (num / sum(ws)).astype(o_ref.dtype)
        return carry

    lax.fori_loop(0, N_QBLK, merge, 0)


def dilated_attention(h):
    blk = (SEQ, HEAD_DIM)
    return pl.pallas_call(
        _dilated_body,
        grid=(BATCH, D_SLOTS),
        in_specs=[pl.BlockSpec(blk, lambda b, s: (b, CD_QD + s)),
                  pl.BlockSpec(blk, lambda b, s: (b, CD_QD + D_SLOTS + s)),
                  pl.BlockSpec(blk, lambda b, s: (b, CD_QD + 2 * D_SLOTS + s)),
                  pl.BlockSpec(blk, lambda b, s: (b, CD_KD + s)),
                  pl.BlockSpec(blk, lambda b, s: (b, CD_VD + s)),
                  pl.BlockSpec((3, QBLK, D_WIN), lambda b, s: (0, 0, 0))],
        out_specs=pl.BlockSpec(blk, lambda b, s: (b, s)),
        out_shape=jax.ShapeDtypeStruct((M_ROWS, D_SLOTS * HEAD_DIM), BF16),
        scratch_shapes=[pltpu.VMEM((SEQ, HEAD_DIM), F32),
                        pltpu.VMEM((D_GROUPS - 1, SEQ, HEAD_DIM), BF16),
                        pltpu.VMEM((D_GROUPS - 1, SEQ, HEAD_DIM), BF16),
                        pltpu.VMEM((D_GROUPS - 1, SEQ, HEAD_DIM), BF16),
                        pltpu.VMEM((D_GROUPS, SEQ, HEAD_DIM), F32),
                        pltpu.VMEM((D_GROUPS, SEQ, HEAD_DIM), F32)],
        compiler_params=_params(),
        name="dilated_attention",
    )(h, h, h, h, h, _band_mask_table(D_SPAN))


def _rope_tables(kind):
    pos = np.arange(SEQ, dtype=np.float64)
    cos = np.ones((SEQ, HEAD_DIM), np.float64)
    sin = np.zeros((SEQ, HEAD_DIM), np.float64)

    def fill(start, r, p, theta):
        half = r // 2
        inv = np.exp(-math.log(theta) * np.arange(half, dtype=np.float64) * (2.0 / r))
        ang = p[:, None] * inv[None, :]
        cos[:, start:start + half] = np.cos(ang)
        cos[:, start + half:start + r] = np.cos(ang)
        sin[:, start:start + half] = -np.sin(ang)
        sin[:, start + half:start + r] = np.sin(ang)

    if kind == "axial":
        hw = HEAD_DIM // 2
        fill(0, hw, np.floor(pos / GRID_W), AXIAL_THETA)
        fill(hw, hw, pos % GRID_W, AXIAL_THETA)
        moves = AXIAL_MOVES
    else:
        fill(0, ROPE_DIMS, pos, ROPE_THETA)
        moves = PARTIAL_MOVES
    perm = _moves_to_perm(moves)
    return moves, jnp.asarray(cos[:, perm], dtype=F32), jnp.asarray(sin[:, perm], dtype=F32)


def _gain_row(parts):
    rows = []
    for g, heads, scale, perm in parts:
        if g is None:
            g = jnp.ones((HEAD_DIM,), F32)
        g = g.astype(F32) * scale
        if perm is not None:
            g = g[perm]
        rows.append(jnp.tile(g, heads))
    return jnp.concatenate(rows).reshape(1, -1)


def _layer_ab(x, xn, w_in, w_out, li, a_qn, a_kn, a_rpb, b_qn, b_kn, axial):
    perm = _moves_to_perm(axial[0])
    gains = _gain_row([(a_qn, A_HEADS, SCALE, None), (a_kn, A_HEADS, 1.0, None), (None, A_HEADS, 1.0, None),
                       (b_qn, B_Q_HEADS, SCALE, perm), (b_kn, B_KV_HEADS, 1.0, perm),
                       (None, B_KV_HEADS, 1.0, None)])
    h = in_proj(xn, w_in, li, AB_KINDS, gains, axial)
    oa = neighbourhood_attention(h, a_rpb)
    ob = dense_gqa(h)
    return matmul_residual([oa, ob], w_out, li, x)


def _layer_cd(x, xn, w_in, w_out, li, c_qn, c_kn, c_sink, d_qn, d_kn, partial):
    perm = _moves_to_perm(partial[0])
    gains = _gain_row([(c_qn, C_Q_HEADS, SCALE, perm), (c_kn, C_KV_HEADS, 1.0, perm), (None, C_KV_HEADS, 1.0, None),
                       (d_qn, D_Q_HEADS, SCALE, perm), (d_kn, D_SLOTS, 1.0, perm), (None, D_SLOTS, 1.0, None)])
    h = in_proj(xn, w_in, li, CD_KINDS, gains, partial)
    oc = window_gqa_sink(h, c_sink)
    od = dilated_attention(h)
    return matmul_residual([oc, od], w_out, li, x)


def kernel(x, attn_norm, ffn_norm, ab_w_in, ab_w_out, a_q_norm, a_k_norm, a_rpb, b_q_norm, b_k_norm,
           cd_w_in, cd_w_out, c_q_norm, c_k_norm, c_sink, d_q_norm, d_k_norm, w_gate, w_up, w_down):
    axial = _rope_tables("axial")
    partial = _rope_tables("partial")
    x = x.reshape(M_ROWS, D_MODEL)
    for layer in range(DEPTH):
        j = layer // 2
        xn = rmsnorm(x, attn_norm[layer])
        if layer % 2 == 0:
            x = _layer_ab(x, xn, ab_w_in, ab_w_out, j, a_q_norm[j], a_k_norm[j], a_rpb[j],
                          b_q_norm[j], b_k_norm[j], axial)
        else:
            x = _layer_cd(x, xn, cd_w_in, cd_w_out, j, c_q_norm[j], c_k_norm[j], c_sink[j],
                          d_q_norm[j], d_k_norm[j], partial)
        xn = rmsnorm(x, ffn_norm[layer])
        hidden = gate_up(xn, w_gate, w_up, layer)
        x = matmul_residual([hidden], w_down, layer, x)
    return x.reshape(BATCH, SEQ, D_MODEL)
```

```python
import functools
import math

import numpy as np
import jax
import jax.numpy as jnp
from jax import lax
from jax.experimental import pallas as pl
from jax.experimental.pallas import tpu as pltpu

D_MODEL = 2048
BATCH = 2
SEQ = 4096
DEPTH = 4
HEAD_DIM = 128
GRID_W = 64
GRID_ROWS = SEQ // GRID_W
EPS = 1e-6
A_HEADS = 8
NA_ROWS = 8
NA_COLS = 16
B_Q_HEADS = 8
B_KV_HEADS = 2
AXIAL_THETA = 10000.0
C_Q_HEADS = 12
C_KV_HEADS = 4
C_WINDOW = 128
D_DILATIONS = ((128, 1), (512, 4), (2048, 16))
D_GROUPS = len(D_DILATIONS)
D_SLOTS = 4
D_Q_HEADS = D_GROUPS * D_SLOTS
ROPE_THETA = 500000.0
ROPE_DIMS = HEAD_DIM // 4
FFN_HIDDEN = ((-(-8 * D_MODEL // 3)) + 255) // 256 * 256
M_ROWS = BATCH * SEQ
LOG2E = math.log2(math.e)
SCALE = HEAD_DIM ** -0.5 * LOG2E
NEG = -1e30

VMEM_LIMIT_BYTES = 52 * 1024 * 1024
MXU_N = 256

F32 = jnp.float32
BF16 = jnp.bfloat16
NT_DIMS = (((1,), (1,)), ((), ()))


def _params(**kw):
    return pltpu.CompilerParams(vmem_limit_bytes=VMEM_LIMIT_BYTES, **kw)


def _rmsnorm_body(x_ref, g_ref, o_ref):
    x = x_ref[...]
    ms = jnp.mean(x * x, axis=-1, keepdims=True)
    o_ref[...] = (x * lax.rsqrt(ms + EPS) * g_ref[...]).astype(o_ref.dtype)


def rmsnorm(x, g, tm=512):
    m, d = x.shape
    return pl.pallas_call(
        _rmsnorm_body,
        grid=(m // tm,),
        in_specs=[pl.BlockSpec((tm, d), lambda i: (i, 0)), pl.BlockSpec((1, d), lambda i: (0, 0))],
        out_specs=pl.BlockSpec((tm, d), lambda i: (i, 0)),
        out_shape=jax.ShapeDtypeStruct((m, d), BF16),
        compiler_params=_params(),
        name="rmsnorm",
    )(x, g.reshape(1, d))


HALF_LANES = HEAD_DIM // 2
AXIAL_MOVES = ((96, 32, 64), (32, 64, 96))
PARTIAL_MOVES = ((112, 16, 64), (48, 64, 80))


def _moves_to_perm(moves):
    perm = np.arange(HEAD_DIM)
    for shift, lo, hi in moves:
        perm[lo:hi] = (np.arange(lo, hi) - shift) % HEAD_DIM
    assert sorted(perm.tolist()) == list(range(HEAD_DIM))
    return perm


def _permute_lanes(w, moves):
    lane = lax.broadcasted_iota(jnp.int32, w.shape, 1)
    out = w
    for shift, lo, hi in moves:
        out = jnp.where((lane >= lo) & (lane < hi), pltpu.roll(w, shift, 1), out)
    return out


PROJ_TM = 2048
PROJ_TN = 512
PROJ_SUB = 256


def _in_proj_body(sig_ref, x_ref, w_ref, g_ref, cos_ref, sin_ref, o_ref, wbf_ref, *, sigs, moves):
    n = pl.program_id(0)
    m = pl.program_id(1)
    heads = PROJ_TN // HEAD_DIM

    def cast_weights(kinds):
        for h in range(heads):
            cols = slice(h * HEAD_DIM, (h + 1) * HEAD_DIM)
            w = w_ref[:, cols]
            if kinds[h] == "rope":
                w = _permute_lanes(w, moves)
            wbf_ref[:, cols] = w.astype(BF16)

    def compute(kinds):
        for r in range(PROJ_TM // PROJ_SUB):
            rows = slice(r * PROJ_SUB, (r + 1) * PROJ_SUB)
            x = x_ref[rows, :]
            for p in range(PROJ_TN // MXU_N):
                acc = jnp.dot(x, wbf_ref[:, p * MXU_N:(p + 1) * MXU_N], preferred_element_type=F32)
                for hh in range(MXU_N // HEAD_DIM):
                    h = p * (MXU_N // HEAD_DIM) + hh
                    cols = slice(h * HEAD_DIM, (h + 1) * HEAD_DIM)
                    y = acc[:, hh * HEAD_DIM:(hh + 1) * HEAD_DIM]
                    if kinds[h] != "plain":
                        ms = jnp.mean(y * y, axis=-1, keepdims=True)
                        y = y * lax.rsqrt(ms + EPS) * g_ref[:, cols]
                    if kinds[h] == "rope":
                        y = y * cos_ref[rows, :] + pltpu.roll(y, HALF_LANES, 1) * sin_ref[rows, :]
                    o_ref[rows, cols] = y.astype(o_ref.dtype)

    for sid, kinds in enumerate(sigs):
        @pl.when(sig_ref[n] == sid)
        def _(kinds=kinds):
            @pl.when(m == 0)
            def _():
                cast_weights(kinds)

            compute(kinds)


def in_proj(xn, w, li, kinds, gains, rope):
    m, k = xn.shape
    n = w.shape[2]
    heads = PROJ_TN // HEAD_DIM
    tiles = [tuple(kinds[i:i + heads]) for i in range(0, len(kinds), heads)]
    sigs = tuple(dict.fromkeys(tiles))
    sig_ids = jnp.asarray([sigs.index(t) for t in tiles], jnp.int32)
    moves, cos, sin = rope
    pos_blocks = SEQ // PROJ_TM
    return pl.pallas_call(
        functools.partial(_in_proj_body, sigs=sigs, moves=moves),
        grid=(n // PROJ_TN, m // PROJ_TM),
        in_specs=[pl.BlockSpec(memory_space=pltpu.SMEM),
                  pl.BlockSpec((PROJ_TM, k), lambda j, i: (i, 0)),
                  pl.BlockSpec((None, k, PROJ_TN), lambda j, i: (li, 0, j)),
                  pl.BlockSpec((1, PROJ_TN), lambda j, i: (0, j)),
                  pl.BlockSpec((PROJ_TM, HEAD_DIM), lambda j, i: (i % pos_blocks, 0)),
                  pl.BlockSpec((PROJ_TM, HEAD_DIM), lambda j, i: (i % pos_blocks, 0))],
        out_specs=pl.BlockSpec((PROJ_TM, PROJ_TN), lambda j, i: (i, j)),
        out_shape=jax.ShapeDtypeStruct((m, n), BF16),
        scratch_shapes=[pltpu.VMEM((k, PROJ_TN), BF16)],
        compiler_params=_params(),
        name="in_proj",
    )(sig_ids, xn, w, gains, cos, sin)


RES_TM = 512
RES_SUB = 256
RES_VMEM_WEIGHT_BYTES = 12 * 1024 * 1024


def _res_tn(kt, n):
    tn = n
    while kt * tn * 4 > RES_VMEM_WEIGHT_BYTES and tn % (2 * MXU_N) == 0:
        tn //= 2
    return tn


def _mm_res_body(*refs, widths, tn):
    n_in = len(widths)
    a_refs = refs[:n_in]
    w_ref, x_ref, o_ref, wbf_ref = refs[n_in:]

    @pl.when(pl.program_id(1) == 0)
    def _():
        wbf_ref[...] = w_ref[...].astype(BF16)

    for r in range(RES_TM // RES_SUB):
        rows = slice(r * RES_SUB, (r + 1) * RES_SUB)
        for p in range(tn // MXU_N):
            cols = slice(p * MXU_N, (p + 1) * MXU_N)
            acc = x_ref[rows, cols]
            k0 = 0
            for a_ref, ka in zip(a_refs, widths):
                acc = acc + jnp.dot(a_ref[rows, :], wbf_ref[k0:k0 + ka, cols], preferred_element_type=F32)
                k0 += ka
            o_ref[rows, cols] = acc


def matmul_residual(a_list, w, li, x):
    m, n = x.shape
    widths = tuple(a.shape[1] for a in a_list)
    kt = sum(widths)
    assert kt == w.shape[1]
    tn = _res_tn(kt, n)
    in_specs = [pl.BlockSpec((RES_TM, ka), lambda j, i: (i, 0)) for ka in widths]
    in_specs += [pl.BlockSpec((None, kt, tn), lambda j, i: (li, 0, j)),
                 pl.BlockSpec((RES_TM, tn), lambda j, i: (i, j))]
    return pl.pallas_call(
        functools.partial(_mm_res_body, widths=widths, tn=tn),
        grid=(n // tn, m // RES_TM),
        in_specs=in_specs,
        out_specs=pl.BlockSpec((RES_TM, tn), lambda j, i: (i, j)),
        out_shape=jax.ShapeDtypeStruct((m, n), F32),
        scratch_shapes=[pltpu.VMEM((kt, tn), BF16)],
        compiler_params=_params(),
        name="matmul_residual",
    )(*a_list, w, x)


FFN_TM = 1024
FFN_TN = 512
FFN_SUB = 256


def _gate_up_body(x_ref, wg_ref, wu_ref, o_ref, wgb_ref, wub_ref):
    @pl.when(pl.program_id(1) == 0)
    def _():
        wgb_ref[...] = wg_ref[...].astype(BF16)
        wub_ref[...] = wu_ref[...].astype(BF16)

    for r in range(FFN_TM // FFN_SUB):
        rows = slice(r * FFN_SUB, (r + 1) * FFN_SUB)
        x = x_ref[rows, :]
        for p in range(FFN_TN // MXU_N):
            cols = slice(p * MXU_N, (p + 1) * MXU_N)
            g = jnp.dot(x, wgb_ref[:, cols], preferred_element_type=F32)
            u = jnp.dot(x, wub_ref[:, cols], preferred_element_type=F32)
            o_ref[rows, cols] = (g * (1.0 / (1.0 + jnp.exp(-g))) * u).astype(o_ref.dtype)


def gate_up(xn, wg, wu, li):
    m, k = xn.shape
    n = wg.shape[2]
    return pl.pallas_call(
        _gate_up_body,
        grid=(n // FFN_TN, m // FFN_TM),
        in_specs=[pl.BlockSpec((FFN_TM, k), lambda j, i: (i, 0)),
                  pl.BlockSpec((None, k, FFN_TN), lambda j, i: (li, 0, j)),
                  pl.BlockSpec((None, k, FFN_TN), lambda j, i: (li, 0, j))],
        out_specs=pl.BlockSpec((FFN_TM, FFN_TN), lambda j, i: (i, j)),
        out_shape=jax.ShapeDtypeStruct((m, n), BF16),
        scratch_shapes=[pltpu.VMEM((k, FFN_TN), BF16), pltpu.VMEM((k, FFN_TN), BF16)],
        compiler_params=_params(),
        name="gate_up",
    )(xn, wg, wu)


AB_QA, AB_KA, AB_VA = 0, A_HEADS, 2 * A_HEADS
AB_QB = 3 * A_HEADS
AB_KB = AB_QB + B_Q_HEADS
AB_VB = AB_KB + B_KV_HEADS
AB_KINDS = ("norm",) * (2 * A_HEADS) + ("plain",) * A_HEADS + ("rope",) * (B_Q_HEADS + B_KV_HEADS) \
    + ("plain",) * B_KV_HEADS
CD_QC = 0
CD_KC = C_Q_HEADS
CD_VC = CD_KC + C_KV_HEADS
CD_QD = CD_VC + C_KV_HEADS
CD_KD = CD_QD + D_Q_HEADS
CD_VD = CD_KD + D_SLOTS
CD_KINDS = ("rope",) * (C_Q_HEADS + C_KV_HEADS) + ("plain",) * C_KV_HEADS \
    + ("rope",) * (D_Q_HEADS + D_SLOTS) + ("plain",) * D_SLOTS


NA_QROWS = 2
NA_WIN_ROWS = 10
NA_Q = NA_QROWS * GRID_W
NA_KEYS = NA_WIN_ROWS * GRID_W
NA_BLOCKS = GRID_ROWS // NA_QROWS
NA_VARIANT_BLOCKS = (0, 1, 2, NA_BLOCKS - 2, NA_BLOCKS - 1)


def _na_win_start(rb):
    return np.clip(NA_QROWS * rb - NA_ROWS // 2, 0, GRID_ROWS - NA_WIN_ROWS)


N_DR = 2 * NA_ROWS - 1


def _na_row_slabs():
    dr_idx = np.full((len(NA_VARIANT_BLOCKS), NA_QROWS, NA_WIN_ROWS), N_DR, np.int32)
    for v, rb in enumerate(NA_VARIANT_BLOCKS):
        ws = _na_win_start(rb)
        for qr in range(NA_QROWS):
            qrow = NA_QROWS * rb + qr
            r0 = np.clip(qrow - NA_ROWS // 2, 0, GRID_ROWS - NA_ROWS)
            for kr in range(NA_WIN_ROWS):
                krow = ws + kr
                if r0 <= krow < r0 + NA_ROWS:
                    dr_idx[v, qr, kr] = krow - qrow + NA_ROWS - 1
    return dr_idx


def _na_column_table(rpb):
    assert rpb.shape == (A_HEADS, N_DR, 2 * NA_COLS - 1)
    lanes = 2 * GRID_W
    ext = jnp.concatenate([rpb.astype(F32) * LOG2E,
                           jnp.full((A_HEADS, N_DR, lanes - rpb.shape[2]), NEG, F32)], axis=-1)
    ext = jnp.roll(ext, -(NA_COLS - 1), axis=-1)
    toep = jnp.tile(ext, (1, 1, GRID_W))[:, :, :GRID_W * (lanes - 1)].reshape(A_HEADS, N_DR, GRID_W, lanes - 1)
    toep = toep[..., :GRID_W]
    cols = np.arange(GRID_W)
    c0 = np.clip(cols - NA_COLS // 2, 0, GRID_W - NA_COLS)
    col_ok = (cols[None, :] >= c0[:, None]) & (cols[None, :] < c0[:, None] + NA_COLS)
    toep = jnp.where(col_ok[None, None], toep, NEG)
    toep = jnp.concatenate([toep, jnp.full((A_HEADS, 1, GRID_W, GRID_W), NEG, F32)], axis=1)
    return jnp.concatenate([toep, toep], axis=-1)


NA_UNROLL = 4


def _na_body(q_ref, k_ref, v_ref, c_ref, o_ref, t_ref):
    left_half = lax.broadcasted_iota(jnp.int32, (GRID_W, 2 * GRID_W), 1) < GRID_W
    for v, per_q in enumerate(_na_row_slabs()):
        for qr, slabs in enumerate(per_q):
            for pair in range(NA_WIN_ROWS // 2):
                tile = jnp.where(left_half, c_ref[int(slabs[2 * pair])], c_ref[int(slabs[2 * pair + 1])])
                t_ref[v, qr * GRID_W:(qr + 1) * GRID_W, pair * 2 * GRID_W:(pair + 1) * 2 * GRID_W] = tile

    def block(rb):
        ws = jnp.clip(NA_QROWS * rb - NA_ROWS // 2, 0, GRID_ROWS - NA_WIN_ROWS)
        koff = pl.multiple_of(ws * GRID_W, GRID_W)
        qoff = pl.multiple_of(rb * NA_Q, NA_Q)
        variant = jnp.where(rb < 2, rb, jnp.where(rb >= NA_BLOCKS - 2, rb - (NA_BLOCKS - 5), 2))
        q = q_ref[pl.ds(qoff, NA_Q), :]
        k = k_ref[pl.ds(koff, NA_KEYS), :]
        v = v_ref[pl.ds(koff, NA_KEYS), :]
        s = lax.dot_general(q, k, NT_DIMS, preferred_element_type=F32) + t_ref[variant]
        m = jnp.max(s, axis=-1, keepdims=True)
        p = jnp.exp2(s - m)
        l = jnp.sum(p, axis=-1, keepdims=True)
        o = jnp.dot(p.astype(BF16), v, preferred_element_type=F32)
        o_ref[pl.ds(qoff, NA_Q), :] = (o / l).astype(o_ref.dtype)

    def step(it, carry):
        for u in range(NA_UNROLL):
            block(it * NA_UNROLL + u)
        return carry

    lax.fori_loop(0, NA_BLOCKS // NA_UNROLL, step, 0)


def neighbourhood_attention(h, rpb):
    nv = len(NA_VARIANT_BLOCKS)
    blk = (SEQ, HEAD_DIM)
    return pl.pallas_call(
        _na_body,
        grid=(BATCH, A_HEADS),
        in_specs=[pl.BlockSpec(blk, lambda b, hd: (b, AB_QA + hd)),
                  pl.BlockSpec(blk, lambda b, hd: (b, AB_KA + hd)),
                  pl.BlockSpec(blk, lambda b, hd: (b, AB_VA + hd)),
                  pl.BlockSpec((None, N_DR + 1, GRID_W, 2 * GRID_W), lambda b, hd: (hd, 0, 0, 0))],
        out_specs=pl.BlockSpec(blk, lambda b, hd: (b, hd)),
        out_shape=jax.ShapeDtypeStruct((M_ROWS, A_HEADS * HEAD_DIM), BF16),
        scratch_shapes=[pltpu.VMEM((nv, NA_Q, NA_KEYS), F32)],
        compiler_params=_params(),
        name="neighbourhood_attention",
    )(h, h, h, _na_column_table(rpb))


B_GROUP = B_Q_HEADS // B_KV_HEADS


DENSE_TQ = 256
DENSE_TK = 512


def _dense_body(q_ref, k_ref, v_ref, o_ref):
    tq = DENSE_TQ
    q = jnp.concatenate([q_ref[:, g * HEAD_DIM:(g + 1) * HEAD_DIM] for g in range(B_GROUP)], axis=0)
    ones = jnp.ones((DENSE_TK, HEAD_DIM), BF16)
    m = acc = None
    for c in range(SEQ // DENSE_TK):
        keys = slice(c * DENSE_TK, (c + 1) * DENSE_TK)
        s = lax.dot_general(q, k_ref[keys, :], NT_DIMS, preferred_element_type=F32).astype(BF16)
        v_ones = jnp.concatenate([v_ref[keys, :], ones], axis=1)
        row_max = jnp.max(s, axis=-1, keepdims=True)
        if c == 0:
            m = row_max
            acc = jnp.dot(jnp.exp2(s - m), v_ones, preferred_element_type=F32)
        else:
            m_new = jnp.maximum(m, row_max)
            alpha = jnp.exp2(m.astype(F32) - m_new.astype(F32))
            acc = alpha * acc + jnp.dot(jnp.exp2(s - m_new), v_ones, preferred_element_type=F32)
            m = m_new
    o = acc[:, :HEAD_DIM] / acc[:, HEAD_DIM:]
    for g in range(B_GROUP):
        o_ref[:, g * HEAD_DIM:(g + 1) * HEAD_DIM] = o[g * tq:(g + 1) * tq].astype(o_ref.dtype)


def dense_gqa(h):
    tq = DENSE_TQ
    nq = SEQ // tq
    gw = B_GROUP * HEAD_DIM
    assert AB_QB % B_GROUP == 0
    q_blk0 = AB_QB // B_GROUP
    blk = (SEQ, HEAD_DIM)
    return pl.pallas_call(
        _dense_body,
        grid=(BATCH, B_KV_HEADS, nq),
        in_specs=[pl.BlockSpec((tq, gw), lambda b, kv, i: (b * nq + i, q_blk0 + kv)),
                  pl.BlockSpec(blk, lambda b, kv, i: (b, AB_KB + kv)),
                  pl.BlockSpec(blk, lambda b, kv, i: (b, AB_VB + kv))],
        out_specs=pl.BlockSpec((tq, gw), lambda b, kv, i: (b * nq + i, kv)),
        out_shape=jax.ShapeDtypeStruct((M_ROWS, B_Q_HEADS * HEAD_DIM), BF16),
        compiler_params=_params(),
        name="dense_gqa",
    )(h, h, h)


QBLK = 128
N_QBLK = SEQ // QBLK


def _band_mask_table(span):
    assert span <= QBLK
    col = np.arange(QBLK + 2 * span)[None, :]
    row = np.arange(QBLK)[:, None]
    t = np.stack([np.where(np.abs(col - row + shift) <= span, 0.0, NEG) for shift in (0, -span, -2 * span)])
    return jnp.asarray(t, F32)


def _band_window(ub, blocks, span):
    start = jnp.clip(ub * QBLK - span, 0, blocks * QBLK - (QBLK + 2 * span))
    variant = jnp.where(ub == 0, 0, jnp.where(ub == blocks - 1, 2, 1))
    return start, variant


C_GROUP = C_Q_HEADS // C_KV_HEADS


C_WIN = QBLK + 2 * C_WINDOW
C_UNROLL = 2


def _window_sink_body(sink_ref, q_ref, k_ref, v_ref, mask_ref, o_ref):
    kv = pl.program_id(1)

    def block(i):
        qoff = pl.multiple_of(i * QBLK, QBLK)
        start, variant = _band_window(i, N_QBLK, C_WINDOW)
        koff = pl.multiple_of(start, QBLK)
        mask = mask_ref[variant]
        q = q_ref[pl.ds(qoff, QBLK), :]
        qs = jnp.concatenate([q[:, g * HEAD_DIM:(g + 1) * HEAD_DIM] for g in range(C_GROUP)], axis=0)
        s = lax.dot_general(qs, k_ref[pl.ds(koff, C_WIN), :], NT_DIMS, preferred_element_type=F32)
        ps, ls = [], []
        for g in range(C_GROUP):
            sg = s[g * QBLK:(g + 1) * QBLK] + mask
            sink = sink_ref[kv * C_GROUP + g] * LOG2E
            m = jnp.maximum(jnp.max(sg, axis=-1, keepdims=True), sink)
            p = jnp.exp2(sg - m)
            ls.append(jnp.sum(p, axis=-1, keepdims=True) + jnp.exp2(sink - m))
            ps.append(p.astype(BF16))
        o = jnp.dot(jnp.concatenate(ps, axis=0), v_ref[pl.ds(koff, C_WIN), :], preferred_element_type=F32)
        for g in range(C_GROUP):
            o_ref[pl.ds(qoff, QBLK), g * HEAD_DIM:(g + 1) * HEAD_DIM] = (
                o[g * QBLK:(g + 1) * QBLK] / ls[g]).astype(o_ref.dtype)

    def step(it, carry):
        for u in range(C_UNROLL):
            block(it * C_UNROLL + u)
        return carry

    lax.fori_loop(0, N_QBLK // C_UNROLL, step, 0)


def window_gqa_sink(h, sink):
    gw = C_GROUP * HEAD_DIM
    assert CD_QC % C_GROUP == 0
    q_blk0 = CD_QC // C_GROUP
    blk = (SEQ, HEAD_DIM)
    return pl.pallas_call(
        _window_sink_body,
        grid=(BATCH, C_KV_HEADS),
        in_specs=[pl.BlockSpec(memory_space=pltpu.SMEM),
                  pl.BlockSpec((SEQ, gw), lambda b, kv: (b, q_blk0 + kv)),
                  pl.BlockSpec(blk, lambda b, kv: (b, CD_KC + kv)),
                  pl.BlockSpec(blk, lambda b, kv: (b, CD_VC + kv)),
                  pl.BlockSpec((3, QBLK, C_WIN), lambda b, kv: (0, 0, 0))],
        out_specs=pl.BlockSpec((SEQ, gw), lambda b, kv: (b, kv)),
        out_shape=jax.ShapeDtypeStruct((M_ROWS, C_Q_HEADS * HEAD_DIM), BF16),
        compiler_params=_params(),
        name="window_gqa_sink",
    )(sink.astype(F32), h, h, h, _band_mask_table(C_WINDOW))


D_DILS = tuple(d for _, d in D_DILATIONS)
D_SPAN = D_DILATIONS[0][0] // 2
assert all((w // 2) // d == D_SPAN and N_QBLK % d == 0 for w, d in D_DILATIONS) and D_DILS[0] == 1
D_WIN = QBLK + 2 * D_SPAN
D_UNROLL = 2


def _dilated_body(q0_ref, q1_ref, q2_ref, k_ref, v_ref, mask_ref, o_ref,
                  stage_ref, qc_ref, kc_ref, vc_ref, og_ref, lse_ref):
    def to_class_major(dst_ref, slot, d):
        run = SEQ // d
        for rho in range(d):
            dst_ref[slot, pl.ds(rho * run, run), :] = stage_ref[pl.ds(rho, run, stride=d), :].astype(BF16)

    for src_ref, dst_ref in ((k_ref, kc_ref), (v_ref, vc_ref)):
        stage_ref[...] = src_ref[...].astype(F32)
        for g in range(1, D_GROUPS):
            to_class_major(dst_ref, g - 1, D_DILS[g])
    for g, src_ref in ((1, q1_ref), (2, q2_ref)):
        stage_ref[...] = src_ref[...].astype(F32)
        to_class_major(qc_ref, g - 1, D_DILS[g])

    def block(g, i):
        d = D_DILS[g]
        run_blocks = N_QBLK // d
        rho = i // run_blocks
        ub = i % run_blocks
        qoff = pl.multiple_of(i * QBLK, QBLK)
        start, variant = _band_window(ub, run_blocks, D_SPAN)
        koff = pl.multiple_of(rho * (run_blocks * QBLK) + start, D_SPAN)
        if g == 0:
            q, k, v = q0_ref[pl.ds(qoff, QBLK), :], k_ref[pl.ds(koff, D_WIN), :], v_ref[pl.ds(koff, D_WIN), :]
        else:
            q = qc_ref[g - 1, pl.ds(qoff, QBLK), :]
            k = kc_ref[g - 1, pl.ds(koff, D_WIN), :]
            v = vc_ref[g - 1, pl.ds(koff, D_WIN), :]
        s = lax.dot_general(q, k, NT_DIMS, preferred_element_type=F32) + mask_ref[variant]
        m = jnp.max(s, axis=-1, keepdims=True)
        p = jnp.exp2(s - m)
        l = jnp.sum(p, axis=-1, keepdims=True)
        o = jnp.dot(p.astype(BF16), v, preferred_element_type=F32) / l
        lse = jnp.broadcast_to(m + jnp.log2(l), (QBLK, HEAD_DIM))
        if d == 1:
            rows = pl.ds(qoff, QBLK)
        else:
            rows = pl.ds(rho + d * ub * QBLK, QBLK, stride=d)
        og_ref[g, rows, :] = o
        lse_ref[g, rows, :] = lse

    def step(it, carry):
        for u in range(D_UNROLL):
            for g in range(D_GROUPS):
                block(g, it * D_UNROLL + u)
        return carry

    lax.fori_loop(0, N_QBLK // D_UNROLL, step, 0)

    def merge(c, carry):
        rows = pl.ds(pl.multiple_of(c * QBLK, QBLK), QBLK)
        lses = [lse_ref[g, rows, :] for g in range(D_GROUPS)]
        top = functools.reduce(jnp.maximum, lses)
        ws = [jnp.exp2(x - top) for x in lses]
        num = sum(w * og_ref[g, rows, :] for g, w in enumerate(ws))
        o_ref[rows, :] = (num / sum(ws)).astype(o_ref.dtype)
        return carry

    lax.fori_loop(0, N_QBLK, merge, 0)


def dilated_attention(h):
    blk = (SEQ, HEAD_DIM)
    return pl.pallas_call(
        _dilated_body,
        grid=(BATCH, D_SLOTS),
        in_specs=[pl.BlockSpec(blk, lambda b, s: (b, CD_QD + s)),
                  pl.BlockSpec(blk, lambda b, s: (b, CD_QD + D_SLOTS + s)),
                  pl.BlockSpec(blk, lambda b, s: (b, CD_QD + 2 * D_SLOTS + s)),
                  pl.BlockSpec(blk, lambda b, s: (b, CD_KD + s)),
                  pl.BlockSpec(blk, lambda b, s: (b, CD_VD + s)),
                  pl.BlockSpec((3, QBLK, D_WIN), lambda b, s: (0, 0, 0))],
        out_specs=pl.BlockSpec(blk, lambda b, s: (b, s)),
        out_shape=jax.ShapeDtypeStruct((M_ROWS, D_SLOTS * HEAD_DIM), BF16),
        scratch_shapes=[pltpu.VMEM((SEQ, HEAD_DIM), F32),
                        pltpu.VMEM((D_GROUPS - 1, SEQ, HEAD_DIM), BF16),
                        pltpu.VMEM((D_GROUPS - 1, SEQ, HEAD_DIM), BF16),
                        pltpu.VMEM((D_GROUPS - 1, SEQ, HEAD_DIM), BF16),
                        pltpu.VMEM((D_GROUPS, SEQ, HEAD_DIM), F32),
                        pltpu.VMEM((D_GROUPS, SEQ, HEAD_DIM), F32)],
        compiler_params=_params(),
        name="dilated_attention",
    )(h, h, h, h, h, _band_mask_table(D_SPAN))


def _rope_tables(kind):
    pos = np.arange(SEQ, dtype=np.float64)
    cos = np.ones((SEQ, HEAD_DIM), np.float64)
    sin = np.zeros((SEQ, HEAD_DIM), np.float64)

    def fill(start, r, p, theta):
        half = r // 2
        inv = np.exp(-math.log(theta) * np.arange(half, dtype=np.float64) * (2.0 / r))
        ang = p[:, None] * inv[None, :]
        cos[:, start:start + half] = np.cos(ang)
        cos[:, start + half:start + r] = np.cos(ang)
        sin[:, start:start + half] = -np.sin(ang)
        sin[:, start + half:start + r] = np.sin(ang)

    if kind == "axial":
        hw = HEAD_DIM // 2
        fill(0, hw, np.floor(pos / GRID_W), AXIAL_THETA)
        fill(hw, hw, pos % GRID_W, AXIAL_THETA)
        moves = AXIAL_MOVES
    else:
        fill(0, ROPE_DIMS, pos, ROPE_THETA)
        moves = PARTIAL_MOVES
    perm = _moves_to_perm(moves)
    return moves, jnp.asarray(cos[:, perm], dtype=F32), jnp.asarray(sin[:, perm], dtype=F32)


def _gain_row(parts):
    rows = []
    for g, heads, scale, perm in parts:
        if g is None:
            g = jnp.ones((HEAD_DIM,), F32)
        g = g.astype(F32) * scale
        if perm is not None:
            g = g[perm]
        rows.append(jnp.tile(g, heads))
    return jnp.concatenate(rows).reshape(1, -1)


def _layer_ab(x, xn, w_in, w_out, li, a_qn, a_kn, a_rpb, b_qn, b_kn, axial):
    perm = _moves_to_perm(axial[0])
    gains = _gain_row([(a_qn, A_HEADS, SCALE, None), (a_kn, A_HEADS, 1.0, None), (None, A_HEADS, 1.0, None),
                       (b_qn, B_Q_HEADS, SCALE, perm), (b_kn, B_KV_HEADS, 1.0, perm),
                       (None, B_KV_HEADS, 1.0, None)])
    h = in_proj(xn, w_in, li, AB_KINDS, gains, axial)
    oa = neighbourhood_attention(h, a_rpb)
    ob = dense_gqa(h)
    return matmul_residual([oa, ob], w_out, li, x)


def _layer_cd(x, xn, w_in, w_out, li, c_qn, c_kn, c_sink, d_qn, d_kn, partial):
    perm = _moves_to_perm(partial[0])
    gains = _gain_row([(c_qn, C_Q_HEADS, SCALE, perm), (c_kn, C_KV_HEADS, 1.0, perm), (None, C_KV_HEADS, 1.0, None),
                       (d_qn, D_Q_HEADS, SCALE, perm), (d_kn, D_SLOTS, 1.0, perm), (None, D_SLOTS, 1.0, None)])
    h = in_proj(xn, w_in, li, CD_KINDS, gains, partial)
    oc = window_gqa_sink(h, c_sink)
    od = dilated_attention(h)
    return matmul_residual([oc, od], w_out, li, x)


def kernel(x, attn_norm, ffn_norm, ab_w_in, ab_w_out, a_q_norm, a_k_norm, a_rpb, b_q_norm, b_k_norm,
           cd_w_in, cd_w_out, c_q_norm, c_k_norm, c_sink, d_q_norm, d_k_norm, w_gate, w_up, w_down):
    axial = _rope_tables("axial")
    partial = _rope_tables("partial")
    x = x.reshape(M_ROWS, D_MODEL)
    for layer in range(DEPTH):
        j = layer // 2
        xn = rmsnorm(x, attn_norm[layer])
        if layer % 2 == 0:
            x = _layer_ab(x, xn, ab_w_in, ab_w_out, j, a_q_norm[j], a_k_norm[j], a_rpb[j],
                          b_q_norm[j], b_k_norm[j], axial)
        else:
            x = _layer_cd(x, xn, cd_w_in, cd_w_out, j, c_q_norm[j], c_k_norm[j], c_sink[j],
                          d_q_norm[j], d_k_norm[j], partial)
        xn = rmsnorm(x, ffn_norm[layer])
        hidden = gate_up(xn, w_gate, w_up, layer)
        x = matmul_residual([hidden], w_down, layer, x)
    return x.reshape(BATCH, SEQ, D_MODEL)
```

```python
import functools
import math

import numpy as np
import jax
import jax.numpy as jnp
from jax import lax
from jax.experimental import pallas as pl
from jax.experimental.pallas import tpu as pltpu

D_MODEL = 2048
BATCH = 2
SEQ = 4096
DEPTH = 4
HEAD_DIM = 128
GRID_W = 64
GRID_ROWS = SEQ // GRID_W
EPS = 1e-6
A_HEADS = 8
NA_ROWS = 8
NA_COLS = 16
B_Q_HEADS = 8
B_KV_HEADS = 2
AXIAL_THETA = 10000.0
C_Q_HEADS = 12
C_KV_HEADS = 4
C_WINDOW = 128
D_DILATIONS = ((128, 1), (512, 4), (2048, 16))
D_GROUPS = len(D_DILATIONS)
D_SLOTS = 4
D_Q_HEADS = D_GROUPS * D_SLOTS
ROPE_THETA = 500000.0
ROPE_DIMS = HEAD_DIM // 4
FFN_HIDDEN = ((-(-8 * D_MODEL // 3)) + 255) // 256 * 256
M_ROWS = BATCH * SEQ
LOG2E = math.log2(math.e)
SCALE = HEAD_DIM ** -0.5 * LOG2E
NEG = -1e30

VMEM_LIMIT_BYTES = 52 * 1024 * 1024
MXU_N = 256

F32 = jnp.float32
BF16 = jnp.bfloat16
NT_DIMS = (((1,), (1,)), ((), ()))


def _params(**kw):
    return pltpu.CompilerParams(vmem_limit_bytes=VMEM_LIMIT_BYTES, **kw)


LANES = 128


def _lane_partial_sums(sq):
    out = sq[:, :LANES]
    for i in range(1, sq.shape[1] // LANES):
        out = out + sq[:, i * LANES:(i + 1) * LANES]
    return out


def _row_scale(ssq_ref, rows, dim):
    part = ssq_ref[0, rows, :]
    for t in range(1, ssq_ref.shape[0]):
        part = part + ssq_ref[t, rows, :]
    return lax.rsqrt(jnp.sum(part, axis=-1, keepdims=True) * (1.0 / dim) + EPS)


def _prenorm_body(x_ref, g_ref, xg_ref, ssq_ref):
    x = x_ref[...]
    xg_ref[...] = (x * g_ref[...]).astype(xg_ref.dtype)
    ssq_ref[...] = _lane_partial_sums(x * x)


def prenorm(x, g, tm=512):
    m, d = x.shape
    return pl.pallas_call(
        _prenorm_body,
        grid=(m // tm,),
        in_specs=[pl.BlockSpec((tm, d), lambda i: (i, 0)), pl.BlockSpec((1, d), lambda i: (0, 0))],
        out_specs=[pl.BlockSpec((tm, d), lambda i: (i, 0)), pl.BlockSpec((None, tm, LANES), lambda i: (0, i, 0))],
        out_shape=[jax.ShapeDtypeStruct((m, d), BF16), jax.ShapeDtypeStruct((1, m, LANES), F32)],
        compiler_params=_params(),
        name="prenorm",
    )(x, g.reshape(1, d).astype(F32))


HALF_LANES = HEAD_DIM // 2
AXIAL_MOVES = ((96, 32, 64), (32, 64, 96))
PARTIAL_MOVES = ((112, 16, 64), (48, 64, 80))


def _moves_to_perm(moves):
    perm = np.arange(HEAD_DIM)
    for shift, lo, hi in moves:
        perm[lo:hi] = (np.arange(lo, hi) - shift) % HEAD_DIM
    assert sorted(perm.tolist()) == list(range(HEAD_DIM))
    return perm


def _permute_lanes(w, moves):
    lane = lax.broadcasted_iota(jnp.int32, w.shape, 1)
    out = w
    for shift, lo, hi in moves:
        out = jnp.where((lane >= lo) & (lane < hi), pltpu.roll(w, shift, 1), out)
    return out


PROJ_TM = 2048
PROJ_TN = 512
PROJ_SUB = 256


def _in_proj_body(sig_ref, x_ref, ssq_ref, w_ref, g_ref, cos_ref, sin_ref, o_ref, wbf_ref, *, sigs, moves):
    n = pl.program_id(0)
    m = pl.program_id(1)
    heads = PROJ_TN // HEAD_DIM
    dim = x_ref.shape[1]

    def cast_weights(kinds):
        for h in range(heads):
            cols = slice(h * HEAD_DIM, (h + 1) * HEAD_DIM)
            w = w_ref[:, cols]
            if kinds[h] == "rope":
                w = _permute_lanes(w, moves)
            wbf_ref[:, cols] = w.astype(BF16)

    def compute(kinds):
        for r in range(PROJ_TM // PROJ_SUB):
            rows = slice(r * PROJ_SUB, (r + 1) * PROJ_SUB)
            x = x_ref[rows, :]
            inv = _row_scale(ssq_ref, rows, dim)
            for p in range(PROJ_TN // MXU_N):
                acc = jnp.dot(x, wbf_ref[:, p * MXU_N:(p + 1) * MXU_N], preferred_element_type=F32)
                for hh in range(MXU_N // HEAD_DIM):
                    h = p * (MXU_N // HEAD_DIM) + hh
                    cols = slice(h * HEAD_DIM, (h + 1) * HEAD_DIM)
                    y = acc[:, hh * HEAD_DIM:(hh + 1) * HEAD_DIM]
                    if kinds[h] == "plain":
                        y = y * inv
                    else:
                        ms = jnp.mean(y * y, axis=-1, keepdims=True)
                        y = y * (inv * lax.rsqrt(ms * (inv * inv) + EPS)) * g_ref[:, cols]
                    if kinds[h] == "rope":
                        y = y * cos_ref[rows, :] + pltpu.roll(y, HALF_LANES, 1) * sin_ref[rows, :]
                    o_ref[rows, cols] = y.astype(o_ref.dtype)

    for sid, kinds in enumerate(sigs):
        @pl.when(sig_ref[n] == sid)
        def _(kinds=kinds):
            @pl.when(m == 0)
            def _():
                cast_weights(kinds)

            compute(kinds)


def in_proj(xg, ssq, w, li, kinds, gains, rope):
    m, k = xg.shape
    n = w.shape[2]
    heads = PROJ_TN // HEAD_DIM
    tiles = [tuple(kinds[i:i + heads]) for i in range(0, len(kinds), heads)]
    sigs = tuple(dict.fromkeys(tiles))
    sig_ids = jnp.asarray([sigs.index(t) for t in tiles], jnp.int32)
    moves, cos, sin = rope
    pos_blocks = SEQ // PROJ_TM
    return pl.pallas_call(
        functools.partial(_in_proj_body, sigs=sigs, moves=moves),
        grid=(n // PROJ_TN, m // PROJ_TM),
        in_specs=[pl.BlockSpec(memory_space=pltpu.SMEM),
                  pl.BlockSpec((PROJ_TM, k), lambda j, i: (i, 0)),
                  pl.BlockSpec((ssq.shape[0], PROJ_TM, LANES), lambda j, i: (0, i, 0)),
                  pl.BlockSpec((None, k, PROJ_TN), lambda j, i: (li, 0, j)),
                  pl.BlockSpec((1, PROJ_TN), lambda j, i: (0, j)),
                  pl.BlockSpec((PROJ_TM, HEAD_DIM), lambda j, i: (i % pos_blocks, 0)),
                  pl.BlockSpec((PROJ_TM, HEAD_DIM), lambda j, i: (i % pos_blocks, 0))],
        out_specs=pl.BlockSpec((PROJ_TM, PROJ_TN), lambda j, i: (i, j)),
        out_shape=jax.ShapeDtypeStruct((m, n), BF16),
        scratch_shapes=[pltpu.VMEM((k, PROJ_TN), BF16)],
        compiler_params=_params(),
        name="in_proj",
    )(sig_ids, xg, ssq, w, gains, cos, sin)


RES_TM = 512
RES_SUB = 256
RES_VMEM_WEIGHT_BYTES = 12 * 1024 * 1024


def _res_tn(kt, n):
    tn = n
    while kt * tn * 4 > RES_VMEM_WEIGHT_BYTES and tn % (2 * MXU_N) == 0:
        tn //= 2
    return tn


def _mm_res_body(*refs, n_in, tn, emit_norm):
    a_refs = refs[:n_in]
    if emit_norm:
        w_ref, x_ref, g_ref, o_ref, xg_ref, ssq_ref, wbf_ref = refs[n_in:]
    else:
        w_ref, x_ref, o_ref, wbf_ref = refs[n_in:]

    @pl.when(pl.program_id(1) == 0)
    def _():
        wbf_ref[...] = w_ref[...].astype(BF16)

    for r in range(RES_TM // RES_SUB):
        rows = slice(r * RES_SUB, (r + 1) * RES_SUB)
        a = a_refs[0][rows, :] if n_in == 1 else jnp.concatenate([a_ref[rows, :] for a_ref in a_refs], axis=1)
        part = None
        for p in range(tn // MXU_N):
            cols = slice(p * MXU_N, (p + 1) * MXU_N)
            acc = x_ref[rows, cols] + jnp.dot(a, wbf_ref[:, cols], preferred_element_type=F32)
            o_ref[rows, cols] = acc
            if emit_norm:
                xg_ref[rows, cols] = (acc * g_ref[:, cols]).astype(xg_ref.dtype)
                sq = _lane_partial_sums(acc * acc)
                part = sq if part is None else part + sq
        if emit_norm:
            ssq_ref[rows, :] = part


def matmul_residual(a_list, w, li, x, g_next=None):
    m, n = x.shape
    widths = tuple(a.shape[1] for a in a_list)
    kt = sum(widths)
    assert kt == w.shape[1]
    tn = _res_tn(kt, n)
    emit_norm = g_next is not None
    tile = pl.BlockSpec((RES_TM, tn), lambda j, i: (i, j))
    in_specs = [pl.BlockSpec((RES_TM, ka), lambda j, i: (i, 0)) for ka in widths]
    in_specs += [pl.BlockSpec((None, kt, tn), lambda j, i: (li, 0, j)), tile]
    args = [*a_list, w, x]
    out_specs, out_shape = tile, jax.ShapeDtypeStruct((m, n), F32)
    if emit_norm:
        in_specs.append(pl.BlockSpec((1, tn), lambda j, i: (0, j)))
        args.append(g_next.reshape(1, n).astype(F32))
        out_specs = [tile, tile, pl.BlockSpec((None, RES_TM, LANES), lambda j, i: (j, i, 0))]
        out_shape = [out_shape, jax.ShapeDtypeStruct((m, n), BF16), jax.ShapeDtypeStruct((n // tn, m, LANES), F32)]
    return pl.pallas_call(
        functools.partial(_mm_res_body, n_in=len(a_list), tn=tn, emit_norm=emit_norm),
        grid=(n // tn, m // RES_TM),
        in_specs=in_specs,
        out_specs=out_specs,
        out_shape=out_shape,
        scratch_shapes=[pltpu.VMEM((kt, tn), BF16)],
        compiler_params=_params(),
        name="matmul_residual",
    )(*args)


FFN_TM = 1024
FFN_TN = 512
FFN_SUB = 256


def _gate_up_body(x_ref, ssq_ref, wg_ref, wu_ref, o_ref, wgb_ref, wub_ref):
    @pl.when(pl.program_id(1) == 0)
    def _():
        wgb_ref[...] = wg_ref[...].astype(BF16)
        wub_ref[...] = wu_ref[...].astype(BF16)

    for r in range(FFN_TM // FFN_SUB):
        rows = slice(r * FFN_SUB, (r + 1) * FFN_SUB)
        x = x_ref[rows, :]
        inv = _row_scale(ssq_ref, rows, x_ref.shape[1])
        for p in range(FFN_TN // MXU_N):
            cols = slice(p * MXU_N, (p + 1) * MXU_N)
            g = jnp.dot(x, wgb_ref[:, cols], preferred_element_type=F32) * inv
            u = jnp.dot(x, wub_ref[:, cols], preferred_element_type=F32) * inv
            o_ref[rows, cols] = (g * (1.0 / (1.0 + jnp.exp(-g))) * u).astype(o_ref.dtype)


def gate_up(xg, ssq, wg, wu, li):
    m, k = xg.shape
    n = wg.shape[2]
    return pl.pallas_call(
        _gate_up_body,
        grid=(n // FFN_TN, m // FFN_TM),
        in_specs=[pl.BlockSpec((FFN_TM, k), lambda j, i: (i, 0)),
                  pl.BlockSpec((ssq.shape[0], FFN_TM, LANES), lambda j, i: (0, i, 0)),
                  pl.BlockSpec((None, k, FFN_TN), lambda j, i: (li, 0, j)),
                  pl.BlockSpec((None, k, FFN_TN), lambda j, i: (li, 0, j))],
        out_specs=pl.BlockSpec((FFN_TM, FFN_TN), lambda j, i: (i, j)),
        out_shape=jax.ShapeDtypeStruct((m, n), BF16),
        scratch_shapes=[pltpu.VMEM((k, FFN_TN), BF16), pltpu.VMEM((k, FFN_TN), BF16)],
        compiler_params=_params(),
        name="gate_up",
    )(xg, ssq, wg, wu)


AB_QA, AB_KA, AB_VA = 0, A_HEADS, 2 * A_HEADS
AB_QB = 3 * A_HEADS
AB_KB = AB_QB + B_Q_HEADS
AB_VB = AB_KB + B_KV_HEADS
AB_KINDS = ("norm",) * (2 * A_HEADS) + ("plain",) * A_HEADS + ("rope",) * (B_Q_HEADS + B_KV_HEADS) \
    + ("plain",) * B_KV_HEADS
CD_QC = 0
CD_KC = C_Q_HEADS
CD_VC = CD_KC + C_KV_HEADS
CD_QD = CD_VC + C_KV_HEADS
CD_KD = CD_QD + D_Q_HEADS
CD_VD = CD_KD + D_SLOTS
CD_KINDS = ("rope",) * (C_Q_HEADS + C_KV_HEADS) + ("plain",) * C_KV_HEADS \
    + ("rope",) * (D_Q_HEADS + D_SLOTS) + ("plain",) * D_SLOTS


NA_QROWS = 2
NA_WIN_ROWS = 10
NA_Q = NA_QROWS * GRID_W
NA_KEYS = NA_WIN_ROWS * GRID_W
NA_BLOCKS = GRID_ROWS // NA_QROWS
NA_VARIANT_BLOCKS = (0, 1, 2, NA_BLOCKS - 2, NA_BLOCKS - 1)


def _na_win_start(rb):
    return np.clip(NA_QROWS * rb - NA_ROWS // 2, 0, GRID_ROWS - NA_WIN_ROWS)


N_DR = 2 * NA_ROWS - 1


def _na_row_slabs():
    dr_idx = np.full((len(NA_VARIANT_BLOCKS), NA_QROWS, NA_WIN_ROWS), N_DR, np.int32)
    for v, rb in enumerate(NA_VARIANT_BLOCKS):
        ws = _na_win_start(rb)
        for qr in range(NA_QROWS):
            qrow = NA_QROWS * rb + qr
            r0 = np.clip(qrow - NA_ROWS // 2, 0, GRID_ROWS - NA_ROWS)
            for kr in range(NA_WIN_ROWS):
                krow = ws + kr
                if r0 <= krow < r0 + NA_ROWS:
                    dr_idx[v, qr, kr] = krow - qrow + NA_ROWS - 1
    return dr_idx


def _na_column_table(rpb):
    assert rpb.shape == (A_HEADS, N_DR, 2 * NA_COLS - 1)
    lanes = 2 * GRID_W
    ext = jnp.concatenate([rpb.astype(F32) * LOG2E,
                           jnp.full((A_HEADS, N_DR, lanes - rpb.shape[2]), NEG, F32)], axis=-1)
    ext = jnp.roll(ext, -(NA_COLS - 1), axis=-1)
    toep = jnp.tile(ext, (1, 1, GRID_W))[:, :, :GRID_W * (lanes - 1)].reshape(A_HEADS, N_DR, GRID_W, lanes - 1)
    toep = toep[..., :GRID_W]
    cols = np.arange(GRID_W)
    c0 = np.clip(cols - NA_COLS // 2, 0, GRID_W - NA_COLS)
    col_ok = (cols[None, :] >= c0[:, None]) & (cols[None, :] < c0[:, None] + NA_COLS)
    toep = jnp.where(col_ok[None, None], toep, NEG)
    toep = jnp.concatenate([toep, jnp.full((A_HEADS, 1, GRID_W, GRID_W), NEG, F32)], axis=1)
    return jnp.concatenate([toep, toep], axis=-1)


NA_UNROLL = 4


def _na_body(q_ref, k_ref, v_ref, c_ref, o_ref, t_ref):
    left_half = lax.broadcasted_iota(jnp.int32, (GRID_W, 2 * GRID_W), 1) < GRID_W
    for v, per_q in enumerate(_na_row_slabs()):
        for qr, slabs in enumerate(per_q):
            for pair in range(NA_WIN_ROWS // 2):
                tile = jnp.where(left_half, c_ref[int(slabs[2 * pair])], c_ref[int(slabs[2 * pair + 1])])
                t_ref[v, qr * GRID_W:(qr + 1) * GRID_W, pair * 2 * GRID_W:(pair + 1) * 2 * GRID_W] = tile

    def block(rb):
        ws = jnp.clip(NA_QROWS * rb - NA_ROWS // 2, 0, GRID_ROWS - NA_WIN_ROWS)
        koff = pl.multiple_of(ws * GRID_W, GRID_W)
        qoff = pl.multiple_of(rb * NA_Q, NA_Q)
        variant = jnp.where(rb < 2, rb, jnp.where(rb >= NA_BLOCKS - 2, rb - (NA_BLOCKS - 5), 2))
        q = q_ref[pl.ds(qoff, NA_Q), :]
        k = k_ref[pl.ds(koff, NA_KEYS), :]
        v = v_ref[pl.ds(koff, NA_KEYS), :]
        s = lax.dot_general(q, k, NT_DIMS, preferred_element_type=F32) + t_ref[variant]
        m = jnp.max(s, axis=-1, keepdims=True)
        p = jnp.exp2(s - m)
        l = jnp.sum(p, axis=-1, keepdims=True)
        o = jnp.dot(p.astype(BF16), v, preferred_element_type=F32)
        o_ref[pl.ds(qoff, NA_Q), :] = (o / l).astype(o_ref.dtype)

    def step(it, carry):
        for u in range(NA_UNROLL):
            block(it * NA_UNROLL + u)
        return carry

    lax.fori_loop(0, NA_BLOCKS // NA_UNROLL, step, 0)


def neighbourhood_attention(h, rpb):
    nv = len(NA_VARIANT_BLOCKS)
    blk = (SEQ, HEAD_DIM)
    return pl.pallas_call(
        _na_body,
        grid=(BATCH, A_HEADS),
        in_specs=[pl.BlockSpec(blk, lambda b, hd: (b, AB_QA + hd)),
                  pl.BlockSpec(blk, lambda b, hd: (b, AB_KA + hd)),
                  pl.BlockSpec(blk, lambda b, hd: (b, AB_VA + hd)),
                  pl.BlockSpec((None, N_DR + 1, GRID_W, 2 * GRID_W), lambda b, hd: (hd, 0, 0, 0))],
        out_specs=pl.BlockSpec(blk, lambda b, hd: (b, hd)),
        out_shape=jax.ShapeDtypeStruct((M_ROWS, A_HEADS * HEAD_DIM), BF16),
        scratch_shapes=[pltpu.VMEM((nv, NA_Q, NA_KEYS), F32)],
        compiler_params=_params(),
        name="neighbourhood_attention",
    )(h, h, h, _na_column_table(rpb))


B_GROUP = B_Q_HEADS // B_KV_HEADS


DENSE_TQ = 256
DENSE_TK = 512


def _dense_body(q_ref, k_ref, v_ref, o_ref):
    tq = DENSE_TQ
    q = jnp.concatenate([q_ref[:, g * HEAD_DIM:(g + 1) * HEAD_DIM] for g in range(B_GROUP)], axis=0)
    ones = jnp.ones((DENSE_TK, HEAD_DIM), BF16)
    m = acc = None
    for c in range(SEQ // DENSE_TK):
        keys = slice(c * DENSE_TK, (c + 1) * DENSE_TK)
        s = lax.dot_general(q, k_ref[keys, :], NT_DIMS, preferred_element_type=F32).astype(BF16)
        v_ones = jnp.concatenate([v_ref[keys, :], ones], axis=1)
        row_max = jnp.max(s, axis=-1, keepdims=True)
        if c == 0:
            m = row_max
            acc = jnp.dot(jnp.exp2(s - m), v_ones, preferred_element_type=F32)
        else:
            m_new = jnp.maximum(m, row_max)
            alpha = jnp.exp2(m.astype(F32) - m_new.astype(F32))
            acc = alpha * acc + jnp.dot(jnp.exp2(s - m_new), v_ones, preferred_element_type=F32)
            m = m_new
    o = acc[:, :HEAD_DIM] / acc[:, HEAD_DIM:]
    for g in range(B_GROUP):
        o_ref[:, g * HEAD_DIM:(g + 1) * HEAD_DIM] = o[g * tq:(g + 1) * tq].astype(o_ref.dtype)


def dense_gqa(h):
    tq = DENSE_TQ
    nq = SEQ // tq
    gw = B_GROUP * HEAD_DIM
    assert AB_QB % B_GROUP == 0
    q_blk0 = AB_QB // B_GROUP
    blk = (SEQ, HEAD_DIM)
    return pl.pallas_call(
        _dense_body,
        grid=(BATCH, B_KV_HEADS, nq),
        in_specs=[pl.BlockSpec((tq, gw), lambda b, kv, i: (b * nq + i, q_blk0 + kv)),
                  pl.BlockSpec(blk, lambda b, kv, i: (b, AB_KB + kv)),
                  pl.BlockSpec(blk, lambda b, kv, i: (b, AB_VB + kv))],
        out_specs=pl.BlockSpec((tq, gw), lambda b, kv, i: (b * nq + i, kv)),
        out_shape=jax.ShapeDtypeStruct((M_ROWS, B_Q_HEADS * HEAD_DIM), BF16),
        compiler_params=_params(),
        name="dense_gqa",
    )(h, h, h)


QBLK = 128
N_QBLK = SEQ // QBLK


def _band_mask_table(span):
    assert span <= QBLK
    col = np.arange(QBLK + 2 * span)[None, :]
    row = np.arange(QBLK)[:, None]
    t = np.stack([np.where(np.abs(col - row + shift) <= span, 0.0, NEG) for shift in (0, -span, -2 * span)])
    return jnp.asarray(t, F32)


def _band_window(ub, blocks, span):
    start = jnp.clip(ub * QBLK - span, 0, blocks * QBLK - (QBLK + 2 * span))
    variant = jnp.where(ub == 0, 0, jnp.where(ub == blocks - 1, 2, 1))
    return start, variant


C_GROUP = C_Q_HEADS // C_KV_HEADS


C_WIN = QBLK + 2 * C_WINDOW
C_UNROLL = 2


def _window_sink_body(sink_ref, q_ref, k_ref, v_ref, mask_ref, o_ref):
    kv = pl.program_id(1)

    def block(i):
        qoff = pl.multiple_of(i * QBLK, QBLK)
        start, variant = _band_window(i, N_QBLK, C_WINDOW)
        koff = pl.multiple_of(start, QBLK)
        mask = mask_ref[variant]
        q = q_ref[pl.ds(qoff, QBLK), :]
        qs = jnp.concatenate([q[:, g * HEAD_DIM:(g + 1) * HEAD_DIM] for g in range(C_GROUP)], axis=0)
        s = lax.dot_general(qs, k_ref[pl.ds(koff, C_WIN), :], NT_DIMS, preferred_element_type=F32)
        ps, ls = [], []
        for g in range(C_GROUP):
            sg = s[g * QBLK:(g + 1) * QBLK] + mask
            sink = sink_ref[kv * C_GROUP + g] * LOG2E
            m = jnp.maximum(jnp.max(sg, axis=-1, keepdims=True), sink)
            p = jnp.exp2(sg - m)
            ls.append(jnp.sum(p, axis=-1, keepdims=True) + jnp.exp2(sink - m))
            ps.append(p.astype(BF16))
        o = jnp.dot(jnp.concatenate(ps, axis=0), v_ref[pl.ds(koff, C_WIN), :], preferred_element_type=F32)
        for g in range(C_GROUP):
            o_ref[pl.ds(qoff, QBLK), g * HEAD_DIM:(g + 1) * HEAD_DIM] = (
                o[g * QBLK:(g + 1) * QBLK] / ls[g]).astype(o_ref.dtype)

    def step(it, carry):
        for u in range(C_UNROLL):
            block(it * C_UNROLL + u)
        return carry

    lax.fori_loop(0, N_QBLK // C_UNROLL, step, 0)


def window_gqa_sink(h, sink):
    gw = C_GROUP * HEAD_DIM
    assert CD_QC % C_GROUP == 0
    q_blk0 = CD_QC // C_GROUP
    blk = (SEQ, HEAD_DIM)
    return pl.pallas_call(
        _window_sink_body,
        grid=(BATCH, C_KV_HEADS),
        in_specs=[pl.BlockSpec(memory_space=pltpu.SMEM),
                  pl.BlockSpec((SEQ, gw), lambda b, kv: (b, q_blk0 + kv)),
                  pl.BlockSpec(blk, lambda b, kv: (b, CD_KC + kv)),
                  pl.BlockSpec(blk, lambda b, kv: (b, CD_VC + kv)),
                  pl.BlockSpec((3, QBLK, C_WIN), lambda b, kv: (0, 0, 0))],
        out_specs=pl.BlockSpec((SEQ, gw), lambda b, kv: (b, kv)),
        out_shape=jax.ShapeDtypeStruct((M_ROWS, C_Q_HEADS * HEAD_DIM), BF16),
        compiler_params=_params(),
        name="window_gqa_sink",
    )(sink.astype(F32), h, h, h, _band_mask_table(C_WINDOW))


D_DILS = tuple(d for _, d in D_DILATIONS)
D_SPAN = D_DILATIONS[0][0] // 2
assert all((w // 2) // d == D_SPAN and N_QBLK % d == 0 for w, d in D_DILATIONS) and D_DILS[0] == 1
D_WIN = QBLK + 2 * D_SPAN
D_UNROLL = 2


def _dilated_body(q0_ref, q1_ref, q2_ref, k_ref, v_ref, mask_ref, o_ref,
                  stage_ref, qc_ref, kc_ref, vc_ref, og_ref, lse_ref):
    def to_class_major(dst_ref, slot, d):
        run = SEQ // d
        for rho in range(d):
            dst_ref[slot, pl.ds(rho * run, run), :] = stage_ref[pl.ds(rho, run, stride=d), :].astype(BF16)

    for src_ref, dst_ref in ((k_ref, kc_ref), (v_ref, vc_ref)):
        stage_ref[...] = src_ref[...].astype(F32)
        for g in range(1, D_GROUPS):
            to_class_major(dst_ref, g - 1, D_DILS[g])
    for g, src_ref in ((1, q1_ref), (2, q2_ref)):
        stage_ref[...] = src_ref[...].astype(F32)
        to_class_major(qc_ref, g - 1, D_DILS[g])

    def block(g, i):
        d = D_DILS[g]
        run_blocks = N_QBLK // d
        rho = i // run_blocks
        ub = i % run_blocks
        qoff = pl.multiple_of(i * QBLK, QBLK)
        start, variant = _band_window(ub, run_blocks, D_SPAN)
        koff = pl.multiple_of(rho * (run_blocks * QBLK) + start, D_SPAN)
        if g == 0:
            q, k, v = q0_ref[pl.ds(qoff, QBLK), :], k_ref[pl.ds(koff, D_WIN), :], v_ref[pl.ds(koff, D_WIN), :]
        else:
            q = qc_ref[g - 1, pl.ds(qoff, QBLK), :]
            k = kc_ref[g - 1, pl.ds(koff, D_WIN), :]
            v = vc_ref[g - 1, pl.ds(koff, D_WIN), :]
        s = lax.dot_general(q, k, NT_DIMS, preferred_element_type=F32) + mask_ref[variant]
        m = jnp.max(s, axis=-1, keepdims=True)
        p = jnp.exp2(s - m)
        l = jnp.sum(p, axis=-1, keepdims=True)
        o = jnp.dot(p.astype(BF16), v, preferred_element_type=F32) / l
        lse = jnp.broadcast_to(m + jnp.log2(l), (QBLK, HEAD_DIM))
        if d == 1:
            rows = pl.ds(qoff, QBLK)
        else:
            rows = pl.ds(rho + d * ub * QBLK, QBLK, stride=d)
        og_ref[g, rows, :] = o
        lse_ref[g, rows, :] = lse

    def step(it, carry):
        for u in range(D_UNROLL):
            for g in range(D_GROUPS):
                block(g, it * D_UNROLL + u)
        return carry

    lax.fori_loop(0, N_QBLK // D_UNROLL, step, 0)

    def merge(c, carry):
        rows = pl.ds(pl.multiple_of(c * QBLK, QBLK), QBLK)
        lses = [lse_ref[g, rows, :] for g in range(D_GROUPS)]
        top = functools.reduce(jnp.maximum, lses)
        ws = [jnp.exp2(x - top) for x in lses]
        num = sum(w * og_ref[g, rows, :] for g, w in enumerate(ws))
        o_ref[rows, :] = (num / sum(ws)).astype(o_ref.dtype)
        return carry

    lax.fori_loop(0, N_QBLK, merge, 0)


def dilated_attention(h):
    blk = (SEQ, HEAD_DIM)
    return pl.pallas_call(
        _dilated_body,
        grid=(BATCH, D_SLOTS),
        in_specs=[pl.BlockSpec(blk, lambda b, s: (b, CD_QD + s)),
                  pl.BlockSpec(blk, lambda b, s: (b, CD_QD + D_SLOTS + s)),
                  pl.BlockSpec(blk, lambda b, s: (b, CD_QD + 2 * D_SLOTS + s)),
                  pl.BlockSpec(blk, lambda b, s: (b, CD_KD + s)),
                  pl.BlockSpec(blk, lambda b, s: (b, CD_VD + s)),
                  pl.BlockSpec((3, QBLK, D_WIN), lambda b, s: (0, 0, 0))],
        out_specs=pl.BlockSpec(blk, lambda b, s: (b, s)),
        out_shape=jax.ShapeDtypeStruct((M_ROWS, D_SLOTS * HEAD_DIM), BF16),
        scratch_shapes=[pltpu.VMEM((SEQ, HEAD_DIM), F32),
                        pltpu.VMEM((D_GROUPS - 1, SEQ, HEAD_DIM), BF16),
                        pltpu.VMEM((D_GROUPS - 1, SEQ, HEAD_DIM), BF16),
                        pltpu.VMEM((D_GROUPS - 1, SEQ, HEAD_DIM), BF16),
                        pltpu.VMEM((D_GROUPS, SEQ, HEAD_DIM), F32),
                        pltpu.VMEM((D_GROUPS, SEQ, HEAD_DIM), F32)],
        compiler_params=_params(),
        name="dilated_attention",
    )(h, h, h, h, h, _band_mask_table(D_SPAN))


def _rope_tables(kind):
    pos = np.arange(SEQ, dtype=np.float64)
    cos = np.ones((SEQ, HEAD_DIM), np.float64)
    sin = np.zeros((SEQ, HEAD_DIM), np.float64)

    def fill(start, r, p, theta):
        half = r // 2
        inv = np.exp(-math.log(theta) * np.arange(half, dtype=np.float64) * (2.0 / r))
        ang = p[:, None] * inv[None, :]
        cos[:, start:start + half] = np.cos(ang)
        cos[:, start + half:start + r] = np.cos(ang)
        sin[:, start:start + half] = -np.sin(ang)
        sin[:, start + half:start + r] = np.sin(ang)

    if kind == "axial":
        hw = HEAD_DIM // 2
        fill(0, hw, np.floor(pos / GRID_W), AXIAL_THETA)
        fill(hw, hw, pos % GRID_W, AXIAL_THETA)
        moves = AXIAL_MOVES
    else:
        fill(0, ROPE_DIMS, pos, ROPE_THETA)
        moves = PARTIAL_MOVES
    perm = _moves_to_perm(moves)
    return moves, jnp.asarray(cos[:, perm], dtype=F32), jnp.asarray(sin[:, perm], dtype=F32)


def _gain_row(parts):
    rows = []
    for g, heads, scale, perm in parts:
        if g is None:
            g = jnp.ones((HEAD_DIM,), F32)
        g = g.astype(F32) * scale
        if perm is not None:
            g = g[perm]
        rows.append(jnp.tile(g, heads))
    return jnp.concatenate(rows).reshape(1, -1)


def _layer_ab(x, xg, ssq, w_in, w_out, li, g_next, a_qn, a_kn, a_rpb, b_qn, b_kn, axial):
    perm = _moves_to_perm(axial[0])
    gains = _gain_row([(a_qn, A_HEADS, SCALE, None), (a_kn, A_HEADS, 1.0, None), (None, A_HEADS, 1.0, None),
                       (b_qn, B_Q_HEADS, SCALE, perm), (b_kn, B_KV_HEADS, 1.0, perm),
                       (None, B_KV_HEADS, 1.0, None)])
    h = in_proj(xg, ssq, w_in, li, AB_KINDS, gains, axial)
    oa = neighbourhood_attention(h, a_rpb)
    ob = dense_gqa(h)
    return matmul_residual([oa, ob], w_out, li, x, g_next)


def _layer_cd(x, xg, ssq, w_in, w_out, li, g_next, c_qn, c_kn, c_sink, d_qn, d_kn, partial):
    perm = _moves_to_perm(partial[0])
    gains = _gain_row([(c_qn, C_Q_HEADS, SCALE, perm), (c_kn, C_KV_HEADS, 1.0, perm), (None, C_KV_HEADS, 1.0, None),
                       (d_qn, D_Q_HEADS, SCALE, perm), (d_kn, D_SLOTS, 1.0, perm), (None, D_SLOTS, 1.0, None)])
    h = in_proj(xg, ssq, w_in, li, CD_KINDS, gains, partial)
    oc = window_gqa_sink(h, c_sink)
    od = dilated_attention(h)
    return matmul_residual([oc, od], w_out, li, x, g_next)


def kernel(x, attn_norm, ffn_norm, ab_w_in, ab_w_out, a_q_norm, a_k_norm, a_rpb, b_q_norm, b_k_norm,
           cd_w_in, cd_w_out, c_q_norm, c_k_norm, c_sink, d_q_norm, d_k_norm, w_gate, w_up, w_down):
    axial = _rope_tables("axial")
    partial = _rope_tables("partial")
    x = x.reshape(M_ROWS, D_MODEL)
    xg, ssq = prenorm(x, attn_norm[0])
    for layer in range(DEPTH):
        j = layer // 2
        if layer % 2 == 0:
            x, xg, ssq = _layer_ab(x, xg, ssq, ab_w_in, ab_w_out, j, ffn_norm[layer], a_q_norm[j], a_k_norm[j],
                                   a_rpb[j], b_q_norm[j], b_k_norm[j], axial)
        else:
            x, xg, ssq = _layer_cd(x, xg, ssq, cd_w_in, cd_w_out, j, ffn_norm[layer], c_q_norm[j], c_k_norm[j],
                                   c_sink[j], d_q_norm[j], d_k_norm[j], partial)
        hidden = gate_up(xg, ssq, w_gate, w_up, layer)
        if layer + 1 < DEPTH:
            x, xg, ssq = matmul_residual([hidden], w_down, layer, x, attn_norm[layer + 1])
        else:
            x = matmul_residual([hidden], w_down, layer, x)
    return x.reshape(BATCH, SEQ, D_MODEL)
```

```python
import functools
import math

import numpy as np
import jax
import jax.numpy as jnp
from jax import lax
from jax.experimental import pallas as pl
from jax.experimental.pallas import tpu as pltpu

D_MODEL = 2048
BATCH = 2
SEQ = 4096
DEPTH = 4
HEAD_DIM = 128
GRID_W = 64
GRID_ROWS = SEQ // GRID_W
EPS = 1e-6
A_HEADS = 8
NA_ROWS = 8
NA_COLS = 16
B_Q_HEADS = 8
B_KV_HEADS = 2
AXIAL_THETA = 10000.0
C_Q_HEADS = 12
C_KV_HEADS = 4
C_WINDOW = 128
D_DILATIONS = ((128, 1), (512, 4), (2048, 16))
D_GROUPS = len(D_DILATIONS)
D_SLOTS = 4
D_Q_HEADS = D_GROUPS * D_SLOTS
ROPE_THETA = 500000.0
ROPE_DIMS = HEAD_DIM // 4
FFN_HIDDEN = ((-(-8 * D_MODEL // 3)) + 255) // 256 * 256
M_ROWS = BATCH * SEQ
LOG2E = math.log2(math.e)
SCALE = HEAD_DIM ** -0.5 * LOG2E
NEG = -1e30

VMEM_LIMIT_BYTES = 52 * 1024 * 1024
MXU_N = 256

F32 = jnp.float32
BF16 = jnp.bfloat16
NT_DIMS = (((1,), (1,)), ((), ()))


def _params(**kw):
    return pltpu.CompilerParams(vmem_limit_bytes=VMEM_LIMIT_BYTES, **kw)


LANES = 128


def _lane_partial_sums(sq):
    out = sq[:, :LANES]
    for i in range(1, sq.shape[1] // LANES):
        out = out + sq[:, i * LANES:(i + 1) * LANES]
    return out


ROW_SCALE_CHUNK = 256


def _store_row_scales(ssq_ref, inv_ref, row0, n_rows, dim):
    for c in range(n_rows // ROW_SCALE_CHUNK):
        rows = slice(c * ROW_SCALE_CHUNK, (c + 1) * ROW_SCALE_CHUNK)
        part = ssq_ref[0, rows, :]
        for t in range(1, ssq_ref.shape[0]):
            part = part + ssq_ref[t, rows, :]
        inv = lax.rsqrt(jnp.sum(part, axis=-1, keepdims=True) * (1.0 / dim) + EPS)
        inv_ref[pl.ds(row0 + c * ROW_SCALE_CHUNK, ROW_SCALE_CHUNK), :] = jnp.broadcast_to(
            inv, (ROW_SCALE_CHUNK, LANES))


def _ssq_spec(ssq, tm):
    return pl.BlockSpec((ssq.shape[0], tm, LANES), lambda j, i: (0, jnp.where(j == 0, i, 0), 0))


def _prenorm_body(x_ref, g_ref, xg_ref, ssq_ref):
    x = x_ref[...]
    xg_ref[...] = (x * g_ref[...]).astype(xg_ref.dtype)
    ssq_ref[...] = _lane_partial_sums(x * x)


def prenorm(x, g, tm=512):
    m, d = x.shape
    return pl.pallas_call(
        _prenorm_body,
        grid=(m // tm,),
        in_specs=[pl.BlockSpec((tm, d), lambda i: (i, 0)), pl.BlockSpec((1, d), lambda i: (0, 0))],
        out_specs=[pl.BlockSpec((tm, d), lambda i: (i, 0)), pl.BlockSpec((None, tm, LANES), lambda i: (0, i, 0))],
        out_shape=[jax.ShapeDtypeStruct((m, d), BF16), jax.ShapeDtypeStruct((1, m, LANES), F32)],
        compiler_params=_params(),
        name="prenorm",
    )(x, g.reshape(1, d).astype(F32))


HALF_LANES = HEAD_DIM // 2
AXIAL_MOVES = ((96, 32, 64), (32, 64, 96))
PARTIAL_MOVES = ((112, 16, 64), (48, 64, 80))


def _moves_to_perm(moves):
    perm = np.arange(HEAD_DIM)
    for shift, lo, hi in moves:
        perm[lo:hi] = (np.arange(lo, hi) - shift) % HEAD_DIM
    assert sorted(perm.tolist()) == list(range(HEAD_DIM))
    return perm


def _permute_lanes(w, moves):
    lane = lax.broadcasted_iota(jnp.int32, w.shape, 1)
    out = w
    for shift, lo, hi in moves:
        out = jnp.where((lane >= lo) & (lane < hi), pltpu.roll(w, shift, 1), out)
    return out


PROJ_TM = 2048
PROJ_TN = 512
PROJ_SUB = 256


def _in_proj_body(sig_ref, x_ref, ssq_ref, w_ref, g_ref, cos_ref, sin_ref, o_ref, wbf_ref, inv_ref, *,
                  sigs, moves):
    n = pl.program_id(0)
    m = pl.program_id(1)
    heads = PROJ_TN // HEAD_DIM
    row0 = pl.multiple_of(m * PROJ_TM, PROJ_TM)

    @pl.when(n == 0)
    def _():
        _store_row_scales(ssq_ref, inv_ref, row0, PROJ_TM, x_ref.shape[1])

    def cast_weights(kinds):
        for h in range(heads):
            cols = slice(h * HEAD_DIM, (h + 1) * HEAD_DIM)
            w = w_ref[:, cols]
            if kinds[h] == "rope":
                w = _permute_lanes(w, moves)
            wbf_ref[:, cols] = w.astype(BF16)

    def compute(kinds):
        for r in range(PROJ_TM // PROJ_SUB):
            rows = slice(r * PROJ_SUB, (r + 1) * PROJ_SUB)
            x = x_ref[rows, :]
            inv = inv_ref[pl.ds(row0 + r * PROJ_SUB, PROJ_SUB), :]
            for p in range(PROJ_TN // MXU_N):
                acc = jnp.dot(x, wbf_ref[:, p * MXU_N:(p + 1) * MXU_N], preferred_element_type=F32)
                for hh in range(MXU_N // HEAD_DIM):
                    h = p * (MXU_N // HEAD_DIM) + hh
                    cols = slice(h * HEAD_DIM, (h + 1) * HEAD_DIM)
                    y = acc[:, hh * HEAD_DIM:(hh + 1) * HEAD_DIM]
                    if kinds[h] == "plain":
                        y = y * inv
                    else:
                        ms = jnp.mean(y * y, axis=-1, keepdims=True)
                        y = y * (inv * lax.rsqrt(ms * (inv * inv) + EPS)) * g_ref[:, cols]
                    if kinds[h] == "rope":
                        y = y * cos_ref[rows, :] + pltpu.roll(y, HALF_LANES, 1) * sin_ref[rows, :]
                    o_ref[rows, cols] = y.astype(o_ref.dtype)

    for sid, kinds in enumerate(sigs):
        @pl.when(sig_ref[n] == sid)
        def _(kinds=kinds):
            @pl.when(m == 0)
            def _():
                cast_weights(kinds)

            compute(kinds)


def in_proj(xg, ssq, w, li, kinds, gains, rope):
    m, k = xg.shape
    n = w.shape[2]
    heads = PROJ_TN // HEAD_DIM
    tiles = [tuple(kinds[i:i + heads]) for i in range(0, len(kinds), heads)]
    sigs = tuple(dict.fromkeys(tiles))
    sig_ids = jnp.asarray([sigs.index(t) for t in tiles], jnp.int32)
    moves, cos, sin = rope
    pos_blocks = SEQ // PROJ_TM
    return pl.pallas_call(
        functools.partial(_in_proj_body, sigs=sigs, moves=moves),
        grid=(n // PROJ_TN, m // PROJ_TM),
        in_specs=[pl.BlockSpec(memory_space=pltpu.SMEM),
                  pl.BlockSpec((PROJ_TM, k), lambda j, i: (i, 0)),
                  _ssq_spec(ssq, PROJ_TM),
                  pl.BlockSpec((None, k, PROJ_TN), lambda j, i: (li, 0, j)),
                  pl.BlockSpec((1, PROJ_TN), lambda j, i: (0, j)),
                  pl.BlockSpec((PROJ_TM, HEAD_DIM), lambda j, i: (i % pos_blocks, 0)),
                  pl.BlockSpec((PROJ_TM, HEAD_DIM), lambda j, i: (i % pos_blocks, 0))],
        out_specs=pl.BlockSpec((PROJ_TM, PROJ_TN), lambda j, i: (i, j)),
        out_shape=jax.ShapeDtypeStruct((m, n), BF16),
        scratch_shapes=[pltpu.VMEM((k, PROJ_TN), BF16), pltpu.VMEM((m, LANES), F32)],
        compiler_params=_params(),
        name="in_proj",
    )(sig_ids, xg, ssq, w, gains, cos, sin)


RES_TM = 512
RES_SUB = 256
RES_VMEM_WEIGHT_BYTES = 12 * 1024 * 1024


def _res_tn(kt, n):
    tn = n
    while kt * tn * 4 > RES_VMEM_WEIGHT_BYTES and tn % (2 * MXU_N) == 0:
        tn //= 2
    return tn


def _mm_res_body(*refs, n_in, tn, emit_norm):
    a_refs = refs[:n_in]
    if emit_norm:
        w_ref, x_ref, g_ref, o_ref, xg_ref, ssq_ref, wbf_ref = refs[n_in:]
    else:
        w_ref, x_ref, o_ref, wbf_ref = refs[n_in:]

    @pl.when(pl.program_id(1) == 0)
    def _():
        wbf_ref[...] = w_ref[...].astype(BF16)

    for r in range(RES_TM // RES_SUB):
        rows = slice(r * RES_SUB, (r + 1) * RES_SUB)
        a = a_refs[0][rows, :] if n_in == 1 else jnp.concatenate([a_ref[rows, :] for a_ref in a_refs], axis=1)
        part = None
        for p in range(tn // MXU_N):
            cols = slice(p * MXU_N, (p + 1) * MXU_N)
            acc = x_ref[rows, cols] + jnp.dot(a, wbf_ref[:, cols], preferred_element_type=F32)
            o_ref[rows, cols] = acc
            if emit_norm:
                xg_ref[rows, cols] = (acc * g_ref[:, cols]).astype(xg_ref.dtype)
                sq = _lane_partial_sums(acc * acc)
                part = sq if part is None else part + sq
        if emit_norm:
            ssq_ref[rows, :] = part


def matmul_residual(a_list, w, li, x, g_next=None):
    m, n = x.shape
    widths = tuple(a.shape[1] for a in a_list)
    kt = sum(widths)
    assert kt == w.shape[1]
    tn = _res_tn(kt, n)
    emit_norm = g_next is not None
    tile = pl.BlockSpec((RES_TM, tn), lambda j, i: (i, j))
    in_specs = [pl.BlockSpec((RES_TM, ka), lambda j, i: (i, 0)) for ka in widths]
    in_specs += [pl.BlockSpec((None, kt, tn), lambda j, i: (li, 0, j)), tile]
    args = [*a_list, w, x]
    out_specs, out_shape = tile, jax.ShapeDtypeStruct((m, n), F32)
    if emit_norm:
        in_specs.append(pl.BlockSpec((1, tn), lambda j, i: (0, j)))
        args.append(g_next.reshape(1, n).astype(F32))
        out_specs = [tile, tile, pl.BlockSpec((None, RES_TM, LANES), lambda j, i: (j, i, 0))]
        out_shape = [out_shape, jax.ShapeDtypeStruct((m, n), BF16), jax.ShapeDtypeStruct((n // tn, m, LANES), F32)]
    return pl.pallas_call(
        functools.partial(_mm_res_body, n_in=len(a_list), tn=tn, emit_norm=emit_norm),
        grid=(n // tn, m // RES_TM),
        in_specs=in_specs,
        out_specs=out_specs,
        out_shape=out_shape,
        scratch_shapes=[pltpu.VMEM((kt, tn), BF16)],
        compiler_params=_params(),
        name="matmul_residual",
    )(*args)


FFN_TM = 1024
FFN_TN = 512
FFN_SUB = 256


def _gate_up_body(x_ref, ssq_ref, wg_ref, wu_ref, o_ref, wgb_ref, wub_ref, inv_ref):
    row0 = pl.multiple_of(pl.program_id(1) * FFN_TM, FFN_TM)

    @pl.when(pl.program_id(1) == 0)
    def _():
        wgb_ref[...] = wg_ref[...].astype(BF16)
        wub_ref[...] = wu_ref[...].astype(BF16)

    @pl.when(pl.program_id(0) == 0)
    def _():
        _store_row_scales(ssq_ref, inv_ref, row0, FFN_TM, x_ref.shape[1])

    for r in range(FFN_TM // FFN_SUB):
        rows = slice(r * FFN_SUB, (r + 1) * FFN_SUB)
        x = x_ref[rows, :]
        inv = jnp.concatenate([inv_ref[pl.ds(row0 + r * FFN_SUB, FFN_SUB), :]] * (MXU_N // LANES), axis=1)
        for p in range(FFN_TN // MXU_N):
            cols = slice(p * MXU_N, (p + 1) * MXU_N)
            g = jnp.dot(x, wgb_ref[:, cols], preferred_element_type=F32) * inv
            u = jnp.dot(x, wub_ref[:, cols], preferred_element_type=F32) * inv
            o_ref[rows, cols] = (g * (1.0 / (1.0 + jnp.exp(-g))) * u).astype(o_ref.dtype)


def gate_up(xg, ssq, wg, wu, li):
    m, k = xg.shape
    n = wg.shape[2]
    return pl.pallas_call(
        _gate_up_body,
        grid=(n // FFN_TN, m // FFN_TM),
        in_specs=[pl.BlockSpec((FFN_TM, k), lambda j, i: (i, 0)),
                  _ssq_spec(ssq, FFN_TM),
                  pl.BlockSpec((None, k, FFN_TN), lambda j, i: (li, 0, j)),
                  pl.BlockSpec((None, k, FFN_TN), lambda j, i: (li, 0, j))],
        out_specs=pl.BlockSpec((FFN_TM, FFN_TN), lambda j, i: (i, j)),
        out_shape=jax.ShapeDtypeStruct((m, n), BF16),
        scratch_shapes=[pltpu.VMEM((k, FFN_TN), BF16), pltpu.VMEM((k, FFN_TN), BF16),
                        pltpu.VMEM((m, LANES), F32)],
        compiler_params=_params(),
        name="gate_up",
    )(xg, ssq, wg, wu)


AB_QA, AB_KA, AB_VA = 0, A_HEADS, 2 * A_HEADS
AB_QB = 3 * A_HEADS
AB_KB = AB_QB + B_Q_HEADS
AB_VB = AB_KB + B_KV_HEADS
AB_KINDS = ("norm",) * (2 * A_HEADS) + ("plain",) * A_HEADS + ("rope",) * (B_Q_HEADS + B_KV_HEADS) \
    + ("plain",) * B_KV_HEADS
CD_QC = 0
CD_KC = C_Q_HEADS
CD_VC = CD_KC + C_KV_HEADS
CD_QD = CD_VC + C_KV_HEADS
CD_KD = CD_QD + D_Q_HEADS
CD_VD = CD_KD + D_SLOTS
CD_KINDS = ("rope",) * (C_Q_HEADS + C_KV_HEADS) + ("plain",) * C_KV_HEADS \
    + ("rope",) * (D_Q_HEADS + D_SLOTS) + ("plain",) * D_SLOTS


NA_QROWS = 2
NA_WIN_ROWS = 10
NA_Q = NA_QROWS * GRID_W
NA_KEYS = NA_WIN_ROWS * GRID_W
NA_BLOCKS = GRID_ROWS // NA_QROWS
NA_VARIANT_BLOCKS = (0, 1, 2, NA_BLOCKS - 2, NA_BLOCKS - 1)


def _na_win_start(rb):
    return np.clip(NA_QROWS * rb - NA_ROWS // 2, 0, GRID_ROWS - NA_WIN_ROWS)


N_DR = 2 * NA_ROWS - 1


def _na_row_slabs():
    dr_idx = np.full((len(NA_VARIANT_BLOCKS), NA_QROWS, NA_WIN_ROWS), N_DR, np.int32)
    for v, rb in enumerate(NA_VARIANT_BLOCKS):
        ws = _na_win_start(rb)
        for qr in range(NA_QROWS):
            qrow = NA_QROWS * rb + qr
            r0 = np.clip(qrow - NA_ROWS // 2, 0, GRID_ROWS - NA_ROWS)
            for kr in range(NA_WIN_ROWS):
                krow = ws + kr
                if r0 <= krow < r0 + NA_ROWS:
                    dr_idx[v, qr, kr] = krow - qrow + NA_ROWS - 1
    return dr_idx


def _na_column_table(rpb):
    assert rpb.shape == (A_HEADS, N_DR, 2 * NA_COLS - 1)
    lanes = 2 * GRID_W
    ext = jnp.concatenate([rpb.astype(F32) * LOG2E,
                           jnp.full((A_HEADS, N_DR, lanes - rpb.shape[2]), NEG, F32)], axis=-1)
    ext = jnp.roll(ext, -(NA_COLS - 1), axis=-1)
    toep = jnp.tile(ext, (1, 1, GRID_W))[:, :, :GRID_W * (lanes - 1)].reshape(A_HEADS, N_DR, GRID_W, lanes - 1)
    toep = toep[..., :GRID_W]
    cols = np.arange(GRID_W)
    c0 = np.clip(cols - NA_COLS // 2, 0, GRID_W - NA_COLS)
    col_ok = (cols[None, :] >= c0[:, None]) & (cols[None, :] < c0[:, None] + NA_COLS)
    toep = jnp.where(col_ok[None, None], toep, NEG)
    toep = jnp.concatenate([toep, jnp.full((A_HEADS, 1, GRID_W, GRID_W), NEG, F32)], axis=1)
    return jnp.concatenate([toep, toep], axis=-1)


NA_UNROLL = 8


def _na_body(q_ref, k_ref, v_ref, c_ref, o_ref, t_ref):
    left_half = lax.broadcasted_iota(jnp.int32, (GRID_W, 2 * GRID_W), 1) < GRID_W
    for v, per_q in enumerate(_na_row_slabs()):
        for qr, slabs in enumerate(per_q):
            for pair in range(NA_WIN_ROWS // 2):
                tile = jnp.where(left_half, c_ref[int(slabs[2 * pair])], c_ref[int(slabs[2 * pair + 1])])
                t_ref[v, qr * GRID_W:(qr + 1) * GRID_W, pair * 2 * GRID_W:(pair + 1) * 2 * GRID_W] = tile

    def block(rb):
        ws = jnp.clip(NA_QROWS * rb - NA_ROWS // 2, 0, GRID_ROWS - NA_WIN_ROWS)
        koff = pl.multiple_of(ws * GRID_W, GRID_W)
        qoff = pl.multiple_of(rb * NA_Q, NA_Q)
        variant = jnp.where(rb < 2, rb, jnp.where(rb >= NA_BLOCKS - 2, rb - (NA_BLOCKS - 5), 2))
        q = q_ref[pl.ds(qoff, NA_Q), :]
        k = k_ref[pl.ds(koff, NA_KEYS), :]
        v = v_ref[pl.ds(koff, NA_KEYS), :]
        s = lax.dot_general(q, k, NT_DIMS, preferred_element_type=F32) + t_ref[variant]
        m = jnp.max(s, axis=-1, keepdims=True)
        p = jnp.exp2(s - m)
        l = jnp.sum(p, axis=-1, keepdims=True)
        o = jnp.dot(p.astype(BF16), v, preferred_element_type=F32)
        o_ref[pl.ds(qoff, NA_Q), :] = (o / l).astype(o_ref.dtype)

    def step(it, carry):
        for u in range(NA_UNROLL):
            block(it * NA_UNROLL + u)
        return carry

    lax.fori_loop(0, NA_BLOCKS // NA_UNROLL, step, 0)


def neighbourhood_attention(h, rpb):
    nv = len(NA_VARIANT_BLOCKS)
    blk = (SEQ, HEAD_DIM)
    return pl.pallas_call(
        _na_body,
        grid=(BATCH, A_HEADS),
        in_specs=[pl.BlockSpec(blk, lambda b, hd: (b, AB_QA + hd)),
                  pl.BlockSpec(blk, lambda b, hd: (b, AB_KA + hd)),
                  pl.BlockSpec(blk, lambda b, hd: (b, AB_VA + hd)),
                  pl.BlockSpec((None, N_DR + 1, GRID_W, 2 * GRID_W), lambda b, hd: (hd, 0, 0, 0))],
        out_specs=pl.BlockSpec(blk, lambda b, hd: (b, hd)),
        out_shape=jax.ShapeDtypeStruct((M_ROWS, A_HEADS * HEAD_DIM), BF16),
        scratch_shapes=[pltpu.VMEM((nv, NA_Q, NA_KEYS), F32)],
        compiler_params=_params(),
        name="neighbourhood_attention",
    )(h, h, h, _na_column_table(rpb))


B_GROUP = B_Q_HEADS // B_KV_HEADS


DENSE_TQ = 256
DENSE_TK = 256


def _dense_body(q_ref, k_ref, v_ref, o_ref):
    tq = DENSE_TQ
    q = jnp.concatenate([q_ref[:, g * HEAD_DIM:(g + 1) * HEAD_DIM] for g in range(B_GROUP)], axis=0)
    ones = jnp.ones((DENSE_TK, HEAD_DIM), BF16)
    m = acc = None
    for c in range(SEQ // DENSE_TK):
        keys = slice(c * DENSE_TK, (c + 1) * DENSE_TK)
        s = lax.dot_general(q, k_ref[keys, :], NT_DIMS, preferred_element_type=F32).astype(BF16)
        v_ones = jnp.concatenate([v_ref[keys, :], ones], axis=1)
        row_max = jnp.max(s, axis=-1, keepdims=True)
        if c == 0:
            m = row_max
            acc = jnp.dot(jnp.exp2(s - m), v_ones, preferred_element_type=F32)
        else:
            m_new = jnp.maximum(m, row_max)
            alpha = jnp.exp2(m.astype(F32) - m_new.astype(F32))
            acc = alpha * acc + jnp.dot(jnp.exp2(s - m_new), v_ones, preferred_element_type=F32)
            m = m_new
    o = acc[:, :HEAD_DIM] / acc[:, HEAD_DIM:]
    for g in range(B_GROUP):
        o_ref[:, g * HEAD_DIM:(g + 1) * HEAD_DIM] = o[g * tq:(g + 1) * tq].astype(o_ref.dtype)


def dense_gqa(h):
    tq = DENSE_TQ
    nq = SEQ // tq
    gw = B_GROUP * HEAD_DIM
    assert AB_QB % B_GROUP == 0
    q_blk0 = AB_QB // B_GROUP
    blk = (SEQ, HEAD_DIM)
    return pl.pallas_call(
        _dense_body,
        grid=(BATCH, B_KV_HEADS, nq),
        in_specs=[pl.BlockSpec((tq, gw), lambda b, kv, i: (b * nq + i, q_blk0 + kv)),
                  pl.BlockSpec(blk, lambda b, kv, i: (b, AB_KB + kv)),
                  pl.BlockSpec(blk, lambda b, kv, i: (b, AB_VB + kv))],
        out_specs=pl.BlockSpec((tq, gw), lambda b, kv, i: (b * nq + i, kv)),
        out_shape=jax.ShapeDtypeStruct((M_ROWS, B_Q_HEADS * HEAD_DIM), BF16),
        compiler_params=_params(),
        name="dense_gqa",
    )(h, h, h)


QBLK = 128
N_QBLK = SEQ // QBLK


def _band_mask_table(span):
    assert span <= QBLK
    col = np.arange(QBLK + 2 * span)[None, :]
    row = np.arange(QBLK)[:, None]
    t = np.stack([np.where(np.abs(col - row + shift) <= span, 0.0, NEG) for shift in (0, -span, -2 * span)])
    return jnp.asarray(t, F32)


def _band_window(ub, blocks, span):
    start = jnp.clip(ub * QBLK - span, 0, blocks * QBLK - (QBLK + 2 * span))
    variant = jnp.where(ub == 0, 0, jnp.where(ub == blocks - 1, 2, 1))
    return start, variant


C_GROUP = C_Q_HEADS // C_KV_HEADS


C_WIN = QBLK + 2 * C_WINDOW
C_UNROLL = 4


def _window_sink_body(sink_ref, q_ref, k_ref, v_ref, mask_ref, o_ref):
    kv = pl.program_id(1)

    def block(i):
        qoff = pl.multiple_of(i * QBLK, QBLK)
        start, variant = _band_window(i, N_QBLK, C_WINDOW)
        koff = pl.multiple_of(start, QBLK)
        mask = mask_ref[variant]
        q = q_ref[pl.ds(qoff, QBLK), :]
        qs = jnp.concatenate([q[:, g * HEAD_DIM:(g + 1) * HEAD_DIM] for g in range(C_GROUP)], axis=0)
        s = lax.dot_general(qs, k_ref[pl.ds(koff, C_WIN), :], NT_DIMS, preferred_element_type=F32)
        ps, ls = [], []
        for g in range(C_GROUP):
            sg = s[g * QBLK:(g + 1) * QBLK] + mask
            sink = sink_ref[kv * C_GROUP + g] * LOG2E
            m = jnp.maximum(jnp.max(sg, axis=-1, keepdims=True), sink)
            p = jnp.exp2(sg - m)
            ls.append(jnp.sum(p, axis=-1, keepdims=True) + jnp.exp2(sink - m))
            ps.append(p.astype(BF16))
        o = jnp.dot(jnp.concatenate(ps, axis=0), v_ref[pl.ds(koff, C_WIN), :], preferred_element_type=F32)
        for g in range(C_GROUP):
            o_ref[pl.ds(qoff, QBLK), g * HEAD_DIM:(g + 1) * HEAD_DIM] = (
                o[g * QBLK:(g + 1) * QBLK] / ls[g]).astype(o_ref.dtype)

    def step(it, carry):
        for u in range(C_UNROLL):
            block(it * C_UNROLL + u)
        return carry

    lax.fori_loop(0, N_QBLK // C_UNROLL, step, 0)


def window_gqa_sink(h, sink):
    gw = C_GROUP * HEAD_DIM
    assert CD_QC % C_GROUP == 0
    q_blk0 = CD_QC // C_GROUP
    blk = (SEQ, HEAD_DIM)
    return pl.pallas_call(
        _window_sink_body,
        grid=(BATCH, C_KV_HEADS),
        in_specs=[pl.BlockSpec(memory_space=pltpu.SMEM),
                  pl.BlockSpec((SEQ, gw), lambda b, kv: (b, q_blk0 + kv)),
                  pl.BlockSpec(blk, lambda b, kv: (b, CD_KC + kv)),
                  pl.BlockSpec(blk, lambda b, kv: (b, CD_VC + kv)),
                  pl.BlockSpec((3, QBLK, C_WIN), lambda b, kv: (0, 0, 0))],
        out_specs=pl.BlockSpec((SEQ, gw), lambda b, kv: (b, kv)),
        out_shape=jax.ShapeDtypeStruct((M_ROWS, C_Q_HEADS * HEAD_DIM), BF16),
        compiler_params=_params(),
        name="window_gqa_sink",
    )(sink.astype(F32), h, h, h, _band_mask_table(C_WINDOW))


D_DILS = tuple(d for _, d in D_DILATIONS)
D_SPAN = D_DILATIONS[0][0] // 2
assert all((w // 2) // d == D_SPAN and N_QBLK % d == 0 for w, d in D_DILATIONS) and D_DILS[0] == 1
D_WIN = QBLK + 2 * D_SPAN
D_UNROLL = 4


def _dilated_body(q0_ref, q1_ref, q2_ref, k_ref, v_ref, mask_ref, o_ref,
                  stage_ref, qc_ref, kc_ref, vc_ref, og_ref, lse_ref):
    def to_class_major(dst_ref, slot, d):
        run = SEQ // d
        for rho in range(d):
            dst_ref[slot, pl.ds(rho * run, run), :] = stage_ref[pl.ds(rho, run, stride=d), :].astype(BF16)

    for src_ref, dst_ref in ((k_ref, kc_ref), (v_ref, vc_ref)):
        stage_ref[...] = src_ref[...].astype(F32)
        for g in range(1, D_GROUPS):
            to_class_major(dst_ref, g - 1, D_DILS[g])
    for g, src_ref in ((1, q1_ref), (2, q2_ref)):
        stage_ref[...] = src_ref[...].astype(F32)
        to_class_major(qc_ref, g - 1, D_DILS[g])

    def block(g, i):
        d = D_DILS[g]
        run_blocks = N_QBLK // d
        rho = i // run_blocks
        ub = i % run_blocks
        qoff = pl.multiple_of(i * QBLK, QBLK)
        start, variant = _band_window(ub, run_blocks, D_SPAN)
        koff = pl.multiple_of(rho * (run_blocks * QBLK) + start, D_SPAN)
        if g == 0:
            q, k, v = q0_ref[pl.ds(qoff, QBLK), :], k_ref[pl.ds(koff, D_WIN), :], v_ref[pl.ds(koff, D_WIN), :]
        else:
            q = qc_ref[g - 1, pl.ds(qoff, QBLK), :]
            k = kc_ref[g - 1, pl.ds(koff, D_WIN), :]
            v = vc_ref[g - 1, pl.ds(koff, D_WIN), :]
        s = lax.dot_general(q, k, NT_DIMS, preferred_element_type=F32) + mask_ref[variant]
        m = jnp.max(s, axis=-1, keepdims=True)
        p = jnp.exp2(s - m)
        l = jnp.sum(p, axis=-1, keepdims=True)
        o = jnp.dot(p.astype(BF16), v, preferred_element_type=F32) / l
        lse = jnp.broadcast_to(m + jnp.log2(l), (QBLK, HEAD_DIM))
        if d == 1:
            rows = pl.ds(qoff, QBLK)
        else:
            rows = pl.ds(rho + d * ub * QBLK, QBLK, stride=d)
        og_ref[g, rows, :] = o
        lse_ref[g, rows, :] = lse

    def step(it, carry):
        for u in range(D_UNROLL):
            for g in range(D_GROUPS):
                block(g, it * D_UNROLL + u)
        return carry

    lax.fori_loop(0, N_QBLK // D_UNROLL, step, 0)

    def merge(c, carry):
        rows = pl.ds(pl.multiple_of(c * QBLK, QBLK), QBLK)
        lses = [lse_ref[g, rows, :] for g in range(D_GROUPS)]
        top = functools.reduce(jnp.maximum, lses)
        ws = [jnp.exp2(x - top) for x in lses]
        num = sum(w * og_ref[g, rows, :] for g, w in enumerate(ws))
        o_ref[rows, :] = (num / sum(ws)).astype(o_ref.dtype)
        return carry

    lax.fori_loop(0, N_QBLK, merge, 0)


def dilated_attention(h):
    blk = (SEQ, HEAD_DIM)
    return pl.pallas_call(
        _dilated_body,
        grid=(BATCH, D_SLOTS),
        in_specs=[pl.BlockSpec(blk, lambda b, s: (b, CD_QD + s)),
                  pl.BlockSpec(blk, lambda b, s: (b, CD_QD + D_SLOTS + s)),
                  pl.BlockSpec(blk, lambda b, s: (b, CD_QD + 2 * D_SLOTS + s)),
                  pl.BlockSpec(blk, lambda b, s: (b, CD_KD + s)),
                  pl.BlockSpec(blk, lambda b, s: (b, CD_VD + s)),
                  pl.BlockSpec((3, QBLK, D_WIN), lambda b, s: (0, 0, 0))],
        out_specs=pl.BlockSpec(blk, lambda b, s: (b, s)),
        out_shape=jax.ShapeDtypeStruct((M_ROWS, D_SLOTS * HEAD_DIM), BF16),
        scratch_shapes=[pltpu.VMEM((SEQ, HEAD_DIM), F32),
                        pltpu.VMEM((D_GROUPS - 1, SEQ, HEAD_DIM), BF16),
                        pltpu.VMEM((D_GROUPS - 1, SEQ, HEAD_DIM), BF16),
                        pltpu.VMEM((D_GROUPS - 1, SEQ, HEAD_DIM), BF16),
                        pltpu.VMEM((D_GROUPS, SEQ, HEAD_DIM), F32),
                        pltpu.VMEM((D_GROUPS, SEQ, HEAD_DIM), F32)],
        compiler_params=_params(),
        name="dilated_attention",
    )(h, h, h, h, h, _band_mask_table(D_SPAN))


def _rope_tables(kind):
    pos = np.arange(SEQ, dtype=np.float64)
    cos = np.ones((SEQ, HEAD_DIM), np.float64)
    sin = np.zeros((SEQ, HEAD_DIM), np.float64)

    def fill(start, r, p, theta):
        half = r // 2
        inv = np.exp(-math.log(theta) * np.arange(half, dtype=np.float64) * (2.0 / r))
        ang = p[:, None] * inv[None, :]
        cos[:, start:start + half] = np.cos(ang)
        cos[:, start + half:start + r] = np.cos(ang)
        sin[:, start:start + half] = -np.sin(ang)
        sin[:, start + half:start + r] = np.sin(ang)

    if kind == "axial":
        hw = HEAD_DIM // 2
        fill(0, hw, np.floor(pos / GRID_W), AXIAL_THETA)
        fill(hw, hw, pos % GRID_W, AXIAL_THETA)
        moves = AXIAL_MOVES
    else:
        fill(0, ROPE_DIMS, pos, ROPE_THETA)
        moves = PARTIAL_MOVES
    perm = _moves_to_perm(moves)
    return moves, jnp.asarray(cos[:, perm], dtype=F32), jnp.asarray(sin[:, perm], dtype=F32)


def _gain_row(parts):
    rows = []
    for g, heads, scale, perm in parts:
        if g is None:
            g = jnp.ones((HEAD_DIM,), F32)
        g = g.astype(F32) * scale
        if perm is not None:
            g = g[perm]
        rows.append(jnp.tile(g, heads))
    return jnp.concatenate(rows).reshape(1, -1)


def _layer_ab(x, xg, ssq, w_in, w_out, li, g_next, a_qn, a_kn, a_rpb, b_qn, b_kn, axial):
    perm = _moves_to_perm(axial[0])
    gains = _gain_row([(a_qn, A_HEADS, SCALE, None), (a_kn, A_HEADS, 1.0, None), (None, A_HEADS, 1.0, None),
                       (b_qn, B_Q_HEADS, SCALE, perm), (b_kn, B_KV_HEADS, 1.0, perm),
                       (None, B_KV_HEADS, 1.0, None)])
    h = in_proj(xg, ssq, w_in, li, AB_KINDS, gains, axial)
    oa = neighbourhood_attention(h, a_rpb)
    ob = dense_gqa(h)
    return matmul_residual([oa, ob], w_out, li, x, g_next)


def _layer_cd(x, xg, ssq, w_in, w_out, li, g_next, c_qn, c_kn, c_sink, d_qn, d_kn, partial):
    perm = _moves_to_perm(partial[0])
    gains = _gain_row([(c_qn, C_Q_HEADS, SCALE, perm), (c_kn, C_KV_HEADS, 1.0, perm), (None, C_KV_HEADS, 1.0, None),
                       (d_qn, D_Q_HEADS, SCALE, perm), (d_kn, D_SLOTS, 1.0, perm), (None, D_SLOTS, 1.0, None)])
    h = in_proj(xg, ssq, w_in, li, CD_KINDS, gains, partial)
    oc = window_gqa_sink(h, c_sink)
    od = dilated_attention(h)
    return matmul_residual([oc, od], w_out, li, x, g_next)


def kernel(x, attn_norm, ffn_norm, ab_w_in, ab_w_out, a_q_norm, a_k_norm, a_rpb, b_q_norm, b_k_norm,
           cd_w_in, cd_w_out, c_q_norm, c_k_norm, c_sink, d_q_norm, d_k_norm, w_gate, w_up, w_down):
    axial = _rope_tables("axial")
    partial = _rope_tables("partial")
    x = x.reshape(M_ROWS, D_MODEL)
    xg, ssq = prenorm(x, attn_norm[0])
    for layer in range(DEPTH):
        j = layer // 2
        if layer % 2 == 0:
            x, xg, ssq = _layer_ab(x, xg, ssq, ab_w_in, ab_w_out, j, ffn_norm[layer], a_q_norm[j], a_k_norm[j],
                                   a_rpb[j], b_q_norm[j], b_k_norm[j], axial)
        else:
            x, xg, ssq = _layer_cd(x, xg, ssq, cd_w_in, cd_w_out, j, ffn_norm[layer], c_q_norm[j], c_k_norm[j],
                                   c_sink[j], d_q_norm[j], d_k_norm[j], partial)
        hidden = gate_up(xg, ssq, w_gate, w_up, layer)
        if layer + 1 < DEPTH:
            x, xg, ssq = matmul_residual([hidden], w_down, layer, x, attn_norm[layer + 1])
        else:
            x = matmul_residual([hidden], w_down, layer, x)
    return x.reshape(BATCH, SEQ, D_MODEL)
```

```python
import functools
import math

import numpy as np
import jax
import jax.numpy as jnp
from jax import lax
from jax.experimental import pallas as pl
from jax.experimental.pallas import tpu as pltpu

D_MODEL = 2048
BATCH = 2
SEQ = 4096
DEPTH = 4
HEAD_DIM = 128
GRID_W = 64
GRID_ROWS = SEQ // GRID_W
EPS = 1e-6
A_HEADS = 8
NA_ROWS = 8
NA_COLS = 16
B_Q_HEADS = 8
B_KV_HEADS = 2
AXIAL_THETA = 10000.0
C_Q_HEADS = 12
C_KV_HEADS = 4
C_WINDOW = 128
D_DILATIONS = ((128, 1), (512, 4), (2048, 16))
D_GROUPS = len(D_DILATIONS)
D_SLOTS = 4
D_Q_HEADS = D_GROUPS * D_SLOTS
ROPE_THETA = 500000.0
ROPE_DIMS = HEAD_DIM // 4
FFN_HIDDEN = ((-(-8 * D_MODEL // 3)) + 255) // 256 * 256
M_ROWS = BATCH * SEQ
LOG2E = math.log2(math.e)
SCALE = HEAD_DIM ** -0.5 * LOG2E
NEG = -1e30

VMEM_LIMIT_BYTES = 52 * 1024 * 1024
MXU_N = 256

F32 = jnp.float32
BF16 = jnp.bfloat16
NT_DIMS = (((1,), (1,)), ((), ()))


def _params(**kw):
    return pltpu.CompilerParams(vmem_limit_bytes=VMEM_LIMIT_BYTES, **kw)


LANES = 128


def _lane_partial_sums(sq):
    out = sq[:, :LANES]
    for i in range(1, sq.shape[1] // LANES):
        out = out + sq[:, i * LANES:(i + 1) * LANES]
    return out


ROW_SCALE_CHUNK = 256


def _store_row_scales(ssq_ref, inv_ref, row0, n_rows, dim):
    for c in range(n_rows // ROW_SCALE_CHUNK):
        rows = slice(c * ROW_SCALE_CHUNK, (c + 1) * ROW_SCALE_CHUNK)
        part = ssq_ref[0, rows, :]
        for t in range(1, ssq_ref.shape[0]):
            part = part + ssq_ref[t, rows, :]
        inv = lax.rsqrt(jnp.sum(part, axis=-1, keepdims=True) * (1.0 / dim) + EPS)
        inv_ref[pl.ds(row0 + c * ROW_SCALE_CHUNK, ROW_SCALE_CHUNK), :] = jnp.broadcast_to(
            inv, (ROW_SCALE_CHUNK, LANES))


def _ssq_spec(ssq, tm):
    return pl.BlockSpec((ssq.shape[0], tm, LANES), lambda j, i: (0, jnp.where(j == 0, i, 0), 0))


def _prenorm_body(x_ref, g_ref, xg_ref, ssq_ref):
    x = x_ref[...]
    xg_ref[...] = (x * g_ref[...]).astype(xg_ref.dtype)
    ssq_ref[...] = _lane_partial_sums(x * x)


def prenorm(x, g, tm=512):
    m, d = x.shape
    return pl.pallas_call(
        _prenorm_body,
        grid=(m // tm,),
        in_specs=[pl.BlockSpec((tm, d), lambda i: (i, 0)), pl.BlockSpec((1, d), lambda i: (0, 0))],
        out_specs=[pl.BlockSpec((tm, d), lambda i: (i, 0)), pl.BlockSpec((None, tm, LANES), lambda i: (0, i, 0))],
        out_shape=[jax.ShapeDtypeStruct((m, d), BF16), jax.ShapeDtypeStruct((1, m, LANES), F32)],
        compiler_params=_params(),
        name="prenorm",
    )(x, g.reshape(1, d).astype(F32))


HALF_LANES = HEAD_DIM // 2
AXIAL_MOVES = ((96, 32, 64), (32, 64, 96))
PARTIAL_MOVES = ((112, 16, 64), (48, 64, 80))


def _moves_to_perm(moves):
    perm = np.arange(HEAD_DIM)
    for shift, lo, hi in moves:
        perm[lo:hi] = (np.arange(lo, hi) - shift) % HEAD_DIM
    assert sorted(perm.tolist()) == list(range(HEAD_DIM))
    return perm


def _permute_lanes(w, moves):
    lane = lax.broadcasted_iota(jnp.int32, w.shape, 1)
    out = w
    for shift, lo, hi in moves:
        out = jnp.where((lane >= lo) & (lane < hi), pltpu.roll(w, shift, 1), out)
    return out


PROJ_TM = 2048
PROJ_TN = 512
PROJ_SUB = 256


def _in_proj_body(sig_ref, x_ref, ssq_ref, w_ref, g_ref, cos_ref, sin_ref, o_ref, wbf_ref, inv_ref, *,
                  sigs, moves):
    n = pl.program_id(0)
    m = pl.program_id(1)
    heads = PROJ_TN // HEAD_DIM
    row0 = pl.multiple_of(m * PROJ_TM, PROJ_TM)

    @pl.when(n == 0)
    def _():
        _store_row_scales(ssq_ref, inv_ref, row0, PROJ_TM, x_ref.shape[1])

    def cast_weights(kinds):
        for h in range(heads):
            cols = slice(h * HEAD_DIM, (h + 1) * HEAD_DIM)
            w = w_ref[:, cols]
            if kinds[h] == "rope":
                w = _permute_lanes(w, moves)
            wbf_ref[:, cols] = w.astype(BF16)

    def compute(kinds):
        for r in range(PROJ_TM // PROJ_SUB):
            rows = slice(r * PROJ_SUB, (r + 1) * PROJ_SUB)
            x = x_ref[rows, :]
            inv = inv_ref[pl.ds(row0 + r * PROJ_SUB, PROJ_SUB), :]
            for p in range(PROJ_TN // MXU_N):
                acc = jnp.dot(x, wbf_ref[:, p * MXU_N:(p + 1) * MXU_N], preferred_element_type=F32)
                for hh in range(MXU_N // HEAD_DIM):
                    h = p * (MXU_N // HEAD_DIM) + hh
                    cols = slice(h * HEAD_DIM, (h + 1) * HEAD_DIM)
                    y = acc[:, hh * HEAD_DIM:(hh + 1) * HEAD_DIM]
                    if kinds[h] == "plain":
                        y = y * inv
                    else:
                        ms = jnp.mean(y * y, axis=-1, keepdims=True)
                        y = y * (inv * lax.rsqrt(ms * (inv * inv) + EPS)) * g_ref[:, cols]
                    if kinds[h] == "rope":
                        y = y * cos_ref[rows, :] + pltpu.roll(y, HALF_LANES, 1) * sin_ref[rows, :]
                    o_ref[rows, cols] = y.astype(o_ref.dtype)

    for sid, kinds in enumerate(sigs):
        @pl.when(sig_ref[n] == sid)
        def _(kinds=kinds):
            @pl.when(m == 0)
            def _():
                cast_weights(kinds)

            compute(kinds)


def in_proj(xg, ssq, w, li, kinds, gains, rope):
    m, k = xg.shape
    n = w.shape[2]
    heads = PROJ_TN // HEAD_DIM
    tiles = [tuple(kinds[i:i + heads]) for i in range(0, len(kinds), heads)]
    sigs = tuple(dict.fromkeys(tiles))
    sig_ids = jnp.asarray([sigs.index(t) for t in tiles], jnp.int32)
    moves, cos, sin = rope
    pos_blocks = SEQ // PROJ_TM
    return pl.pallas_call(
        functools.partial(_in_proj_body, sigs=sigs, moves=moves),
        grid=(n // PROJ_TN, m // PROJ_TM),
        in_specs=[pl.BlockSpec(memory_space=pltpu.SMEM),
                  pl.BlockSpec((PROJ_TM, k), lambda j, i: (i, 0)),
                  _ssq_spec(ssq, PROJ_TM),
                  pl.BlockSpec((None, k, PROJ_TN), lambda j, i: (li, 0, j)),
                  pl.BlockSpec((1, PROJ_TN), lambda j, i: (0, j)),
                  pl.BlockSpec((PROJ_TM, HEAD_DIM), lambda j, i: (i % pos_blocks, 0)),
                  pl.BlockSpec((PROJ_TM, HEAD_DIM), lambda j, i: (i % pos_blocks, 0))],
        out_specs=pl.BlockSpec((PROJ_TM, PROJ_TN), lambda j, i: (i, j)),
        out_shape=jax.ShapeDtypeStruct((m, n), BF16),
        scratch_shapes=[pltpu.VMEM((k, PROJ_TN), BF16), pltpu.VMEM((m, LANES), F32)],
        compiler_params=_params(),
        name="in_proj",
    )(sig_ids, xg, ssq, w, gains, cos, sin)


RES_TM = 512
RES_SUB = 256
RES_VMEM_WEIGHT_BYTES = 12 * 1024 * 1024


def _res_tn(kt, n):
    tn = n
    while kt * tn * 4 > RES_VMEM_WEIGHT_BYTES and tn % (2 * MXU_N) == 0:
        tn //= 2
    return tn


def _mm_res_body(*refs, n_in, tn, emit_norm):
    a_refs = refs[:n_in]
    if emit_norm:
        w_ref, x_ref, g_ref, o_ref, xg_ref, ssq_ref, wbf_ref = refs[n_in:]
    else:
        w_ref, x_ref, o_ref, wbf_ref = refs[n_in:]

    @pl.when(pl.program_id(1) == 0)
    def _():
        wbf_ref[...] = w_ref[...].astype(BF16)

    for r in range(RES_TM // RES_SUB):
        rows = slice(r * RES_SUB, (r + 1) * RES_SUB)
        a = a_refs[0][rows, :] if n_in == 1 else jnp.concatenate([a_ref[rows, :] for a_ref in a_refs], axis=1)
        part = None
        for p in range(tn // MXU_N):
            cols = slice(p * MXU_N, (p + 1) * MXU_N)
            acc = x_ref[rows, cols] + jnp.dot(a, wbf_ref[:, cols], preferred_element_type=F32)
            o_ref[rows, cols] = acc
            if emit_norm:
                xg_ref[rows, cols] = (acc * g_ref[:, cols]).astype(xg_ref.dtype)
                sq = _lane_partial_sums(acc * acc)
                part = sq if part is None else part + sq
        if emit_norm:
            ssq_ref[rows, :] = part


def matmul_residual(a_list, w, li, x, g_next=None):
    m, n = x.shape
    widths = tuple(a.shape[1] for a in a_list)
    kt = sum(widths)
    assert kt == w.shape[1]
    tn = _res_tn(kt, n)
    emit_norm = g_next is not None
    tile = pl.BlockSpec((RES_TM, tn), lambda j, i: (i, j))
    in_specs = [pl.BlockSpec((RES_TM, ka), lambda j, i: (i, 0)) for ka in widths]
    in_specs += [pl.BlockSpec((None, kt, tn), lambda j, i: (li, 0, j)), tile]
    args = [*a_list, w, x]
    out_specs, out_shape = tile, jax.ShapeDtypeStruct((m, n), F32)
    if emit_norm:
        in_specs.append(pl.BlockSpec((1, tn), lambda j, i: (0, j)))
        args.append(g_next.reshape(1, n).astype(F32))
        out_specs = [tile, tile, pl.BlockSpec((None, RES_TM, LANES), lambda j, i: (j, i, 0))]
        out_shape = [out_shape, jax.ShapeDtypeStruct((m, n), BF16), jax.ShapeDtypeStruct((n // tn, m, LANES), F32)]
    return pl.pallas_call(
        functools.partial(_mm_res_body, n_in=len(a_list), tn=tn, emit_norm=emit_norm),
        grid=(n // tn, m // RES_TM),
        in_specs=in_specs,
        out_specs=out_specs,
        out_shape=out_shape,
        scratch_shapes=[pltpu.VMEM((kt, tn), BF16)],
        compiler_params=_params(),
        name="matmul_residual",
    )(*args)


FFN_TM = 1024
FFN_TN = 512
FFN_SUB = 256


def _gate_up_body(x_ref, ssq_ref, wg_ref, wu_ref, o_ref, wgb_ref, wub_ref, inv_ref):
    row0 = pl.multiple_of(pl.program_id(1) * FFN_TM, FFN_TM)

    @pl.when(pl.program_id(1) == 0)
    def _():
        wgb_ref[...] = wg_ref[...].astype(BF16)
        wub_ref[...] = wu_ref[...].astype(BF16)

    @pl.when(pl.program_id(0) == 0)
    def _():
        _store_row_scales(ssq_ref, inv_ref, row0, FFN_TM, x_ref.shape[1])

    for r in range(FFN_TM // FFN_SUB):
        rows = slice(r * FFN_SUB, (r + 1) * FFN_SUB)
        x = x_ref[rows, :]
        inv = jnp.concatenate([inv_ref[pl.ds(row0 + r * FFN_SUB, FFN_SUB), :]] * (MXU_N // LANES), axis=1)
        for p in range(FFN_TN // MXU_N):
            cols = slice(p * MXU_N, (p + 1) * MXU_N)
            g = jnp.dot(x, wgb_ref[:, cols], preferred_element_type=F32) * inv
            u = jnp.dot(x, wub_ref[:, cols], preferred_element_type=F32) * inv
            o_ref[rows, cols] = (g * (1.0 / (1.0 + jnp.exp(-g))) * u).astype(o_ref.dtype)


def gate_up(xg, ssq, wg, wu, li):
    m, k = xg.shape
    n = wg.shape[2]
    return pl.pallas_call(
        _gate_up_body,
        grid=(n // FFN_TN, m // FFN_TM),
        in_specs=[pl.BlockSpec((FFN_TM, k), lambda j, i: (i, 0)),
                  _ssq_spec(ssq, FFN_TM),
                  pl.BlockSpec((None, k, FFN_TN), lambda j, i: (li, 0, j)),
                  pl.BlockSpec((None, k, FFN_TN), lambda j, i: (li, 0, j))],
        out_specs=pl.BlockSpec((FFN_TM, FFN_TN), lambda j, i: (i, j)),
        out_shape=jax.ShapeDtypeStruct((m, n), BF16),
        scratch_shapes=[pltpu.VMEM((k, FFN_TN), BF16), pltpu.VMEM((k, FFN_TN), BF16),
                        pltpu.VMEM((m, LANES), F32)],
        compiler_params=_params(),
        name="gate_up",
    )(xg, ssq, wg, wu)


def _two_stage_pipeline(n_blocks, per_stage, logits, finish, buf_a, buf_b):
    n_stages = n_blocks // per_stage
    assert n_stages * per_stage == n_blocks and n_stages % 2 == 0 and n_stages >= 2

    def stage(first, cur, nxt, with_next=True):
        for u in range(per_stage):
            finish(first + u, cur, u)
            if with_next:
                logits(first + per_stage + u, nxt, u)

    for u in range(per_stage):
        logits(u, buf_a, u)

    def pair(j, carry):
        first = 2 * j * per_stage
        stage(first, buf_a, buf_b)
        stage(first + per_stage, buf_b, buf_a)
        return carry

    lax.fori_loop(0, n_stages // 2 - 1, pair, 0)
    last = (n_stages - 2) * per_stage
    stage(last, buf_a, buf_b)
    stage(last + per_stage, buf_b, buf_a, with_next=False)


AB_QA, AB_KA, AB_VA = 0, A_HEADS, 2 * A_HEADS
AB_QB = 3 * A_HEADS
AB_KB = AB_QB + B_Q_HEADS
AB_VB = AB_KB + B_KV_HEADS
AB_KINDS = ("norm",) * (2 * A_HEADS) + ("plain",) * A_HEADS + ("rope",) * (B_Q_HEADS + B_KV_HEADS) \
    + ("plain",) * B_KV_HEADS
CD_QC = 0
CD_KC = C_Q_HEADS
CD_VC = CD_KC + C_KV_HEADS
CD_QD = CD_VC + C_KV_HEADS
CD_KD = CD_QD + D_Q_HEADS
CD_VD = CD_KD + D_SLOTS
CD_KINDS = ("rope",) * (C_Q_HEADS + C_KV_HEADS) + ("plain",) * C_KV_HEADS \
    + ("rope",) * (D_Q_HEADS + D_SLOTS) + ("plain",) * D_SLOTS


NA_QROWS = 2
NA_WIN_ROWS = 10
NA_Q = NA_QROWS * GRID_W
NA_KEYS = NA_WIN_ROWS * GRID_W
NA_BLOCKS = GRID_ROWS // NA_QROWS
NA_VARIANT_BLOCKS = (0, 1, 2, NA_BLOCKS - 2, NA_BLOCKS - 1)


def _na_win_start(rb):
    return np.clip(NA_QROWS * rb - NA_ROWS // 2, 0, GRID_ROWS - NA_WIN_ROWS)


N_DR = 2 * NA_ROWS - 1


def _na_row_slabs():
    dr_idx = np.full((len(NA_VARIANT_BLOCKS), NA_QROWS, NA_WIN_ROWS), N_DR, np.int32)
    for v, rb in enumerate(NA_VARIANT_BLOCKS):
        ws = _na_win_start(rb)
        for qr in range(NA_QROWS):
            qrow = NA_QROWS * rb + qr
            r0 = np.clip(qrow - NA_ROWS // 2, 0, GRID_ROWS - NA_ROWS)
            for kr in range(NA_WIN_ROWS):
                krow = ws + kr
                if r0 <= krow < r0 + NA_ROWS:
                    dr_idx[v, qr, kr] = krow - qrow + NA_ROWS - 1
    return dr_idx


def _na_column_table(rpb):
    assert rpb.shape == (A_HEADS, N_DR, 2 * NA_COLS - 1)
    lanes = 2 * GRID_W
    ext = jnp.concatenate([rpb.astype(F32) * LOG2E,
                           jnp.full((A_HEADS, N_DR, lanes - rpb.shape[2]), NEG, F32)], axis=-1)
    ext = jnp.roll(ext, -(NA_COLS - 1), axis=-1)
    toep = jnp.tile(ext, (1, 1, GRID_W))[:, :, :GRID_W * (lanes - 1)].reshape(A_HEADS, N_DR, GRID_W, lanes - 1)
    toep = toep[..., :GRID_W]
    cols = np.arange(GRID_W)
    c0 = np.clip(cols - NA_COLS // 2, 0, GRID_W - NA_COLS)
    col_ok = (cols[None, :] >= c0[:, None]) & (cols[None, :] < c0[:, None] + NA_COLS)
    toep = jnp.where(col_ok[None, None], toep, NEG)
    toep = jnp.concatenate([toep, jnp.full((A_HEADS, 1, GRID_W, GRID_W), NEG, F32)], axis=1)
    return jnp.concatenate([toep, toep], axis=-1)


NA_UNROLL = 4


def _na_body(q_ref, k_ref, v_ref, c_ref, o_ref, t_ref, sa_ref, sb_ref):
    left_half = lax.broadcasted_iota(jnp.int32, (GRID_W, 2 * GRID_W), 1) < GRID_W
    for v, per_q in enumerate(_na_row_slabs()):
        for qr, slabs in enumerate(per_q):
            for pair in range(NA_WIN_ROWS // 2):
                tile = jnp.where(left_half, c_ref[int(slabs[2 * pair])], c_ref[int(slabs[2 * pair + 1])])
                t_ref[v, qr * GRID_W:(qr + 1) * GRID_W, pair * 2 * GRID_W:(pair + 1) * 2 * GRID_W] = tile

    def offsets(rb):
        ws = jnp.clip(NA_QROWS * rb - NA_ROWS // 2, 0, GRID_ROWS - NA_WIN_ROWS)
        return pl.multiple_of(rb * NA_Q, NA_Q), pl.multiple_of(ws * GRID_W, GRID_W)

    def logits(rb, s_ref, u):
        qoff, koff = offsets(rb)
        variant = jnp.where(rb < 2, rb, jnp.where(rb >= NA_BLOCKS - 2, rb - (NA_BLOCKS - 5), 2))
        s_ref[u] = lax.dot_general(q_ref[pl.ds(qoff, NA_Q), :], k_ref[pl.ds(koff, NA_KEYS), :], NT_DIMS,
                                   preferred_element_type=F32) + t_ref[variant]

    def finish(rb, s_ref, u):
        qoff, koff = offsets(rb)
        s = s_ref[u]
        m = jnp.max(s, axis=-1, keepdims=True)
        p = jnp.exp2(s - m)
        l = jnp.sum(p, axis=-1, keepdims=True)
        o = jnp.dot(p.astype(BF16), v_ref[pl.ds(koff, NA_KEYS), :], preferred_element_type=F32)
        o_ref[pl.ds(qoff, NA_Q), :] = (o / l).astype(o_ref.dtype)

    _two_stage_pipeline(NA_BLOCKS, NA_UNROLL, logits, finish, sa_ref, sb_ref)


def neighbourhood_attention(h, rpb):
    nv = len(NA_VARIANT_BLOCKS)
    blk = (SEQ, HEAD_DIM)
    return pl.pallas_call(
        _na_body,
        grid=(BATCH, A_HEADS),
        in_specs=[pl.BlockSpec(blk, lambda b, hd: (b, AB_QA + hd)),
                  pl.BlockSpec(blk, lambda b, hd: (b, AB_KA + hd)),
                  pl.BlockSpec(blk, lambda b, hd: (b, AB_VA + hd)),
                  pl.BlockSpec((None, N_DR + 1, GRID_W, 2 * GRID_W), lambda b, hd: (hd, 0, 0, 0))],
        out_specs=pl.BlockSpec(blk, lambda b, hd: (b, hd)),
        out_shape=jax.ShapeDtypeStruct((M_ROWS, A_HEADS * HEAD_DIM), BF16),
        scratch_shapes=[pltpu.VMEM((nv, NA_Q, NA_KEYS), F32),
                        pltpu.VMEM((NA_UNROLL, NA_Q, NA_KEYS), F32),
                        pltpu.VMEM((NA_UNROLL, NA_Q, NA_KEYS), F32)],
        compiler_params=_params(),
        name="neighbourhood_attention",
    )(h, h, h, _na_column_table(rpb))


B_GROUP = B_Q_HEADS // B_KV_HEADS


DENSE_TQ = 256
DENSE_TK = 256


def _dense_body(q_ref, k_ref, v_ref, o_ref):
    tq = DENSE_TQ
    q = jnp.concatenate([q_ref[:, g * HEAD_DIM:(g + 1) * HEAD_DIM] for g in range(B_GROUP)], axis=0)
    ones = jnp.ones((DENSE_TK, HEAD_DIM), BF16)
    m = acc = None
    for c in range(SEQ // DENSE_TK):
        keys = slice(c * DENSE_TK, (c + 1) * DENSE_TK)
        s = lax.dot_general(q, k_ref[keys, :], NT_DIMS, preferred_element_type=F32).astype(BF16)
        v_ones = jnp.concatenate([v_ref[keys, :], ones], axis=1)
        row_max = jnp.max(s, axis=-1, keepdims=True)
        if c == 0:
            m = row_max
            acc = jnp.dot(jnp.exp2(s - m), v_ones, preferred_element_type=F32)
        else:
            m_new = jnp.maximum(m, row_max)
            alpha = jnp.exp2(m.astype(F32) - m_new.astype(F32))
            acc = alpha * acc + jnp.dot(jnp.exp2(s - m_new), v_ones, preferred_element_type=F32)
            m = m_new
    o = acc[:, :HEAD_DIM] / acc[:, HEAD_DIM:]
    for g in range(B_GROUP):
        o_ref[:, g * HEAD_DIM:(g + 1) * HEAD_DIM] = o[g * tq:(g + 1) * tq].astype(o_ref.dtype)


def dense_gqa(h):
    tq = DENSE_TQ
    nq = SEQ // tq
    gw = B_GROUP * HEAD_DIM
    assert AB_QB % B_GROUP == 0
    q_blk0 = AB_QB // B_GROUP
    blk = (SEQ, HEAD_DIM)
    return pl.pallas_call(
        _dense_body,
        grid=(BATCH, B_KV_HEADS, nq),
        in_specs=[pl.BlockSpec((tq, gw), lambda b, kv, i: (b * nq + i, q_blk0 + kv)),
                  pl.BlockSpec(blk, lambda b, kv, i: (b, AB_KB + kv)),
                  pl.BlockSpec(blk, lambda b, kv, i: (b, AB_VB + kv))],
        out_specs=pl.BlockSpec((tq, gw), lambda b, kv, i: (b * nq + i, kv)),
        out_shape=jax.ShapeDtypeStruct((M_ROWS, B_Q_HEADS * HEAD_DIM), BF16),
        compiler_params=_params(),
        name="dense_gqa",
    )(h, h, h)


QBLK = 128
N_QBLK = SEQ // QBLK


def _band_mask_table(span):
    assert span <= QBLK
    col = np.arange(QBLK + 2 * span)[None, :]
    row = np.arange(QBLK)[:, None]
    t = np.stack([np.where(np.abs(col - row + shift) <= span, 0.0, NEG) for shift in (0, -span, -2 * span)])
    return jnp.asarray(t, F32)


def _band_window(ub, blocks, span):
    start = jnp.clip(ub * QBLK - span, 0, blocks * QBLK - (QBLK + 2 * span))
    variant = jnp.where(ub == 0, 0, jnp.where(ub == blocks - 1, 2, 1))
    return start, variant


C_GROUP = C_Q_HEADS // C_KV_HEADS


C_WIN = QBLK + 2 * C_WINDOW
C_UNROLL = 2


def _window_sink_body(sink_ref, q_ref, k_ref, v_ref, mask_ref, o_ref, sa_ref, sb_ref):
    kv = pl.program_id(1)

    def logits(i, s_ref, u):
        qoff = pl.multiple_of(i * QBLK, QBLK)
        start, _ = _band_window(i, N_QBLK, C_WINDOW)
        q = q_ref[pl.ds(qoff, QBLK), :]
        qs = jnp.concatenate([q[:, g * HEAD_DIM:(g + 1) * HEAD_DIM] for g in range(C_GROUP)], axis=0)
        s_ref[u] = lax.dot_general(qs, k_ref[pl.ds(pl.multiple_of(start, QBLK), C_WIN), :], NT_DIMS,
                                   preferred_element_type=F32)

    def finish(i, s_ref, u):
        qoff = pl.multiple_of(i * QBLK, QBLK)
        start, variant = _band_window(i, N_QBLK, C_WINDOW)
        mask = mask_ref[variant]
        ps, ls = [], []
        for g in range(C_GROUP):
            sg = s_ref[u, g * QBLK:(g + 1) * QBLK, :] + mask
            sink = sink_ref[kv * C_GROUP + g] * LOG2E
            m = jnp.maximum(jnp.max(sg, axis=-1, keepdims=True), sink)
            p = jnp.exp2(sg - m)
            ls.append(jnp.sum(p, axis=-1, keepdims=True) + jnp.exp2(sink - m))
            ps.append(p.astype(BF16))
        o = jnp.dot(jnp.concatenate(ps, axis=0), v_ref[pl.ds(pl.multiple_of(start, QBLK), C_WIN), :],
                    preferred_element_type=F32)
        for g in range(C_GROUP):
            o_ref[pl.ds(qoff, QBLK), g * HEAD_DIM:(g + 1) * HEAD_DIM] = (
                o[g * QBLK:(g + 1) * QBLK] / ls[g]).astype(o_ref.dtype)

    _two_stage_pipeline(N_QBLK, C_UNROLL, logits, finish, sa_ref, sb_ref)


def window_gqa_sink(h, sink):
    gw = C_GROUP * HEAD_DIM
    assert CD_QC % C_GROUP == 0
    q_blk0 = CD_QC // C_GROUP
    blk = (SEQ, HEAD_DIM)
    return pl.pallas_call(
        _window_sink_body,
        grid=(BATCH, C_KV_HEADS),
        in_specs=[pl.BlockSpec(memory_space=pltpu.SMEM),
                  pl.BlockSpec((SEQ, gw), lambda b, kv: (b, q_blk0 + kv)),
                  pl.BlockSpec(blk, lambda b, kv: (b, CD_KC + kv)),
                  pl.BlockSpec(blk, lambda b, kv: (b, CD_VC + kv)),
                  pl.BlockSpec((3, QBLK, C_WIN), lambda b, kv: (0, 0, 0))],
        out_specs=pl.BlockSpec((SEQ, gw), lambda b, kv: (b, kv)),
        out_shape=jax.ShapeDtypeStruct((M_ROWS, C_Q_HEADS * HEAD_DIM), BF16),
        scratch_shapes=[pltpu.VMEM((C_UNROLL, C_GROUP * QBLK, C_WIN), F32),
                        pltpu.VMEM((C_UNROLL, C_GROUP * QBLK, C_WIN), F32)],
        compiler_params=_params(),
        name="window_gqa_sink",
    )(sink.astype(F32), h, h, h, _band_mask_table(C_WINDOW))


D_DILS = tuple(d for _, d in D_DILATIONS)
D_SPAN = D_DILATIONS[0][0] // 2
assert all((w // 2) // d == D_SPAN and N_QBLK % d == 0 for w, d in D_DILATIONS) and D_DILS[0] == 1
D_WIN = QBLK + 2 * D_SPAN
D_UNROLL = 2


def _dilated_body(q0_ref, q1_ref, q2_ref, k_ref, v_ref, mask_ref, o_ref,
                  stage_ref, qc_ref, kc_ref, vc_ref, og_ref, lse_ref, sa_ref, sb_ref):
    def to_class_major(dst_ref, slot, d):
        run = SEQ // d
        for rho in range(d):
            dst_ref[slot, pl.ds(rho * run, run), :] = stage_ref[pl.ds(rho, run, stride=d), :].astype(BF16)

    for src_ref, dst_ref in ((k_ref, kc_ref), (v_ref, vc_ref)):
        stage_ref[...] = src_ref[...].astype(F32)
        for g in range(1, D_GROUPS):
            to_class_major(dst_ref, g - 1, D_DILS[g])
    for g, src_ref in ((1, q1_ref), (2, q2_ref)):
        stage_ref[...] = src_ref[...].astype(F32)
        to_class_major(qc_ref, g - 1, D_DILS[g])

    def geometry(g, i):
        d = D_DILS[g]
        run_blocks = N_QBLK // d
        rho = i // run_blocks
        ub = i % run_blocks
        start, variant = _band_window(ub, run_blocks, D_SPAN)
        koff = pl.multiple_of(rho * (run_blocks * QBLK) + start, D_SPAN)
        return pl.multiple_of(i * QBLK, QBLK), koff, variant, rho, ub

    def logits(i, s_ref, u):
        for g in range(D_GROUPS):
            qoff, koff, variant, _, _ = geometry(g, i)
            if g == 0:
                q, k = q0_ref[pl.ds(qoff, QBLK), :], k_ref[pl.ds(koff, D_WIN), :]
            else:
                q, k = qc_ref[g - 1, pl.ds(qoff, QBLK), :], kc_ref[g - 1, pl.ds(koff, D_WIN), :]
            s_ref[u * D_GROUPS + g] = lax.dot_general(q, k, NT_DIMS, preferred_element_type=F32) + mask_ref[variant]

    def finish(i, s_ref, u):
        for g in range(D_GROUPS):
            d = D_DILS[g]
            qoff, koff, _, rho, ub = geometry(g, i)
            v = v_ref[pl.ds(koff, D_WIN), :] if g == 0 else vc_ref[g - 1, pl.ds(koff, D_WIN), :]
            s = s_ref[u * D_GROUPS + g]
            m = jnp.max(s, axis=-1, keepdims=True)
            p = jnp.exp2(s - m)
            l = jnp.sum(p, axis=-1, keepdims=True)
            o = jnp.dot(p.astype(BF16), v, preferred_element_type=F32) / l
            lse = jnp.broadcast_to(m + jnp.log2(l), (QBLK, HEAD_DIM))
            if d == 1:
                rows = pl.ds(qoff, QBLK)
            else:
                rows = pl.ds(rho + d * ub * QBLK, QBLK, stride=d)
            og_ref[g, rows, :] = o
            lse_ref[g, rows, :] = lse

    _two_stage_pipeline(N_QBLK, D_UNROLL, logits, finish, sa_ref, sb_ref)

    def merge(c, carry):
        rows = pl.ds(pl.multiple_of(c * QBLK, QBLK), QBLK)
        lses = [lse_ref[g, rows, :] for g in range(D_GROUPS)]
        top = functools.reduce(jnp.maximum, lses)
        ws = [jnp.exp2(x - top) for x in lses]
        num = sum(w * og_ref[g, rows, :] for g, w in enumerate(ws))
        o_ref[rows, :] = (num / sum(ws)).astype(o_ref.dtype)
        return carry

    lax.fori_loop(0, N_QBLK, merge, 0)


def dilated_attention(h):
    blk = (SEQ, HEAD_DIM)
    return pl.pallas_call(
        _dilated_body,
        grid=(BATCH, D_SLOTS),
        in_specs=[pl.BlockSpec(blk, lambda b, s: (b, CD_QD + s)),
                  pl.BlockSpec(blk, lambda b, s: (b, CD_QD + D_SLOTS + s)),
                  pl.BlockSpec(blk, lambda b, s: (b, CD_QD + 2 * D_SLOTS + s)),
                  pl.BlockSpec(blk, lambda b, s: (b, CD_KD + s)),
                  pl.BlockSpec(blk, lambda b, s: (b, CD_VD + s)),
                  pl.BlockSpec((3, QBLK, D_WIN), lambda b, s: (0, 0, 0))],
        out_specs=pl.BlockSpec(blk, lambda b, s: (b, s)),
        out_shape=jax.ShapeDtypeStruct((M_ROWS, D_SLOTS * HEAD_DIM), BF16),
        scratch_shapes=[pltpu.VMEM((SEQ, HEAD_DIM), F32),
                        pltpu.VMEM((D_GROUPS - 1, SEQ, HEAD_DIM), BF16),
                        pltpu.VMEM((D_GROUPS - 1, SEQ, HEAD_DIM), BF16),
                        pltpu.VMEM((D_GROUPS - 1, SEQ, HEAD_DIM), BF16),
                        pltpu.VMEM((D_GROUPS, SEQ, HEAD_DIM), F32),
                        pltpu.VMEM((D_GROUPS, SEQ, HEAD_DIM), F32),
                        pltpu.VMEM((D_UNROLL * D_GROUPS, QBLK, D_WIN), F32),
                        pltpu.VMEM((D_UNROLL * D_GROUPS, QBLK, D_WIN), F32)],
        compiler_params=_params(),
        name="dilated_attention",
    )(h, h, h, h, h, _band_mask_table(D_SPAN))


def _rope_tables(kind):
    pos = np.arange(SEQ, dtype=np.float64)
    cos = np.ones((SEQ, HEAD_DIM), np.float64)
    sin = np.zeros((SEQ, HEAD_DIM), np.float64)

    def fill(start, r, p, theta):
        half = r // 2
        inv = np.exp(-math.log(theta) * np.arange(half, dtype=np.float64) * (2.0 / r))
        ang = p[:, None] * inv[None, :]
        cos[:, start:start + half] = np.cos(ang)
        cos[:, start + half:start + r] = np.cos(ang)
        sin[:, start:start + half] = -np.sin(ang)
        sin[:, start + half:start + r] = np.sin(ang)

    if kind == "axial":
        hw = HEAD_DIM // 2
        fill(0, hw, np.floor(pos / GRID_W), AXIAL_THETA)
        fill(hw, hw, pos % GRID_W, AXIAL_THETA)
        moves = AXIAL_MOVES
    else:
        fill(0, ROPE_DIMS, pos, ROPE_THETA)
        moves = PARTIAL_MOVES
    perm = _moves_to_perm(moves)
    return moves, jnp.asarray(cos[:, perm], dtype=F32), jnp.asarray(sin[:, perm], dtype=F32)


def _gain_row(parts):
    rows = []
    for g, heads, scale, perm in parts:
        if g is None:
            g = jnp.ones((HEAD_DIM,), F32)
        g = g.astype(F32) * scale
        if perm is not None:
            g = g[perm]
        rows.append(jnp.tile(g, heads))
    return jnp.concatenate(rows).reshape(1, -1)


def _layer_ab(x, xg, ssq, w_in, w_out, li, g_next, a_qn, a_kn, a_rpb, b_qn, b_kn, axial):
    perm = _moves_to_perm(axial[0])
    gains = _gain_row([(a_qn, A_HEADS, SCALE, None), (a_kn, A_HEADS, 1.0, None), (None, A_HEADS, 1.0, None),
                       (b_qn, B_Q_HEADS, SCALE, perm), (b_kn, B_KV_HEADS, 1.0, perm),
                       (None, B_KV_HEADS, 1.0, None)])
    h = in_proj(xg, ssq, w_in, li, AB_KINDS, gains, axial)
    oa = neighbourhood_attention(h, a_rpb)
    ob = dense_gqa(h)
    return matmul_residual([oa, ob], w_out, li, x, g_next)


def _layer_cd(x, xg, ssq, w_in, w_out, li, g_next, c_qn, c_kn, c_sink, d_qn, d_kn, partial):
    perm = _moves_to_perm(partial[0])
    gains = _gain_row([(c_qn, C_Q_HEADS, SCALE, perm), (c_kn, C_KV_HEADS, 1.0, perm), (None, C_KV_HEADS, 1.0, None),
                       (d_qn, D_Q_HEADS, SCALE, perm), (d_kn, D_SLOTS, 1.0, perm), (None, D_SLOTS, 1.0, None)])
    h = in_proj(xg, ssq, w_in, li, CD_KINDS, gains, partial)
    oc = window_gqa_sink(h, c_sink)
    od = dilated_attention(h)
    return matmul_residual([oc, od], w_out, li, x, g_next)


def kernel(x, attn_norm, ffn_norm, ab_w_in, ab_w_out, a_q_norm, a_k_norm, a_rpb, b_q_norm, b_k_norm,
           cd_w_in, cd_w_out, c_q_norm, c_k_norm, c_sink, d_q_norm, d_k_norm, w_gate, w_up, w_down):
    axial = _rope_tables("axial")
    partial = _rope_tables("partial")
    x = x.reshape(M_ROWS, D_MODEL)
    xg, ssq = prenorm(x, attn_norm[0])
    for layer in range(DEPTH):
        j = layer // 2
        if layer % 2 == 0:
            x, xg, ssq = _layer_ab(x, xg, ssq, ab_w_in, ab_w_out, j, ffn_norm[layer], a_q_norm[j], a_k_norm[j],
                                   a_rpb[j], b_q_norm[j], b_k_norm[j], axial)
        else:
            x, xg, ssq = _layer_cd(x, xg, ssq, cd_w_in, cd_w_out, j, ffn_norm[layer], c_q_norm[j], c_k_norm[j],
                                   c_sink[j], d_q_norm[j], d_k_norm[j], partial)
        hidden = gate_up(xg, ssq, w_gate, w_up, layer)
        if layer + 1 < DEPTH:
            x, xg, ssq = matmul_residual([hidden], w_down, layer, x, attn_norm[layer + 1])
        else:
            x = matmul_residual([hidden], w_down, layer, x)
    return x.reshape(BATCH, SEQ, D_MODEL)
```

```python
import functools
import math

import numpy as np
import jax
import jax.numpy as jnp
from jax import lax
from jax.experimental import pallas as pl
from jax.experimental.pallas import tpu as pltpu

D_MODEL = 2048
BATCH = 2
SEQ = 4096
DEPTH = 4
HEAD_DIM = 128
GRID_W = 64
GRID_ROWS = SEQ // GRID_W
EPS = 1e-6
A_HEADS = 8
NA_ROWS = 8
NA_COLS = 16
B_Q_HEADS = 8
B_KV_HEADS = 2
AXIAL_THETA = 10000.0
C_Q_HEADS = 12
C_KV_HEADS = 4
C_WINDOW = 128
D_DILATIONS = ((128, 1), (512, 4), (2048, 16))
D_GROUPS = len(D_DILATIONS)
D_SLOTS = 4
D_Q_HEADS = D_GROUPS * D_SLOTS
ROPE_THETA = 500000.0
ROPE_DIMS = HEAD_DIM // 4
FFN_HIDDEN = ((-(-8 * D_MODEL // 3)) + 255) // 256 * 256
M_ROWS = BATCH * SEQ
LOG2E = math.log2(math.e)
SCALE = HEAD_DIM ** -0.5 * LOG2E
NEG = -1e30

VMEM_LIMIT_BYTES = 52 * 1024 * 1024
MXU_N = 256

F32 = jnp.float32
BF16 = jnp.bfloat16
NT_DIMS = (((1,), (1,)), ((), ()))


def _params(**kw):
    return pltpu.CompilerParams(vmem_limit_bytes=VMEM_LIMIT_BYTES, **kw)


LANES = 128


def _lane_partial_sums(sq):
    out = sq[:, :LANES]
    for i in range(1, sq.shape[1] // LANES):
        out = out + sq[:, i * LANES:(i + 1) * LANES]
    return out


ROW_SCALE_CHUNK = 256


def _store_row_scales(ssq_ref, inv_ref, row0, n_rows, dim):
    for c in range(n_rows // ROW_SCALE_CHUNK):
        rows = slice(c * ROW_SCALE_CHUNK, (c + 1) * ROW_SCALE_CHUNK)
        part = ssq_ref[0, rows, :]
        for t in range(1, ssq_ref.shape[0]):
            part = part + ssq_ref[t, rows, :]
        inv = lax.rsqrt(jnp.sum(part, axis=-1, keepdims=True) * (1.0 / dim) + EPS)
        inv_ref[pl.ds(row0 + c * ROW_SCALE_CHUNK, ROW_SCALE_CHUNK), :] = jnp.broadcast_to(
            inv, (ROW_SCALE_CHUNK, LANES))


def _ssq_spec(ssq, tm):
    return pl.BlockSpec((ssq.shape[0], tm, LANES), lambda j, i: (0, jnp.where(j == 0, i, 0), 0))


def _prenorm_body(x_ref, g_ref, xg_ref, ssq_ref):
    x = x_ref[...]
    xg_ref[...] = (x * g_ref[...]).astype(xg_ref.dtype)
    ssq_ref[...] = _lane_partial_sums(x * x)


def prenorm(x, g, tm=512):
    m, d = x.shape
    return pl.pallas_call(
        _prenorm_body,
        grid=(m // tm,),
        in_specs=[pl.BlockSpec((tm, d), lambda i: (i, 0)), pl.BlockSpec((1, d), lambda i: (0, 0))],
        out_specs=[pl.BlockSpec((tm, d), lambda i: (i, 0)), pl.BlockSpec((None, tm, LANES), lambda i: (0, i, 0))],
        out_shape=[jax.ShapeDtypeStruct((m, d), BF16), jax.ShapeDtypeStruct((1, m, LANES), F32)],
        compiler_params=_params(),
        name="prenorm",
    )(x, g.reshape(1, d).astype(F32))


HALF_LANES = HEAD_DIM // 2
AXIAL_MOVES = ((96, 32, 64), (32, 64, 96))
PARTIAL_MOVES = ((112, 16, 64), (48, 64, 80))


def _moves_to_perm(moves):
    perm = np.arange(HEAD_DIM)
    for shift, lo, hi in moves:
        perm[lo:hi] = (np.arange(lo, hi) - shift) % HEAD_DIM
    assert sorted(perm.tolist()) == list(range(HEAD_DIM))
    return perm


def _permute_lanes(w, moves):
    lane = lax.broadcasted_iota(jnp.int32, w.shape, 1)
    out = w
    for shift, lo, hi in moves:
        out = jnp.where((lane >= lo) & (lane < hi), pltpu.roll(w, shift, 1), out)
    return out


PROJ_TM = 2048
PROJ_TN = 512
PROJ_SUB = 256


def _in_proj_body(sig_ref, x_ref, ssq_ref, w_ref, g_ref, cos_ref, sin_ref, o_ref, wbf_ref, inv_ref, *,
                  sigs, moves):
    n = pl.program_id(0)
    m = pl.program_id(1)
    heads = PROJ_TN // HEAD_DIM
    row0 = pl.multiple_of(m * PROJ_TM, PROJ_TM)

    @pl.when(n == 0)
    def _():
        _store_row_scales(ssq_ref, inv_ref, row0, PROJ_TM, x_ref.shape[1])

    def cast_weights(kinds):
        for h in range(heads):
            cols = slice(h * HEAD_DIM, (h + 1) * HEAD_DIM)
            w = w_ref[:, cols]
            if kinds[h] == "rope":
                w = _permute_lanes(w, moves)
            wbf_ref[:, cols] = w.astype(BF16)

    def compute(kinds):
        for r in range(PROJ_TM // PROJ_SUB):
            rows = slice(r * PROJ_SUB, (r + 1) * PROJ_SUB)
            x = x_ref[rows, :]
            inv = inv_ref[pl.ds(row0 + r * PROJ_SUB, PROJ_SUB), :]
            for p in range(PROJ_TN // MXU_N):
                acc = jnp.dot(x, wbf_ref[:, p * MXU_N:(p + 1) * MXU_N], preferred_element_type=F32)
                for hh in range(MXU_N // HEAD_DIM):
                    h = p * (MXU_N // HEAD_DIM) + hh
                    cols = slice(h * HEAD_DIM, (h + 1) * HEAD_DIM)
                    y = acc[:, hh * HEAD_DIM:(hh + 1) * HEAD_DIM]
                    if kinds[h] == "plain":
                        y = y * inv
                    else:
                        ms = jnp.mean(y * y, axis=-1, keepdims=True)
                        y = y * (inv * lax.rsqrt(ms * (inv * inv) + EPS)) * g_ref[:, cols]
                    if kinds[h] == "rope":
                        y = y * cos_ref[rows, :] + pltpu.roll(y, HALF_LANES, 1) * sin_ref[rows, :]
                    o_ref[rows, cols] = y.astype(o_ref.dtype)

    for sid, kinds in enumerate(sigs):
        @pl.when(sig_ref[n] == sid)
        def _(kinds=kinds):
            @pl.when(m == 0)
            def _():
                cast_weights(kinds)

            compute(kinds)


def in_proj(xg, ssq, w, li, kinds, gains, rope):
    m, k = xg.shape
    n = w.shape[2]
    heads = PROJ_TN // HEAD_DIM
    tiles = [tuple(kinds[i:i + heads]) for i in range(0, len(kinds), heads)]
    sigs = tuple(dict.fromkeys(tiles))
    sig_ids = jnp.asarray([sigs.index(t) for t in tiles], jnp.int32)
    moves, cos, sin = rope
    pos_blocks = SEQ // PROJ_TM
    return pl.pallas_call(
        functools.partial(_in_proj_body, sigs=sigs, moves=moves),
        grid=(n // PROJ_TN, m // PROJ_TM),
        in_specs=[pl.BlockSpec(memory_space=pltpu.SMEM),
                  pl.BlockSpec((PROJ_TM, k), lambda j, i: (i, 0)),
                  _ssq_spec(ssq, PROJ_TM),
                  pl.BlockSpec((None, k, PROJ_TN), lambda j, i: (li, 0, j)),
                  pl.BlockSpec((1, PROJ_TN), lambda j, i: (0, j)),
                  pl.BlockSpec((PROJ_TM, HEAD_DIM), lambda j, i: (i % pos_blocks, 0)),
                  pl.BlockSpec((PROJ_TM, HEAD_DIM), lambda j, i: (i % pos_blocks, 0))],
        out_specs=pl.BlockSpec((PROJ_TM, PROJ_TN), lambda j, i: (i, j)),
        out_shape=jax.ShapeDtypeStruct((m, n), BF16),
        scratch_shapes=[pltpu.VMEM((k, PROJ_TN), BF16), pltpu.VMEM((m, LANES), F32)],
        compiler_params=_params(),
        name="in_proj",
    )(sig_ids, xg, ssq, w, gains, cos, sin)


RES_SUB = 256
RES_WEIGHT_TILE_BYTES = 12 * 1024 * 1024
RES_WEIGHT_FULL_BYTES = 16 * 1024 * 1024


def _res_tiles(kt, n):
    if kt * n * 4 <= RES_WEIGHT_FULL_BYTES:
        return 512, n, True
    tn = n
    while kt * tn * 4 > RES_WEIGHT_TILE_BYTES and tn % (2 * MXU_N) == 0:
        tn //= 2
    return 512, tn, False


def _mm_res_body(*refs, n_in, tm, tn, emit_norm):
    a_refs = refs[:n_in]
    if emit_norm:
        w_ref, x_ref, g_ref, o_ref, xg_ref, ssq_ref, wbf_ref = refs[n_in:]
    else:
        w_ref, x_ref, o_ref, wbf_ref = refs[n_in:]

    @pl.when(pl.program_id(1) == 0)
    def _():
        wbf_ref[...] = w_ref[...].astype(BF16)

    for r in range(tm // RES_SUB):
        rows = slice(r * RES_SUB, (r + 1) * RES_SUB)
        a = a_refs[0][rows, :] if n_in == 1 else jnp.concatenate([a_ref[rows, :] for a_ref in a_refs], axis=1)
        part = None
        for p in range(tn // MXU_N):
            cols = slice(p * MXU_N, (p + 1) * MXU_N)
            acc = x_ref[rows, cols] + jnp.dot(a, wbf_ref[:, cols], preferred_element_type=F32)
            o_ref[rows, cols] = acc
            if emit_norm:
                xg_ref[rows, cols] = (acc * g_ref[:, cols]).astype(xg_ref.dtype)
                sq = _lane_partial_sums(acc * acc)
                part = sq if part is None else part + sq
        if emit_norm:
            ssq_ref[rows, :] = part


def matmul_residual(a_list, w, li, x, g_next=None):
    m, n = x.shape
    widths = tuple(a.shape[1] for a in a_list)
    kt = sum(widths)
    assert kt == w.shape[1]
    tm, tn, whole = _res_tiles(kt, n)
    emit_norm = g_next is not None
    tile = pl.BlockSpec((tm, tn), lambda j, i: (i, j))
    in_specs = [pl.BlockSpec((tm, ka), lambda j, i: (i, 0)) for ka in widths]
    w_mode = dict(pipeline_mode=pl.Buffered(1)) if whole else {}
    in_specs += [pl.BlockSpec((None, kt, tn), lambda j, i: (li, 0, j), **w_mode), tile]
    args = [*a_list, w, x]
    out_specs, out_shape = tile, jax.ShapeDtypeStruct((m, n), F32)
    if emit_norm:
        in_specs.append(pl.BlockSpec((1, tn), lambda j, i: (0, j)))
        args.append(g_next.reshape(1, n).astype(F32))
        out_specs = [tile, tile, pl.BlockSpec((None, tm, LANES), lambda j, i: (j, i, 0))]
        out_shape = [out_shape, jax.ShapeDtypeStruct((m, n), BF16), jax.ShapeDtypeStruct((n // tn, m, LANES), F32)]
    return pl.pallas_call(
        functools.partial(_mm_res_body, n_in=len(a_list), tm=tm, tn=tn, emit_norm=emit_norm),
        grid=(n // tn, m // tm),
        in_specs=in_specs,
        out_specs=out_specs,
        out_shape=out_shape,
        scratch_shapes=[pltpu.VMEM((kt, tn), BF16)],
        compiler_params=_params(),
        name="matmul_residual",
    )(*args)


FFN_TM = 2048
FFN_TN = 512
FFN_SUB = 256


def _gate_up_body(x_ref, ssq_ref, wg_ref, wu_ref, o_ref, wgb_ref, wub_ref, inv_ref):
    row0 = pl.multiple_of(pl.program_id(1) * FFN_TM, FFN_TM)

    @pl.when(pl.program_id(1) == 0)
    def _():
        wgb_ref[...] = wg_ref[...].astype(BF16)
        wub_ref[...] = wu_ref[...].astype(BF16)

    @pl.when(pl.program_id(0) == 0)
    def _():
        _store_row_scales(ssq_ref, inv_ref, row0, FFN_TM, x_ref.shape[1])

    for r in range(FFN_TM // FFN_SUB):
        rows = slice(r * FFN_SUB, (r + 1) * FFN_SUB)
        x = x_ref[rows, :]
        inv = jnp.concatenate([inv_ref[pl.ds(row0 + r * FFN_SUB, FFN_SUB), :]] * (MXU_N // LANES), axis=1)
        for p in range(FFN_TN // MXU_N):
            cols = slice(p * MXU_N, (p + 1) * MXU_N)
            g = jnp.dot(x, wgb_ref[:, cols], preferred_element_type=F32) * inv
            u = jnp.dot(x, wub_ref[:, cols], preferred_element_type=F32) * inv
            o_ref[rows, cols] = (g * (1.0 / (1.0 + jnp.exp(-g))) * u).astype(o_ref.dtype)


def gate_up(xg, ssq, wg, wu, li):
    m, k = xg.shape
    n = wg.shape[2]
    return pl.pallas_call(
        _gate_up_body,
        grid=(n // FFN_TN, m // FFN_TM),
        in_specs=[pl.BlockSpec((FFN_TM, k), lambda j, i: (i, 0)),
                  _ssq_spec(ssq, FFN_TM),
                  pl.BlockSpec((None, k, FFN_TN), lambda j, i: (li, 0, j)),
                  pl.BlockSpec((None, k, FFN_TN), lambda j, i: (li, 0, j))],
        out_specs=pl.BlockSpec((FFN_TM, FFN_TN), lambda j, i: (i, j)),
        out_shape=jax.ShapeDtypeStruct((m, n), BF16),
        scratch_shapes=[pltpu.VMEM((k, FFN_TN), BF16), pltpu.VMEM((k, FFN_TN), BF16),
                        pltpu.VMEM((m, LANES), F32)],
        compiler_params=_params(),
        name="gate_up",
    )(xg, ssq, wg, wu)


def _two_stage_pipeline(n_blocks, per_stage, logits, finish, buf_a, buf_b):
    n_stages = n_blocks // per_stage
    assert n_stages * per_stage == n_blocks and n_stages % 2 == 0 and n_stages >= 2

    def stage(first, cur, nxt, with_next=True):
        for u in range(per_stage):
            finish(first + u, cur, u)
            if with_next:
                logits(first + per_stage + u, nxt, u)

    for u in range(per_stage):
        logits(u, buf_a, u)

    def pair(j, carry):
        first = 2 * j * per_stage
        stage(first, buf_a, buf_b)
        stage(first + per_stage, buf_b, buf_a)
        return carry

    lax.fori_loop(0, n_stages // 2 - 1, pair, 0)
    last = (n_stages - 2) * per_stage
    stage(last, buf_a, buf_b)
    stage(last + per_stage, buf_b, buf_a, with_next=False)


AB_QA, AB_KA, AB_VA = 0, A_HEADS, 2 * A_HEADS
AB_QB = 3 * A_HEADS
AB_KB = AB_QB + B_Q_HEADS
AB_VB = AB_KB + B_KV_HEADS
AB_KINDS = ("norm",) * (2 * A_HEADS) + ("plain",) * A_HEADS + ("rope",) * (B_Q_HEADS + B_KV_HEADS) \
    + ("plain",) * B_KV_HEADS
CD_QC = 0
CD_KC = C_Q_HEADS
CD_VC = CD_KC + C_KV_HEADS
CD_QD = CD_VC + C_KV_HEADS
CD_KD = CD_QD + D_Q_HEADS
CD_VD = CD_KD + D_SLOTS
CD_KINDS = ("rope",) * (C_Q_HEADS + C_KV_HEADS) + ("plain",) * C_KV_HEADS \
    + ("rope",) * (D_Q_HEADS + D_SLOTS) + ("plain",) * D_SLOTS


NA_QROWS = 2
NA_WIN_ROWS = 10
NA_Q = NA_QROWS * GRID_W
NA_KEYS = NA_WIN_ROWS * GRID_W
NA_BLOCKS = GRID_ROWS // NA_QROWS
NA_VARIANT_BLOCKS = (0, 1, 2, NA_BLOCKS - 2, NA_BLOCKS - 1)


def _na_win_start(rb):
    return np.clip(NA_QROWS * rb - NA_ROWS // 2, 0, GRID_ROWS - NA_WIN_ROWS)


N_DR = 2 * NA_ROWS - 1


def _na_row_slabs():
    dr_idx = np.full((len(NA_VARIANT_BLOCKS), NA_QROWS, NA_WIN_ROWS), N_DR, np.int32)
    for v, rb in enumerate(NA_VARIANT_BLOCKS):
        ws = _na_win_start(rb)
        for qr in range(NA_QROWS):
            qrow = NA_QROWS * rb + qr
            r0 = np.clip(qrow - NA_ROWS // 2, 0, GRID_ROWS - NA_ROWS)
            for kr in range(NA_WIN_ROWS):
                krow = ws + kr
                if r0 <= krow < r0 + NA_ROWS:
                    dr_idx[v, qr, kr] = krow - qrow + NA_ROWS - 1
    return dr_idx


def _na_column_table(rpb):
    assert rpb.shape == (A_HEADS, N_DR, 2 * NA_COLS - 1)
    lanes = 2 * GRID_W
    ext = jnp.concatenate([rpb.astype(F32) * LOG2E,
                           jnp.full((A_HEADS, N_DR, lanes - rpb.shape[2]), NEG, F32)], axis=-1)
    ext = jnp.roll(ext, -(NA_COLS - 1), axis=-1)
    toep = jnp.tile(ext, (1, 1, GRID_W))[:, :, :GRID_W * (lanes - 1)].reshape(A_HEADS, N_DR, GRID_W, lanes - 1)
    toep = toep[..., :GRID_W]
    cols = np.arange(GRID_W)
    c0 = np.clip(cols - NA_COLS // 2, 0, GRID_W - NA_COLS)
    col_ok = (cols[None, :] >= c0[:, None]) & (cols[None, :] < c0[:, None] + NA_COLS)
    toep = jnp.where(col_ok[None, None], toep, NEG)
    toep = jnp.concatenate([toep, jnp.full((A_HEADS, 1, GRID_W, GRID_W), NEG, F32)], axis=1)
    return jnp.concatenate([toep, toep], axis=-1)


NA_UNROLL = 4


def _na_body(q_ref, k_ref, v_ref, c_ref, o_ref, t_ref, sa_ref, sb_ref):
    left_half = lax.broadcasted_iota(jnp.int32, (GRID_W, 2 * GRID_W), 1) < GRID_W
    for v, per_q in enumerate(_na_row_slabs()):
        for qr, slabs in enumerate(per_q):
            for pair in range(NA_WIN_ROWS // 2):
                tile = jnp.where(left_half, c_ref[int(slabs[2 * pair])], c_ref[int(slabs[2 * pair + 1])])
                t_ref[v, qr * GRID_W:(qr + 1) * GRID_W, pair * 2 * GRID_W:(pair + 1) * 2 * GRID_W] = tile

    def offsets(rb):
        ws = jnp.clip(NA_QROWS * rb - NA_ROWS // 2, 0, GRID_ROWS - NA_WIN_ROWS)
        return pl.multiple_of(rb * NA_Q, NA_Q), pl.multiple_of(ws * GRID_W, GRID_W)

    def logits(rb, s_ref, u):
        qoff, koff = offsets(rb)
        variant = jnp.where(rb < 2, rb, jnp.where(rb >= NA_BLOCKS - 2, rb - (NA_BLOCKS - 5), 2))
        s_ref[u] = lax.dot_general(q_ref[pl.ds(qoff, NA_Q), :], k_ref[pl.ds(koff, NA_KEYS), :], NT_DIMS,
                                   preferred_element_type=F32) + t_ref[variant]

    def finish(rb, s_ref, u):
        qoff, koff = offsets(rb)
        s = s_ref[u]
        m = jnp.max(s, axis=-1, keepdims=True)
        p = jnp.exp2(s - m)
        l = jnp.sum(p, axis=-1, keepdims=True)
        o = jnp.dot(p.astype(BF16), v_ref[pl.ds(koff, NA_KEYS), :], preferred_element_type=F32)
        o_ref[pl.ds(qoff, NA_Q), :] = (o / l).astype(o_ref.dtype)

    _two_stage_pipeline(NA_BLOCKS, NA_UNROLL, logits, finish, sa_ref, sb_ref)


def neighbourhood_attention(h, rpb):
    nv = len(NA_VARIANT_BLOCKS)
    blk = (SEQ, HEAD_DIM)
    return pl.pallas_call(
        _na_body,
        grid=(BATCH, A_HEADS),
        in_specs=[pl.BlockSpec(blk, lambda b, hd: (b, AB_QA + hd)),
                  pl.BlockSpec(blk, lambda b, hd: (b, AB_KA + hd)),
                  pl.BlockSpec(blk, lambda b, hd: (b, AB_VA + hd)),
                  pl.BlockSpec((None, N_DR + 1, GRID_W, 2 * GRID_W), lambda b, hd: (hd, 0, 0, 0))],
        out_specs=pl.BlockSpec(blk, lambda b, hd: (b, hd)),
        out_shape=jax.ShapeDtypeStruct((M_ROWS, A_HEADS * HEAD_DIM), BF16),
        scratch_shapes=[pltpu.VMEM((nv, NA_Q, NA_KEYS), F32),
                        pltpu.VMEM((NA_UNROLL, NA_Q, NA_KEYS), F32),
                        pltpu.VMEM((NA_UNROLL, NA_Q, NA_KEYS), F32)],
        compiler_params=_params(),
        name="neighbourhood_attention",
    )(h, h, h, _na_column_table(rpb))


B_GROUP = B_Q_HEADS // B_KV_HEADS


DENSE_TQ = 256
DENSE_TK = 256


def _dense_body(q_ref, k_ref, v_ref, o_ref):
    tq = DENSE_TQ
    q = jnp.concatenate([q_ref[:, g * HEAD_DIM:(g + 1) * HEAD_DIM] for g in range(B_GROUP)], axis=0)
    ones = jnp.ones((DENSE_TK, HEAD_DIM), BF16)
    m = acc = None
    for c in range(SEQ // DENSE_TK):
        keys = slice(c * DENSE_TK, (c + 1) * DENSE_TK)
        s = lax.dot_general(q, k_ref[keys, :], NT_DIMS, preferred_element_type=F32).astype(BF16)
        v_ones = jnp.concatenate([v_ref[keys, :], ones], axis=1)
        row_max = jnp.max(s, axis=-1, keepdims=True)
        if c == 0:
            m = row_max
            acc = jnp.dot(jnp.exp2(s - m), v_ones, preferred_element_type=F32)
        else:
            m_new = jnp.maximum(m, row_max)
            alpha = jnp.exp2(m.astype(F32) - m_new.astype(F32))
            acc = alpha * acc + jnp.dot(jnp.exp2(s - m_new), v_ones, preferred_element_type=F32)
            m = m_new
    o = acc[:, :HEAD_DIM] / acc[:, HEAD_DIM:]
    for g in range(B_GROUP):
        o_ref[:, g * HEAD_DIM:(g + 1) * HEAD_DIM] = o[g * tq:(g + 1) * tq].astype(o_ref.dtype)


def dense_gqa(h):
    tq = DENSE_TQ
    nq = SEQ // tq
    gw = B_GROUP * HEAD_DIM
    assert AB_QB % B_GROUP == 0
    q_blk0 = AB_QB // B_GROUP
    blk = (SEQ, HEAD_DIM)
    return pl.pallas_call(
        _dense_body,
        grid=(BATCH, B_KV_HEADS, nq),
        in_specs=[pl.BlockSpec((tq, gw), lambda b, kv, i: (b * nq + i, q_blk0 + kv)),
                  pl.BlockSpec(blk, lambda b, kv, i: (b, AB_KB + kv)),
                  pl.BlockSpec(blk, lambda b, kv, i: (b, AB_VB + kv))],
        out_specs=pl.BlockSpec((tq, gw), lambda b, kv, i: (b * nq + i, kv)),
        out_shape=jax.ShapeDtypeStruct((M_ROWS, B_Q_HEADS * HEAD_DIM), BF16),
        compiler_params=_params(),
        name="dense_gqa",
    )(h, h, h)


QBLK = 128
N_QBLK = SEQ // QBLK


def _band_mask_table(span):
    assert span <= QBLK
    col = np.arange(QBLK + 2 * span)[None, :]
    row = np.arange(QBLK)[:, None]
    t = np.stack([np.where(np.abs(col - row + shift) <= span, 0.0, NEG) for shift in (0, -span, -2 * span)])
    return jnp.asarray(t, F32)


def _band_window(ub, blocks, span):
    start = jnp.clip(ub * QBLK - span, 0, blocks * QBLK - (QBLK + 2 * span))
    variant = jnp.where(ub == 0, 0, jnp.where(ub == blocks - 1, 2, 1))
    return start, variant


C_GROUP = C_Q_HEADS // C_KV_HEADS


C_WIN = QBLK + 2 * C_WINDOW
C_UNROLL = 2


def _window_sink_body(sink_ref, q_ref, k_ref, v_ref, mask_ref, o_ref, sa_ref, sb_ref):
    kv = pl.program_id(1)

    def logits(i, s_ref, u):
        qoff = pl.multiple_of(i * QBLK, QBLK)
        start, _ = _band_window(i, N_QBLK, C_WINDOW)
        q = q_ref[pl.ds(qoff, QBLK), :]
        qs = jnp.concatenate([q[:, g * HEAD_DIM:(g + 1) * HEAD_DIM] for g in range(C_GROUP)], axis=0)
        s_ref[u] = lax.dot_general(qs, k_ref[pl.ds(pl.multiple_of(start, QBLK), C_WIN), :], NT_DIMS,
                                   preferred_element_type=F32)

    def finish(i, s_ref, u):
        qoff = pl.multiple_of(i * QBLK, QBLK)
        start, variant = _band_window(i, N_QBLK, C_WINDOW)
        mask = mask_ref[variant]
        ps, ls = [], []
        for g in range(C_GROUP):
            sg = s_ref[u, g * QBLK:(g + 1) * QBLK, :] + mask
            sink = sink_ref[kv * C_GROUP + g] * LOG2E
            m = jnp.maximum(jnp.max(sg, axis=-1, keepdims=True), sink)
            p = jnp.exp2(sg - m)
            ls.append(jnp.sum(p, axis=-1, keepdims=True) + jnp.exp2(sink - m))
            ps.append(p.astype(BF16))
        o = jnp.dot(jnp.concatenate(ps, axis=0), v_ref[pl.ds(pl.multiple_of(start, QBLK), C_WIN), :],
                    preferred_element_type=F32)
        for g in range(C_GROUP):
            o_ref[pl.ds(qoff, QBLK), g * HEAD_DIM:(g + 1) * HEAD_DIM] = (
                o[g * QBLK:(g + 1) * QBLK] / ls[g]).astype(o_ref.dtype)

    _two_stage_pipeline(N_QBLK, C_UNROLL, logits, finish, sa_ref, sb_ref)


def window_gqa_sink(h, sink):
    gw = C_GROUP * HEAD_DIM
    assert CD_QC % C_GROUP == 0
    q_blk0 = CD_QC // C_GROUP
    blk = (SEQ, HEAD_DIM)
    return pl.pallas_call(
        _window_sink_body,
        grid=(BATCH, C_KV_HEADS),
        in_specs=[pl.BlockSpec(memory_space=pltpu.SMEM),
                  pl.BlockSpec((SEQ, gw), lambda b, kv: (b, q_blk0 + kv)),
                  pl.BlockSpec(blk, lambda b, kv: (b, CD_KC + kv)),
                  pl.BlockSpec(blk, lambda b, kv: (b, CD_VC + kv)),
                  pl.BlockSpec((3, QBLK, C_WIN), lambda b, kv: (0, 0, 0))],
        out_specs=pl.BlockSpec((SEQ, gw), lambda b, kv: (b, kv)),
        out_shape=jax.ShapeDtypeStruct((M_ROWS, C_Q_HEADS * HEAD_DIM), BF16),
        scratch_shapes=[pltpu.VMEM((C_UNROLL, C_GROUP * QBLK, C_WIN), F32),
                        pltpu.VMEM((C_UNROLL, C_GROUP * QBLK, C_WIN), F32)],
        compiler_params=_params(),
        name="window_gqa_sink",
    )(sink.astype(F32), h, h, h, _band_mask_table(C_WINDOW))


D_DILS = tuple(d for _, d in D_DILATIONS)
D_SPAN = D_DILATIONS[0][0] // 2
assert all((w // 2) // d == D_SPAN and N_QBLK % d == 0 for w, d in D_DILATIONS) and D_DILS[0] == 1
D_WIN = QBLK + 2 * D_SPAN
D_UNROLL = 2


def _dilated_body(q0_ref, q1_ref, q2_ref, k_ref, v_ref, mask_ref, o_ref,
                  stage_ref, qc_ref, kc_ref, vc_ref, og_ref, lse_ref, sa_ref, sb_ref):
    def to_class_major(dst_ref, slot, d):
        run = SEQ // d
        for rho in range(d):
            dst_ref[slot, pl.ds(rho * run, run), :] = stage_ref[pl.ds(rho, run, stride=d), :].astype(BF16)

    for src_ref, dst_ref in ((k_ref, kc_ref), (v_ref, vc_ref)):
        stage_ref[...] = src_ref[...].astype(F32)
        for g in range(1, D_GROUPS):
            to_class_major(dst_ref, g - 1, D_DILS[g])
    for g, src_ref in ((1, q1_ref), (2, q2_ref)):
        stage_ref[...] = src_ref[...].astype(F32)
        to_class_major(qc_ref, g - 1, D_DILS[g])

    def geometry(g, i):
        d = D_DILS[g]
        run_blocks = N_QBLK // d
        rho = i // run_blocks
        ub = i % run_blocks
        start, variant = _band_window(ub, run_blocks, D_SPAN)
        koff = pl.multiple_of(rho * (run_blocks * QBLK) + start, D_SPAN)
        return pl.multiple_of(i * QBLK, QBLK), koff, variant, rho, ub

    def logits(i, s_ref, u):
        for g in range(D_GROUPS):
            qoff, koff, variant, _, _ = geometry(g, i)
            if g == 0:
                q, k = q0_ref[pl.ds(qoff, QBLK), :], k_ref[pl.ds(koff, D_WIN), :]
            else:
                q, k = qc_ref[g - 1, pl.ds(qoff, QBLK), :], kc_ref[g - 1, pl.ds(koff, D_WIN), :]
            s_ref[u * D_GROUPS + g] = lax.dot_general(q, k, NT_DIMS, preferred_element_type=F32) + mask_ref[variant]

    def finish(i, s_ref, u):
        for g in range(D_GROUPS):
            d = D_DILS[g]
            qoff, koff, _, rho, ub = geometry(g, i)
            v = v_ref[pl.ds(koff, D_WIN), :] if g == 0 else vc_ref[g - 1, pl.ds(koff, D_WIN), :]
            s = s_ref[u * D_GROUPS + g]
            m = jnp.max(s, axis=-1, keepdims=True)
            p = jnp.exp2(s - m)
            l = jnp.sum(p, axis=-1, keepdims=True)
            o = jnp.dot(p.astype(BF16), v, preferred_element_type=F32) / l
            lse = jnp.broadcast_to(m + jnp.log2(l), (QBLK, HEAD_DIM))
            if d == 1:
                rows = pl.ds(qoff, QBLK)
            else:
                rows = pl.ds(rho + d * ub * QBLK, QBLK, stride=d)
            og_ref[g, rows, :] = o
            lse_ref[g, rows, :] = lse

    _two_stage_pipeline(N_QBLK, D_UNROLL, logits, finish, sa_ref, sb_ref)

    def merge(c, carry):
        rows = pl.ds(pl.multiple_of(c * QBLK, QBLK), QBLK)
        lses = [lse_ref[g, rows, :] for g in range(D_GROUPS)]
        top = functools.reduce(jnp.maximum, lses)
        ws = [jnp.exp2(x - top) for x in lses]
        num = sum(w * og_ref[g, rows, :] for g, w in enumerate(ws))
        o_ref[rows, :] = (num / sum(ws)).astype(o_ref.dtype)
        return carry

    lax.fori_loop(0, N_QBLK, merge, 0)


def dilated_attention(h):
    blk = (SEQ, HEAD_DIM)
    return pl.pallas_call(
        _dilated_body,
        grid=(BATCH, D_SLOTS),
        in_specs=[pl.BlockSpec(blk, lambda b, s: (b, CD_QD + s)),
                  pl.BlockSpec(blk, lambda b, s: (b, CD_QD + D_SLOTS + s)),
                  pl.BlockSpec(blk, lambda b, s: (b, CD_QD + 2 * D_SLOTS + s)),
                  pl.BlockSpec(blk, lambda b, s: (b, CD_KD + s)),
                  pl.BlockSpec(blk, lambda b, s: (b, CD_VD + s)),
                  pl.BlockSpec((3, QBLK, D_WIN), lambda b, s: (0, 0, 0))],
        out_specs=pl.BlockSpec(blk, lambda b, s: (b, s)),
        out_shape=jax.ShapeDtypeStruct((M_ROWS, D_SLOTS * HEAD_DIM), BF16),
        scratch_shapes=[pltpu.VMEM((SEQ, HEAD_DIM), F32),
                        pltpu.VMEM((D_GROUPS - 1, SEQ, HEAD_DIM), BF16),
                        pltpu.VMEM((D_GROUPS - 1, SEQ, HEAD_DIM), BF16),
                        pltpu.VMEM((D_GROUPS - 1, SEQ, HEAD_DIM), BF16),
                        pltpu.VMEM((D_GROUPS, SEQ, HEAD_DIM), F32),
                        pltpu.VMEM((D_GROUPS, SEQ, HEAD_DIM), F32),
                        pltpu.VMEM((D_UNROLL * D_GROUPS, QBLK, D_WIN), F32),
                        pltpu.VMEM((D_UNROLL * D_GROUPS, QBLK, D_WIN), F32)],
        compiler_params=_params(),
        name="dilated_attention",
    )(h, h, h, h, h, _band_mask_table(D_SPAN))


def _rope_tables(kind):
    pos = np.arange(SEQ, dtype=np.float64)
    cos = np.ones((SEQ, HEAD_DIM), np.float64)
    sin = np.zeros((SEQ, HEAD_DIM), np.float64)

    def fill(start, r, p, theta):
        half = r // 2
        inv = np.exp(-math.log(theta) * np.arange(half, dtype=np.float64) * (2.0 / r))
        ang = p[:, None] * inv[None, :]
        cos[:, start:start + half] = np.cos(ang)
        cos[:, start + half:start + r] = np.cos(ang)
        sin[:, start:start + half] = -np.sin(ang)
        sin[:, start + half:start + r] = np.sin(ang)

    if kind == "axial":
        hw = HEAD_DIM // 2
        fill(0, hw, np.floor(pos / GRID_W), AXIAL_THETA)
        fill(hw, hw, pos % GRID_W, AXIAL_THETA)
        moves = AXIAL_MOVES
    else:
        fill(0, ROPE_DIMS, pos, ROPE_THETA)
        moves = PARTIAL_MOVES
    perm = _moves_to_perm(moves)
    return moves, jnp.asarray(cos[:, perm], dtype=F32), jnp.asarray(sin[:, perm], dtype=F32)


def _gain_row(parts):
    rows = []
    for g, heads, scale, perm in parts:
        if g is None:
            g = jnp.ones((HEAD_DIM,), F32)
        g = g.astype(F32) * scale
        if perm is not None:
            g = g[perm]
        rows.append(jnp.tile(g, heads))
    return jnp.concatenate(rows).reshape(1, -1)


def _layer_ab(x, xg, ssq, w_in, w_out, li, g_next, a_qn, a_kn, a_rpb, b_qn, b_kn, axial):
    perm = _moves_to_perm(axial[0])
    gains = _gain_row([(a_qn, A_HEADS, SCALE, None), (a_kn, A_HEADS, 1.0, None), (None, A_HEADS, 1.0, None),
                       (b_qn, B_Q_HEADS, SCALE, perm), (b_kn, B_KV_HEADS, 1.0, perm),
                       (None, B_KV_HEADS, 1.0, None)])
    h = in_proj(xg, ssq, w_in, li, AB_KINDS, gains, axial)
    oa = neighbourhood_attention(h, a_rpb)
    ob = dense_gqa(h)
    return matmul_residual([oa, ob], w_out, li, x, g_next)


def _layer_cd(x, xg, ssq, w_in, w_out, li, g_next, c_qn, c_kn, c_sink, d_qn, d_kn, partial):
    perm = _moves_to_perm(partial[0])
    gains = _gain_row([(c_qn, C_Q_HEADS, SCALE, perm), (c_kn, C_KV_HEADS, 1.0, perm), (None, C_KV_HEADS, 1.0, None),
                       (d_qn, D_Q_HEADS, SCALE, perm), (d_kn, D_SLOTS, 1.0, perm), (None, D_SLOTS, 1.0, None)])
    h = in_proj(xg, ssq, w_in, li, CD_KINDS, gains, partial)
    oc = window_gqa_sink(h, c_sink)
    od = dilated_attention(h)
    return matmul_residual([oc, od], w_out, li, x, g_next)


def kernel(x, attn_norm, ffn_norm, ab_w_in, ab_w_out, a_q_norm, a_k_norm, a_rpb, b_q_norm, b_k_norm,
           cd_w_in, cd_w_out, c_q_norm, c_k_norm, c_sink, d_q_norm, d_k_norm, w_gate, w_up, w_down):
    axial = _rope_tables("axial")
    partial = _rope_tables("partial")
    x = x.reshape(M_ROWS, D_MODEL)
    xg, ssq = prenorm(x, attn_norm[0])
    for layer in range(DEPTH):
        j = layer // 2
        if layer % 2 == 0:
            x, xg, ssq = _layer_ab(x, xg, ssq, ab_w_in, ab_w_out, j, ffn_norm[layer], a_q_norm[j], a_k_norm[j],
                                   a_rpb[j], b_q_norm[j], b_k_norm[j], axial)
        else:
            x, xg, ssq = _layer_cd(x, xg, ssq, cd_w_in, cd_w_out, j, ffn_norm[layer], c_q_norm[j], c_k_norm[j],
                                   c_sink[j], d_q_norm[j], d_k_norm[j], partial)
        hidden = gate_up(xg, ssq, w_gate, w_up, layer)
        if layer + 1 < DEPTH:
            x, xg, ssq = matmul_residual([hidden], w_down, layer, x, attn_norm[layer + 1])
        else:
            x = matmul_residual([hidden], w_down, layer, x)
    return x.reshape(BATCH, SEQ, D_MODEL)
```

```python
import functools
import math

import numpy as np
import jax
import jax.numpy as jnp
from jax import lax
from jax.experimental import pallas as pl
from jax.experimental.pallas import tpu as pltpu

D_MODEL = 2048
BATCH = 2
SEQ = 4096
DEPTH = 4
HEAD_DIM = 128
GRID_W = 64
GRID_ROWS = SEQ // GRID_W
EPS = 1e-6
A_HEADS = 8
NA_ROWS = 8
NA_COLS = 16
B_Q_HEADS = 8
B_KV_HEADS = 2
AXIAL_THETA = 10000.0
C_Q_HEADS = 12
C_KV_HEADS = 4
C_WINDOW = 128
D_DILATIONS = ((128, 1), (512, 4), (2048, 16))
D_GROUPS = len(D_DILATIONS)
D_SLOTS = 4
D_Q_HEADS = D_GROUPS * D_SLOTS
ROPE_THETA = 500000.0
ROPE_DIMS = HEAD_DIM // 4
FFN_HIDDEN = ((-(-8 * D_MODEL // 3)) + 255) // 256 * 256
M_ROWS = BATCH * SEQ
LOG2E = math.log2(math.e)
SCALE = HEAD_DIM ** -0.5 * LOG2E
NEG = -1e30

VMEM_LIMIT_BYTES = 52 * 1024 * 1024
MXU_N = 256

F32 = jnp.float32
BF16 = jnp.bfloat16
NT_DIMS = (((1,), (1,)), ((), ()))


def _params(**kw):
    return pltpu.CompilerParams(vmem_limit_bytes=VMEM_LIMIT_BYTES, **kw)


LANES = 128


def _lane_partial_sums(sq):
    out = sq[:, :LANES]
    for i in range(1, sq.shape[1] // LANES):
        out = out + sq[:, i * LANES:(i + 1) * LANES]
    return out


ROW_SCALE_CHUNK = 256


def _store_row_scales(ssq_ref, inv_ref, row0, n_rows, dim):
    for c in range(n_rows // ROW_SCALE_CHUNK):
        rows = slice(c * ROW_SCALE_CHUNK, (c + 1) * ROW_SCALE_CHUNK)
        part = ssq_ref[0, rows, :]
        for t in range(1, ssq_ref.shape[0]):
            part = part + ssq_ref[t, rows, :]
        inv = lax.rsqrt(jnp.sum(part, axis=-1, keepdims=True) * (1.0 / dim) + EPS)
        inv_ref[pl.ds(row0 + c * ROW_SCALE_CHUNK, ROW_SCALE_CHUNK), :] = jnp.broadcast_to(
            inv, (ROW_SCALE_CHUNK, LANES))


def _ssq_spec(ssq, tm):
    return pl.BlockSpec((ssq.shape[0], tm, LANES), lambda j, i: (0, jnp.where(j == 0, i, 0), 0))


def _prenorm_body(x_ref, g_ref, xg_ref, ssq_ref):
    x = x_ref[...]
    xg_ref[...] = (x * g_ref[...]).astype(xg_ref.dtype)
    ssq_ref[...] = _lane_partial_sums(x * x)


def prenorm(x, g, tm=512):
    m, d = x.shape
    return pl.pallas_call(
        _prenorm_body,
        grid=(m // tm,),
        in_specs=[pl.BlockSpec((tm, d), lambda i: (i, 0)), pl.BlockSpec((1, d), lambda i: (0, 0))],
        out_specs=[pl.BlockSpec((tm, d), lambda i: (i, 0)), pl.BlockSpec((None, tm, LANES), lambda i: (0, i, 0))],
        out_shape=[jax.ShapeDtypeStruct((m, d), BF16), jax.ShapeDtypeStruct((1, m, LANES), F32)],
        compiler_params=_params(),
        name="prenorm",
    )(x, g.reshape(1, d).astype(F32))


HALF_LANES = HEAD_DIM // 2
AXIAL_MOVES = ((96, 32, 64), (32, 64, 96))
PARTIAL_MOVES = ((112, 16, 64), (48, 64, 80))


def _moves_to_perm(moves):
    perm = np.arange(HEAD_DIM)
    for shift, lo, hi in moves:
        perm[lo:hi] = (np.arange(lo, hi) - shift) % HEAD_DIM
    assert sorted(perm.tolist()) == list(range(HEAD_DIM))
    return perm


def _permute_lanes(w, moves):
    lane = lax.broadcasted_iota(jnp.int32, w.shape, 1)
    out = w
    for shift, lo, hi in moves:
        out = jnp.where((lane >= lo) & (lane < hi), pltpu.roll(w, shift, 1), out)
    return out


PROJ_TM = 2048
PROJ_TN = 512
PROJ_SUB = 256


def _in_proj_body(sig_ref, x_ref, ssq_ref, w_ref, g_ref, cos_ref, sin_ref, o_ref, wbf_ref, inv_ref, *,
                  sigs, moves):
    n = pl.program_id(0)
    m = pl.program_id(1)
    heads = PROJ_TN // HEAD_DIM
    row0 = pl.multiple_of(m * PROJ_TM, PROJ_TM)

    @pl.when(n == 0)
    def _():
        _store_row_scales(ssq_ref, inv_ref, row0, PROJ_TM, x_ref.shape[1])

    def cast_weights(kinds):
        for h in range(heads):
            cols = slice(h * HEAD_DIM, (h + 1) * HEAD_DIM)
            w = w_ref[:, cols]
            if kinds[h] == "rope":
                w = _permute_lanes(w, moves)
            wbf_ref[:, cols] = w.astype(BF16)

    def compute(kinds):
        for r in range(PROJ_TM // PROJ_SUB):
            rows = slice(r * PROJ_SUB, (r + 1) * PROJ_SUB)
            x = x_ref[rows, :]
            inv = inv_ref[pl.ds(row0 + r * PROJ_SUB, PROJ_SUB), :]
            for p in range(PROJ_TN // MXU_N):
                acc = jnp.dot(x, wbf_ref[:, p * MXU_N:(p + 1) * MXU_N], preferred_element_type=F32)
                for hh in range(MXU_N // HEAD_DIM):
                    h = p * (MXU_N // HEAD_DIM) + hh
                    cols = slice(h * HEAD_DIM, (h + 1) * HEAD_DIM)
                    y = acc[:, hh * HEAD_DIM:(hh + 1) * HEAD_DIM]
                    if kinds[h] == "plain":
                        y = y * inv
                    else:
                        ms = jnp.mean(y * y, axis=-1, keepdims=True)
                        y = y * (inv * lax.rsqrt(ms * (inv * inv) + EPS)) * g_ref[:, cols]
                    if kinds[h] == "rope":
                        y = y * cos_ref[rows, :] + pltpu.roll(y, HALF_LANES, 1) * sin_ref[rows, :]
                    o_ref[rows, cols] = y.astype(o_ref.dtype)

    for sid, kinds in enumerate(sigs):
        @pl.when(sig_ref[n] == sid)
        def _(kinds=kinds):
            @pl.when(m == 0)
            def _():
                cast_weights(kinds)

            compute(kinds)


def in_proj(xg, ssq, w, li, kinds, gains, rope):
    m, k = xg.shape
    n = w.shape[2]
    heads = PROJ_TN // HEAD_DIM
    tiles = [tuple(kinds[i:i + heads]) for i in range(0, len(kinds), heads)]
    sigs = tuple(dict.fromkeys(tiles))
    sig_ids = jnp.asarray([sigs.index(t) for t in tiles], jnp.int32)
    moves, cos, sin = rope
    pos_blocks = SEQ // PROJ_TM
    return pl.pallas_call(
        functools.partial(_in_proj_body, sigs=sigs, moves=moves),
        grid=(n // PROJ_TN, m // PROJ_TM),
        in_specs=[pl.BlockSpec(memory_space=pltpu.SMEM),
                  pl.BlockSpec((PROJ_TM, k), lambda j, i: (i, 0)),
                  _ssq_spec(ssq, PROJ_TM),
                  pl.BlockSpec((None, k, PROJ_TN), lambda j, i: (li, 0, j)),
                  pl.BlockSpec((1, PROJ_TN), lambda j, i: (0, j)),
                  pl.BlockSpec((PROJ_TM, HEAD_DIM), lambda j, i: (i % pos_blocks, 0)),
                  pl.BlockSpec((PROJ_TM, HEAD_DIM), lambda j, i: (i % pos_blocks, 0))],
        out_specs=pl.BlockSpec((PROJ_TM, PROJ_TN), lambda j, i: (i, j)),
        out_shape=jax.ShapeDtypeStruct((m, n), BF16),
        scratch_shapes=[pltpu.VMEM((k, PROJ_TN), BF16), pltpu.VMEM((m, LANES), F32)],
        compiler_params=_params(),
        name="in_proj",
    )(sig_ids, xg, ssq, w, gains, cos, sin)


RES_SUB = 256
RES_WEIGHT_TILE_BYTES = 12 * 1024 * 1024
RES_WEIGHT_FULL_BYTES = 16 * 1024 * 1024


def _res_tiles(kt, n):
    if kt * n * 4 <= RES_WEIGHT_FULL_BYTES:
        return 512, n, True
    tn = n
    while kt * tn * 4 > RES_WEIGHT_TILE_BYTES and tn % (2 * MXU_N) == 0:
        tn //= 2
    return 512, tn, False


def _mm_res_body(*refs, n_in, tm, tn, emit_norm):
    a_refs = refs[:n_in]
    if emit_norm:
        w_ref, x_ref, g_ref, o_ref, xg_ref, ssq_ref, wbf_ref = refs[n_in:]
    else:
        w_ref, x_ref, o_ref, wbf_ref = refs[n_in:]

    @pl.when(pl.program_id(1) == 0)
    def _():
        wbf_ref[...] = w_ref[...].astype(BF16)

    for r in range(tm // RES_SUB):
        rows = slice(r * RES_SUB, (r + 1) * RES_SUB)
        a = a_refs[0][rows, :] if n_in == 1 else jnp.concatenate([a_ref[rows, :] for a_ref in a_refs], axis=1)
        part = None
        for p in range(tn // MXU_N):
            cols = slice(p * MXU_N, (p + 1) * MXU_N)
            acc = x_ref[rows, cols] + jnp.dot(a, wbf_ref[:, cols], preferred_element_type=F32)
            o_ref[rows, cols] = acc
            if emit_norm:
                xg_ref[rows, cols] = (acc * g_ref[:, cols]).astype(xg_ref.dtype)
                sq = _lane_partial_sums(acc * acc)
                part = sq if part is None else part + sq
        if emit_norm:
            ssq_ref[rows, :] = part


def matmul_residual(a_list, w, li, x, g_next=None):
    m, n = x.shape
    widths = tuple(a.shape[1] for a in a_list)
    kt = sum(widths)
    assert kt == w.shape[1]
    tm, tn, whole = _res_tiles(kt, n)
    emit_norm = g_next is not None
    tile = pl.BlockSpec((tm, tn), lambda j, i: (i, j))
    in_specs = [pl.BlockSpec((tm, ka), lambda j, i: (i, 0)) for ka in widths]
    w_mode = dict(pipeline_mode=pl.Buffered(1)) if whole else {}
    in_specs += [pl.BlockSpec((None, kt, tn), lambda j, i: (li, 0, j), **w_mode), tile]
    args = [*a_list, w, x]
    out_specs, out_shape = tile, jax.ShapeDtypeStruct((m, n), F32)
    if emit_norm:
        in_specs.append(pl.BlockSpec((1, tn), lambda j, i: (0, j)))
        args.append(g_next.reshape(1, n).astype(F32))
        out_specs = [tile, tile, pl.BlockSpec((None, tm, LANES), lambda j, i: (j, i, 0))]
        out_shape = [out_shape, jax.ShapeDtypeStruct((m, n), BF16), jax.ShapeDtypeStruct((n // tn, m, LANES), F32)]
    return pl.pallas_call(
        functools.partial(_mm_res_body, n_in=len(a_list), tm=tm, tn=tn, emit_norm=emit_norm),
        grid=(n // tn, m // tm),
        in_specs=in_specs,
        out_specs=out_specs,
        out_shape=out_shape,
        scratch_shapes=[pltpu.VMEM((kt, tn), BF16)],
        compiler_params=_params(),
        name="matmul_residual",
    )(*args)


FFN_TM = 2048
FFN_TN = 512
FFN_SUB = 256


def _gate_up_body(x_ref, ssq_ref, wg_ref, wu_ref, o_ref, wgb_ref, wub_ref, inv_ref):
    row0 = pl.multiple_of(pl.program_id(1) * FFN_TM, FFN_TM)

    @pl.when(pl.program_id(1) == 0)
    def _():
        wgb_ref[...] = wg_ref[...].astype(BF16)
        wub_ref[...] = wu_ref[...].astype(BF16)

    @pl.when(pl.program_id(0) == 0)
    def _():
        _store_row_scales(ssq_ref, inv_ref, row0, FFN_TM, x_ref.shape[1])

    for r in range(FFN_TM // FFN_SUB):
        rows = slice(r * FFN_SUB, (r + 1) * FFN_SUB)
        x = x_ref[rows, :]
        inv = jnp.concatenate([inv_ref[pl.ds(row0 + r * FFN_SUB, FFN_SUB), :]] * (MXU_N // LANES), axis=1)
        for p in range(FFN_TN // MXU_N):
            cols = slice(p * MXU_N, (p + 1) * MXU_N)
            g = jnp.dot(x, wgb_ref[:, cols], preferred_element_type=F32) * inv
            u = jnp.dot(x, wub_ref[:, cols], preferred_element_type=F32) * inv
            o_ref[rows, cols] = (g * (1.0 / (1.0 + jnp.exp(-g))) * u).astype(o_ref.dtype)


def gate_up(xg, ssq, wg, wu, li):
    m, k = xg.shape
    n = wg.shape[2]
    return pl.pallas_call(
        _gate_up_body,
        grid=(n // FFN_TN, m // FFN_TM),
        in_specs=[pl.BlockSpec((FFN_TM, k), lambda j, i: (i, 0)),
                  _ssq_spec(ssq, FFN_TM),
                  pl.BlockSpec((None, k, FFN_TN), lambda j, i: (li, 0, j)),
                  pl.BlockSpec((None, k, FFN_TN), lambda j, i: (li, 0, j))],
        out_specs=pl.BlockSpec((FFN_TM, FFN_TN), lambda j, i: (i, j)),
        out_shape=jax.ShapeDtypeStruct((m, n), BF16),
        scratch_shapes=[pltpu.VMEM((k, FFN_TN), BF16), pltpu.VMEM((k, FFN_TN), BF16),
                        pltpu.VMEM((m, LANES), F32)],
        compiler_params=_params(),
        name="gate_up",
    )(xg, ssq, wg, wu)


def _two_stage_pipeline(n_blocks, per_stage, logits, finish, buf_a, buf_b):
    n_stages = n_blocks // per_stage
    assert n_stages * per_stage == n_blocks and n_stages % 2 == 0 and n_stages >= 2

    def stage(first, cur, nxt, with_next=True):
        for u in range(per_stage):
            finish(first + u, cur, u)
            if with_next:
                logits(first + per_stage + u, nxt, u)

    for u in range(per_stage):
        logits(u, buf_a, u)

    def pair(j, carry):
        first = 2 * j * per_stage
        stage(first, buf_a, buf_b)
        stage(first + per_stage, buf_b, buf_a)
        return carry

    lax.fori_loop(0, n_stages // 2 - 1, pair, 0)
    last = (n_stages - 2) * per_stage
    stage(last, buf_a, buf_b)
    stage(last + per_stage, buf_b, buf_a, with_next=False)


AB_QA, AB_KA, AB_VA = 0, A_HEADS, 2 * A_HEADS
AB_QB = 3 * A_HEADS
AB_KB = AB_QB + B_Q_HEADS
AB_VB = AB_KB + B_KV_HEADS
AB_KINDS = ("norm",) * (2 * A_HEADS) + ("plain",) * A_HEADS + ("rope",) * (B_Q_HEADS + B_KV_HEADS) \
    + ("plain",) * B_KV_HEADS
CD_QC = 0
CD_KC = C_Q_HEADS
CD_VC = CD_KC + C_KV_HEADS
CD_QD = CD_VC + C_KV_HEADS
CD_KD = CD_QD + D_Q_HEADS
CD_VD = CD_KD + D_SLOTS
CD_KINDS = ("rope",) * (C_Q_HEADS + C_KV_HEADS) + ("plain",) * C_KV_HEADS \
    + ("rope",) * (D_Q_HEADS + D_SLOTS) + ("plain",) * D_SLOTS


NA_QROWS = 2
NA_WIN_ROWS = 10
NA_Q = NA_QROWS * GRID_W
NA_KEYS = NA_WIN_ROWS * GRID_W
NA_BLOCKS = GRID_ROWS // NA_QROWS
NA_VARIANT_BLOCKS = (0, 1, 2, NA_BLOCKS - 2, NA_BLOCKS - 1)


def _na_win_start(rb):
    return np.clip(NA_QROWS * rb - NA_ROWS // 2, 0, GRID_ROWS - NA_WIN_ROWS)


N_DR = 2 * NA_ROWS - 1


def _na_row_slabs():
    dr_idx = np.full((len(NA_VARIANT_BLOCKS), NA_QROWS, NA_WIN_ROWS), N_DR, np.int32)
    for v, rb in enumerate(NA_VARIANT_BLOCKS):
        ws = _na_win_start(rb)
        for qr in range(NA_QROWS):
            qrow = NA_QROWS * rb + qr
            r0 = np.clip(qrow - NA_ROWS // 2, 0, GRID_ROWS - NA_ROWS)
            for kr in range(NA_WIN_ROWS):
                krow = ws + kr
                if r0 <= krow < r0 + NA_ROWS:
                    dr_idx[v, qr, kr] = krow - qrow + NA_ROWS - 1
    return dr_idx


def _na_column_table(rpb):
    assert rpb.shape == (A_HEADS, N_DR, 2 * NA_COLS - 1)
    lanes = 2 * GRID_W
    ext = jnp.concatenate([rpb.astype(F32) * LOG2E,
                           jnp.full((A_HEADS, N_DR, lanes - rpb.shape[2]), NEG, F32)], axis=-1)
    ext = jnp.roll(ext, -(NA_COLS - 1), axis=-1)
    toep = jnp.tile(ext, (1, 1, GRID_W))[:, :, :GRID_W * (lanes - 1)].reshape(A_HEADS, N_DR, GRID_W, lanes - 1)
    toep = toep[..., :GRID_W]
    cols = np.arange(GRID_W)
    c0 = np.clip(cols - NA_COLS // 2, 0, GRID_W - NA_COLS)
    col_ok = (cols[None, :] >= c0[:, None]) & (cols[None, :] < c0[:, None] + NA_COLS)
    toep = jnp.where(col_ok[None, None], toep, NEG)
    toep = jnp.concatenate([toep, jnp.full((A_HEADS, 1, GRID_W, GRID_W), NEG, F32)], axis=1)
    return jnp.concatenate([toep, toep], axis=-1)


NA_UNROLL = 4


def _na_body(q_ref, k_ref, v_ref, c_ref, o_ref, t_ref, sa_ref, sb_ref):
    left_half = lax.broadcasted_iota(jnp.int32, (GRID_W, 2 * GRID_W), 1) < GRID_W
    for v, per_q in enumerate(_na_row_slabs()):
        for qr, slabs in enumerate(per_q):
            for pair in range(NA_WIN_ROWS // 2):
                tile = jnp.where(left_half, c_ref[int(slabs[2 * pair])], c_ref[int(slabs[2 * pair + 1])])
                t_ref[v, qr * GRID_W:(qr + 1) * GRID_W, pair * 2 * GRID_W:(pair + 1) * 2 * GRID_W] = tile

    def offsets(rb):
        ws = jnp.clip(NA_QROWS * rb - NA_ROWS // 2, 0, GRID_ROWS - NA_WIN_ROWS)
        return pl.multiple_of(rb * NA_Q, NA_Q), pl.multiple_of(ws * GRID_W, GRID_W)

    def logits(rb, s_ref, u):
        qoff, koff = offsets(rb)
        variant = jnp.where(rb < 2, rb, jnp.where(rb >= NA_BLOCKS - 2, rb - (NA_BLOCKS - 5), 2))
        s_ref[u] = lax.dot_general(q_ref[pl.ds(qoff, NA_Q), :], k_ref[pl.ds(koff, NA_KEYS), :], NT_DIMS,
                                   preferred_element_type=F32) + t_ref[variant]

    def finish(rb, s_ref, u):
        qoff, koff = offsets(rb)
        s = s_ref[u]
        m = jnp.max(s, axis=-1, keepdims=True)
        p = jnp.exp2(s - m)
        l = jnp.sum(p, axis=-1, keepdims=True)
        o = jnp.dot(p.astype(BF16), v_ref[pl.ds(koff, NA_KEYS), :], preferred_element_type=F32)
        o_ref[pl.ds(qoff, NA_Q), :] = (o / l).astype(o_ref.dtype)

    _two_stage_pipeline(NA_BLOCKS, NA_UNROLL, logits, finish, sa_ref, sb_ref)


def neighbourhood_attention(h, rpb):
    nv = len(NA_VARIANT_BLOCKS)
    blk = (SEQ, HEAD_DIM)
    return pl.pallas_call(
        _na_body,
        grid=(BATCH, A_HEADS),
        in_specs=[pl.BlockSpec(blk, lambda b, hd: (b, AB_QA + hd)),
                  pl.BlockSpec(blk, lambda b, hd: (b, AB_KA + hd)),
                  pl.BlockSpec(blk, lambda b, hd: (b, AB_VA + hd)),
                  pl.BlockSpec((None, N_DR + 1, GRID_W, 2 * GRID_W), lambda b, hd: (hd, 0, 0, 0))],
        out_specs=pl.BlockSpec(blk, lambda b, hd: (b, hd)),
        out_shape=jax.ShapeDtypeStruct((M_ROWS, A_HEADS * HEAD_DIM), BF16),
        scratch_shapes=[pltpu.VMEM((nv, NA_Q, NA_KEYS), F32),
                        pltpu.VMEM((NA_UNROLL, NA_Q, NA_KEYS), F32),
                        pltpu.VMEM((NA_UNROLL, NA_Q, NA_KEYS), F32)],
        compiler_params=_params(),
        name="neighbourhood_attention",
    )(h, h, h, _na_column_table(rpb))


B_GROUP = B_Q_HEADS // B_KV_HEADS


DENSE_TQ = 512
DENSE_TK = 256


def _dense_body(q_ref, k_ref, v_ref, o_ref):
    tq = DENSE_TQ
    q = jnp.concatenate([q_ref[:, g * HEAD_DIM:(g + 1) * HEAD_DIM] for g in range(B_GROUP)], axis=0)
    ones = jnp.ones((DENSE_TK, HEAD_DIM), BF16)
    m = acc = None
    for c in range(SEQ // DENSE_TK):
        keys = slice(c * DENSE_TK, (c + 1) * DENSE_TK)
        s = lax.dot_general(q, k_ref[keys, :], NT_DIMS, preferred_element_type=F32).astype(BF16)
        v_ones = jnp.concatenate([v_ref[keys, :], ones], axis=1)
        row_max = jnp.max(s, axis=-1, keepdims=True)
        if c == 0:
            m = row_max
            acc = jnp.dot(jnp.exp2(s - m), v_ones, preferred_element_type=F32)
        else:
            m_new = jnp.maximum(m, row_max)
            alpha = jnp.exp2(m.astype(F32) - m_new.astype(F32))
            acc = alpha * acc + jnp.dot(jnp.exp2(s - m_new), v_ones, preferred_element_type=F32)
            m = m_new
    o = acc[:, :HEAD_DIM] / acc[:, HEAD_DIM:]
    for g in range(B_GROUP):
        o_ref[:, g * HEAD_DIM:(g + 1) * HEAD_DIM] = o[g * tq:(g + 1) * tq].astype(o_ref.dtype)


def dense_gqa(h):
    tq = DENSE_TQ
    nq = SEQ // tq
    gw = B_GROUP * HEAD_DIM
    assert AB_QB % B_GROUP == 0
    q_blk0 = AB_QB // B_GROUP
    blk = (SEQ, HEAD_DIM)
    return pl.pallas_call(
        _dense_body,
        grid=(BATCH, B_KV_HEADS, nq),
        in_specs=[pl.BlockSpec((tq, gw), lambda b, kv, i: (b * nq + i, q_blk0 + kv)),
                  pl.BlockSpec(blk, lambda b, kv, i: (b, AB_KB + kv)),
                  pl.BlockSpec(blk, lambda b, kv, i: (b, AB_VB + kv))],
        out_specs=pl.BlockSpec((tq, gw), lambda b, kv, i: (b * nq + i, kv)),
        out_shape=jax.ShapeDtypeStruct((M_ROWS, B_Q_HEADS * HEAD_DIM), BF16),
        compiler_params=_params(),
        name="dense_gqa",
    )(h, h, h)


QBLK = 128
N_QBLK = SEQ // QBLK


def _band_mask_table(span):
    assert span <= QBLK
    col = np.arange(QBLK + 2 * span)[None, :]
    row = np.arange(QBLK)[:, None]
    t = np.stack([np.where(np.abs(col - row + shift) <= span, 0.0, NEG) for shift in (0, -span, -2 * span)])
    return jnp.asarray(t, F32)


def _band_window(ub, blocks, span):
    start = jnp.clip(ub * QBLK - span, 0, blocks * QBLK - (QBLK + 2 * span))
    variant = jnp.where(ub == 0, 0, jnp.where(ub == blocks - 1, 2, 1))
    return start, variant


C_GROUP = C_Q_HEADS // C_KV_HEADS


C_WIN = QBLK + 2 * C_WINDOW
C_UNROLL = 2


def _window_sink_body(sink_ref, q_ref, k_ref, v_ref, mask_ref, o_ref, sa_ref, sb_ref):
    kv = pl.program_id(1)

    def logits(i, s_ref, u):
        qoff = pl.multiple_of(i * QBLK, QBLK)
        start, _ = _band_window(i, N_QBLK, C_WINDOW)
        q = q_ref[pl.ds(qoff, QBLK), :]
        qs = jnp.concatenate([q[:, g * HEAD_DIM:(g + 1) * HEAD_DIM] for g in range(C_GROUP)], axis=0)
        s_ref[u] = lax.dot_general(qs, k_ref[pl.ds(pl.multiple_of(start, QBLK), C_WIN), :], NT_DIMS,
                                   preferred_element_type=F32)

    def finish(i, s_ref, u):
        qoff = pl.multiple_of(i * QBLK, QBLK)
        start, variant = _band_window(i, N_QBLK, C_WINDOW)
        mask = mask_ref[variant]
        ps, ls = [], []
        for g in range(C_GROUP):
            sg = s_ref[u, g * QBLK:(g + 1) * QBLK, :] + mask
            sink = sink_ref[kv * C_GROUP + g] * LOG2E
            m = jnp.maximum(jnp.max(sg, axis=-1, keepdims=True), sink)
            p = jnp.exp2(sg - m)
            ls.append(jnp.sum(p, axis=-1, keepdims=True) + jnp.exp2(sink - m))
            ps.append(p.astype(BF16))
        o = jnp.dot(jnp.concatenate(ps, axis=0), v_ref[pl.ds(pl.multiple_of(start, QBLK), C_WIN), :],
                    preferred_element_type=F32)
        for g in range(C_GROUP):
            o_ref[pl.ds(qoff, QBLK), g * HEAD_DIM:(g + 1) * HEAD_DIM] = (
                o[g * QBLK:(g + 1) * QBLK] / ls[g]).astype(o_ref.dtype)

    _two_stage_pipeline(N_QBLK, C_UNROLL, logits, finish, sa_ref, sb_ref)


def window_gqa_sink(h, sink):
    gw = C_GROUP * HEAD_DIM
    assert CD_QC % C_GROUP == 0
    q_blk0 = CD_QC // C_GROUP
    blk = (SEQ, HEAD_DIM)
    return pl.pallas_call(
        _window_sink_body,
        grid=(BATCH, C_KV_HEADS),
        in_specs=[pl.BlockSpec(memory_space=pltpu.SMEM),
                  pl.BlockSpec((SEQ, gw), lambda b, kv: (b, q_blk0 + kv)),
                  pl.BlockSpec(blk, lambda b, kv: (b, CD_KC + kv)),
                  pl.BlockSpec(blk, lambda b, kv: (b, CD_VC + kv)),
                  pl.BlockSpec((3, QBLK, C_WIN), lambda b, kv: (0, 0, 0))],
        out_specs=pl.BlockSpec((SEQ, gw), lambda b, kv: (b, kv)),
        out_shape=jax.ShapeDtypeStruct((M_ROWS, C_Q_HEADS * HEAD_DIM), BF16),
        scratch_shapes=[pltpu.VMEM((C_UNROLL, C_GROUP * QBLK, C_WIN), F32),
                        pltpu.VMEM((C_UNROLL, C_GROUP * QBLK, C_WIN), F32)],
        compiler_params=_params(),
        name="window_gqa_sink",
    )(sink.astype(F32), h, h, h, _band_mask_table(C_WINDOW))


D_DILS = tuple(d for _, d in D_DILATIONS)
D_SPAN = D_DILATIONS[0][0] // 2
assert all((w // 2) // d == D_SPAN and N_QBLK % d == 0 for w, d in D_DILATIONS) and D_DILS[0] == 1
D_WIN = QBLK + 2 * D_SPAN
D_UNROLL = 2


def _dilated_body(q0_ref, q1_ref, q2_ref, k_ref, v_ref, mask_ref, o_ref,
                  stage_ref, qc_ref, kc_ref, vc_ref, og_ref, lse_ref, sa_ref, sb_ref):
    def to_class_major(dst_ref, slot, d):
        run = SEQ // d
        for rho in range(d):
            dst_ref[slot, pl.ds(rho * run, run), :] = stage_ref[pl.ds(rho, run, stride=d), :].astype(BF16)

    for src_ref, dst_ref in ((k_ref, kc_ref), (v_ref, vc_ref)):
        stage_ref[...] = src_ref[...].astype(F32)
        for g in range(1, D_GROUPS):
            to_class_major(dst_ref, g - 1, D_DILS[g])
    for g, src_ref in ((1, q1_ref), (2, q2_ref)):
        stage_ref[...] = src_ref[...].astype(F32)
        to_class_major(qc_ref, g - 1, D_DILS[g])

    def geometry(g, i):
        d = D_DILS[g]
        run_blocks = N_QBLK // d
        rho = i // run_blocks
        ub = i % run_blocks
        start, variant = _band_window(ub, run_blocks, D_SPAN)
        koff = pl.multiple_of(rho * (run_blocks * QBLK) + start, D_SPAN)
        return pl.multiple_of(i * QBLK, QBLK), koff, variant, rho, ub

    def logits(i, s_ref, u):
        for g in range(D_GROUPS):
            qoff, koff, variant, _, _ = geometry(g, i)
            if g == 0:
                q, k = q0_ref[pl.ds(qoff, QBLK), :], k_ref[pl.ds(koff, D_WIN), :]
            else:
                q, k = qc_ref[g - 1, pl.ds(qoff, QBLK), :], kc_ref[g - 1, pl.ds(koff, D_WIN), :]
            s_ref[u * D_GROUPS + g] = lax.dot_general(q, k, NT_DIMS, preferred_element_type=F32) + mask_ref[variant]

    def finish(i, s_ref, u):
        for g in range(D_GROUPS):
            d = D_DILS[g]
            qoff, koff, _, rho, ub = geometry(g, i)
            v = v_ref[pl.ds(koff, D_WIN), :] if g == 0 else vc_ref[g - 1, pl.ds(koff, D_WIN), :]
            s = s_ref[u * D_GROUPS + g]
            m = jnp.max(s, axis=-1, keepdims=True)
            p = jnp.exp2(s - m)
            l = jnp.sum(p, axis=-1, keepdims=True)
            o = jnp.dot(p.astype(BF16), v, preferred_element_type=F32) / l
            lse = jnp.broadcast_to(m + jnp.log2(l), (QBLK, HEAD_DIM))
            if d == 1:
                rows = pl.ds(qoff, QBLK)
            else:
                rows = pl.ds(rho + d * ub * QBLK, QBLK, stride=d)
            og_ref[g, rows, :] = o
            lse_ref[g, rows, :] = lse

    _two_stage_pipeline(N_QBLK, D_UNROLL, logits, finish, sa_ref, sb_ref)

    def merge(c, carry):
        rows = pl.ds(pl.multiple_of(c * QBLK, QBLK), QBLK)
        lses = [lse_ref[g, rows, :] for g in range(D_GROUPS)]
        top = functools.reduce(jnp.maximum, lses)
        ws = [jnp.exp2(x - top) for x in lses]
        num = sum(w * og_ref[g, rows, :] for g, w in enumerate(ws))
        o_ref[rows, :] = (num / sum(ws)).astype(o_ref.dtype)
        return carry

    lax.fori_loop(0, N_QBLK, merge, 0)


def dilated_attention(h):
    blk = (SEQ, HEAD_DIM)
    return pl.pallas_call(
        _dilated_body,
        grid=(BATCH, D_SLOTS),
        in_specs=[pl.BlockSpec(blk, lambda b, s: (b, CD_QD + s)),
                  pl.BlockSpec(blk, lambda b, s: (b, CD_QD + D_SLOTS + s)),
                  pl.BlockSpec(blk, lambda b, s: (b, CD_QD + 2 * D_SLOTS + s)),
                  pl.BlockSpec(blk, lambda b, s: (b, CD_KD + s)),
                  pl.BlockSpec(blk, lambda b, s: (b, CD_VD + s)),
                  pl.BlockSpec((3, QBLK, D_WIN), lambda b, s: (0, 0, 0))],
        out_specs=pl.BlockSpec(blk, lambda b, s: (b, s)),
        out_shape=jax.ShapeDtypeStruct((M_ROWS, D_SLOTS * HEAD_DIM), BF16),
        scratch_shapes=[pltpu.VMEM((SEQ, HEAD_DIM), F32),
                        pltpu.VMEM((D_GROUPS - 1, SEQ, HEAD_DIM), BF16),
                        pltpu.VMEM((D_GROUPS - 1, SEQ, HEAD_DIM), BF16),
                        pltpu.VMEM((D_GROUPS - 1, SEQ, HEAD_DIM), BF16),
                        pltpu.VMEM((D_GROUPS, SEQ, HEAD_DIM), F32),
                        pltpu.VMEM((D_GROUPS, SEQ, HEAD_DIM), F32),
                        pltpu.VMEM((D_UNROLL * D_GROUPS, QBLK, D_WIN), F32),
                        pltpu.VMEM((D_UNROLL * D_GROUPS, QBLK, D_WIN), F32)],
        compiler_params=_params(),
        name="dilated_attention",
    )(h, h, h, h, h, _band_mask_table(D_SPAN))


def _rope_tables(kind):
    pos = np.arange(SEQ, dtype=np.float64)
    cos = np.ones((SEQ, HEAD_DIM), np.float64)
    sin = np.zeros((SEQ, HEAD_DIM), np.float64)

    def fill(start, r, p, theta):
        half = r // 2
        inv = np.exp(-math.log(theta) * np.arange(half, dtype=np.float64) * (2.0 / r))
        ang = p[:, None] * inv[None, :]
        cos[:, start:start + half] = np.cos(ang)
        cos[:, start + half:start + r] = np.cos(ang)
        sin[:, start:start + half] = -np.sin(ang)
        sin[:, start + half:start + r] = np.sin(ang)

    if kind == "axial":
        hw = HEAD_DIM // 2
        fill(0, hw, np.floor(pos / GRID_W), AXIAL_THETA)
        fill(hw, hw, pos % GRID_W, AXIAL_THETA)
        moves = AXIAL_MOVES
    else:
        fill(0, ROPE_DIMS, pos, ROPE_THETA)
        moves = PARTIAL_MOVES
    perm = _moves_to_perm(moves)
    return moves, jnp.asarray(cos[:, perm], dtype=F32), jnp.asarray(sin[:, perm], dtype=F32)


def _gain_row(parts):
    rows = []
    for g, heads, scale, perm in parts:
        if g is None:
            g = jnp.ones((HEAD_DIM,), F32)
        g = g.astype(F32) * scale
        if perm is not None:
            g = g[perm]
        rows.append(jnp.tile(g, heads))
    return jnp.concatenate(rows).reshape(1, -1)


def _layer_ab(x, xg, ssq, w_in, w_out, li, g_next, a_qn, a_kn, a_rpb, b_qn, b_kn, axial):
    perm = _moves_to_perm(axial[0])
    gains = _gain_row([(a_qn, A_HEADS, SCALE, None), (a_kn, A_HEADS, 1.0, None), (None, A_HEADS, 1.0, None),
                       (b_qn, B_Q_HEADS, SCALE, perm), (b_kn, B_KV_HEADS, 1.0, perm),
                       (None, B_KV_HEADS, 1.0, None)])
    h = in_proj(xg, ssq, w_in, li, AB_KINDS, gains, axial)
    oa = neighbourhood_attention(h, a_rpb)
    ob = dense_gqa(h)
    return matmul_residual([oa, ob], w_out, li, x, g_next)


def _layer_cd(x, xg, ssq, w_in, w_out, li, g_next, c_qn, c_kn, c_sink, d_qn, d_kn, partial):
    perm = _moves_to_perm(partial[0])
    gains = _gain_row([(c_qn, C_Q_HEADS, SCALE, perm), (c_kn, C_KV_HEADS, 1.0, perm), (None, C_KV_HEADS, 1.0, None),
                       (d_qn, D_Q_HEADS, SCALE, perm), (d_kn, D_SLOTS, 1.0, perm), (None, D_SLOTS, 1.0, None)])
    h = in_proj(xg, ssq, w_in, li, CD_KINDS, gains, partial)
    oc = window_gqa_sink(h, c_sink)
    od = dilated_attention(h)
    return matmul_residual([oc, od], w_out, li, x, g_next)


def kernel(x, attn_norm, ffn_norm, ab_w_in, ab_w_out, a_q_norm, a_k_norm, a_rpb, b_q_norm, b_k_norm,
           cd_w_in, cd_w_out, c_q_norm, c_k_norm, c_sink, d_q_norm, d_k_norm, w_gate, w_up, w_down):
    axial = _rope_tables("axial")
    partial = _rope_tables("partial")
    x = x.reshape(M_ROWS, D_MODEL)
    xg, ssq = prenorm(x, attn_norm[0])
    for layer in range(DEPTH):
        j = layer // 2
        if layer % 2 == 0:
            x, xg, ssq = _layer_ab(x, xg, ssq, ab_w_in, ab_w_out, j, ffn_norm[layer], a_q_norm[j], a_k_norm[j],
                                   a_rpb[j], b_q_norm[j], b_k_norm[j], axial)
        else:
            x, xg, ssq = _layer_cd(x, xg, ssq, cd_w_in, cd_w_out, j, ffn_norm[layer], c_q_norm[j], c_k_norm[j],
                                   c_sink[j], d_q_norm[j], d_k_norm[j], partial)
        hidden = gate_up(xg, ssq, w_gate, w_up, layer)
        if layer + 1 < DEPTH:
            x, xg, ssq = matmul_residual([hidden], w_down, layer, x, attn_norm[layer + 1])
        else:
            x = matmul_residual([hidden], w_down, layer, x)
    return x.reshape(BATCH, SEQ, D_MODEL)
```

```python
import functools
import math

import numpy as np
import jax
import jax.numpy as jnp
from jax import lax
from jax.experimental import pallas as pl
from jax.experimental.pallas import tpu as pltpu

D_MODEL = 2048
BATCH = 2
SEQ = 4096
DEPTH = 4
HEAD_DIM = 128
GRID_W = 64
GRID_ROWS = SEQ // GRID_W
EPS = 1e-6
A_HEADS = 8
NA_ROWS = 8
NA_COLS = 16
B_Q_HEADS = 8
B_KV_HEADS = 2
AXIAL_THETA = 10000.0
C_Q_HEADS = 12
C_KV_HEADS = 4
C_WINDOW = 128
D_DILATIONS = ((128, 1), (512, 4), (2048, 16))
D_GROUPS = len(D_DILATIONS)
D_SLOTS = 4
D_Q_HEADS = D_GROUPS * D_SLOTS
ROPE_THETA = 500000.0
ROPE_DIMS = HEAD_DIM // 4
FFN_HIDDEN = ((-(-8 * D_MODEL // 3)) + 255) // 256 * 256
M_ROWS = BATCH * SEQ
LOG2E = math.log2(math.e)
SCALE = HEAD_DIM ** -0.5 * LOG2E
NEG = -1e30

VMEM_LIMIT_BYTES = 52 * 1024 * 1024
MXU_N = 256

F32 = jnp.float32
BF16 = jnp.bfloat16
NT_DIMS = (((1,), (1,)), ((), ()))


def _params(**kw):
    return pltpu.CompilerParams(vmem_limit_bytes=VMEM_LIMIT_BYTES, **kw)


LANES = 128


def _lane_partial_sums(sq):
    out = sq[:, :LANES]
    for i in range(1, sq.shape[1] // LANES):
        out = out + sq[:, i * LANES:(i + 1) * LANES]
    return out


ROW_SCALE_CHUNK = 256


def _store_row_scales(ssq_ref, inv_ref, row0, n_rows, dim):
    for c in range(n_rows // ROW_SCALE_CHUNK):
        rows = slice(c * ROW_SCALE_CHUNK, (c + 1) * ROW_SCALE_CHUNK)
        part = ssq_ref[0, rows, :]
        for t in range(1, ssq_ref.shape[0]):
            part = part + ssq_ref[t, rows, :]
        inv = lax.rsqrt(jnp.sum(part, axis=-1, keepdims=True) * (1.0 / dim) + EPS)
        inv_ref[pl.ds(row0 + c * ROW_SCALE_CHUNK, ROW_SCALE_CHUNK), :] = jnp.broadcast_to(
            inv, (ROW_SCALE_CHUNK, LANES))


def _ssq_spec(ssq, tm):
    return pl.BlockSpec((ssq.shape[0], tm, LANES), lambda j, i: (0, jnp.where(j == 0, i, 0), 0))


def _prenorm_body(x_ref, g_ref, xg_ref, ssq_ref):
    x = x_ref[...]
    xg_ref[...] = (x * g_ref[...]).astype(xg_ref.dtype)
    ssq_ref[...] = _lane_partial_sums(x * x)


def prenorm(x, g, tm=512):
    m, d = x.shape
    return pl.pallas_call(
        _prenorm_body,
        grid=(m // tm,),
        in_specs=[pl.BlockSpec((tm, d), lambda i: (i, 0)), pl.BlockSpec((1, d), lambda i: (0, 0))],
        out_specs=[pl.BlockSpec((tm, d), lambda i: (i, 0)), pl.BlockSpec((None, tm, LANES), lambda i: (0, i, 0))],
        out_shape=[jax.ShapeDtypeStruct((m, d), BF16), jax.ShapeDtypeStruct((1, m, LANES), F32)],
        compiler_params=_params(),
        name="prenorm",
    )(x, g.reshape(1, d).astype(F32))


HALF_LANES = HEAD_DIM // 2
AXIAL_MOVES = ((96, 32, 64), (32, 64, 96))
PARTIAL_MOVES = ((112, 16, 64), (48, 64, 80))


def _moves_to_perm(moves):
    perm = np.arange(HEAD_DIM)
    for shift, lo, hi in moves:
        perm[lo:hi] = (np.arange(lo, hi) - shift) % HEAD_DIM
    assert sorted(perm.tolist()) == list(range(HEAD_DIM))
    return perm


def _permute_lanes(w, moves):
    lane = lax.broadcasted_iota(jnp.int32, w.shape, 1)
    out = w
    for shift, lo, hi in moves:
        out = jnp.where((lane >= lo) & (lane < hi), pltpu.roll(w, shift, 1), out)
    return out


PROJ_TM = 2048
PROJ_TN = 512
PROJ_SUB = 256


def _in_proj_body(sig_ref, x_ref, ssq_ref, w_ref, g_ref, cos_ref, sin_ref, o_ref, wbf_ref, inv_ref, *,
                  sigs, moves):
    n = pl.program_id(0)
    m = pl.program_id(1)
    heads = PROJ_TN // HEAD_DIM
    row0 = pl.multiple_of(m * PROJ_TM, PROJ_TM)

    @pl.when(n == 0)
    def _():
        _store_row_scales(ssq_ref, inv_ref, row0, PROJ_TM, x_ref.shape[1])

    def cast_weights(kinds):
        for h in range(heads):
            cols = slice(h * HEAD_DIM, (h + 1) * HEAD_DIM)
            w = w_ref[:, cols]
            if kinds[h] == "rope":
                w = _permute_lanes(w, moves)
            wbf_ref[:, cols] = w.astype(BF16)

    def compute(kinds):
        for r in range(PROJ_TM // PROJ_SUB):
            rows = slice(r * PROJ_SUB, (r + 1) * PROJ_SUB)
            x = x_ref[rows, :]
            inv = inv_ref[pl.ds(row0 + r * PROJ_SUB, PROJ_SUB), :]
            for p in range(PROJ_TN // MXU_N):
                acc = jnp.dot(x, wbf_ref[:, p * MXU_N:(p + 1) * MXU_N], preferred_element_type=F32)
                for hh in range(MXU_N // HEAD_DIM):
                    h = p * (MXU_N // HEAD_DIM) + hh
                    cols = slice(h * HEAD_DIM, (h + 1) * HEAD_DIM)
                    y = acc[:, hh * HEAD_DIM:(hh + 1) * HEAD_DIM]
                    if kinds[h] == "plain":
                        y = y * inv
                    else:
                        ms = jnp.mean(y * y, axis=-1, keepdims=True)
                        y = y * (inv * lax.rsqrt(ms * (inv * inv) + EPS)) * g_ref[:, cols]
                    if kinds[h] == "rope":
                        y = y * cos_ref[rows, :] + pltpu.roll(y, HALF_LANES, 1) * sin_ref[rows, :]
                    o_ref[rows, cols] = y.astype(o_ref.dtype)

    for sid, kinds in enumerate(sigs):
        @pl.when(sig_ref[n] == sid)
        def _(kinds=kinds):
            @pl.when(m == 0)
            def _():
                cast_weights(kinds)

            compute(kinds)


def in_proj(xg, ssq, w, li, kinds, gains, rope):
    m, k = xg.shape
    n = w.shape[2]
    heads = PROJ_TN // HEAD_DIM
    tiles = [tuple(kinds[i:i + heads]) for i in range(0, len(kinds), heads)]
    sigs = tuple(dict.fromkeys(tiles))
    sig_ids = jnp.asarray([sigs.index(t) for t in tiles], jnp.int32)
    moves, cos, sin = rope
    pos_blocks = SEQ // PROJ_TM
    return pl.pallas_call(
        functools.partial(_in_proj_body, sigs=sigs, moves=moves),
        grid=(n // PROJ_TN, m // PROJ_TM),
        in_specs=[pl.BlockSpec(memory_space=pltpu.SMEM),
                  pl.BlockSpec((PROJ_TM, k), lambda j, i: (i, 0)),
                  _ssq_spec(ssq, PROJ_TM),
                  pl.BlockSpec((None, k, PROJ_TN), lambda j, i: (li, 0, j)),
                  pl.BlockSpec((1, PROJ_TN), lambda j, i: (0, j)),
                  pl.BlockSpec((PROJ_TM, HEAD_DIM), lambda j, i: (i % pos_blocks, 0)),
                  pl.BlockSpec((PROJ_TM, HEAD_DIM), lambda j, i: (i % pos_blocks, 0))],
        out_specs=pl.BlockSpec((PROJ_TM, PROJ_TN), lambda j, i: (i, j)),
        out_shape=jax.ShapeDtypeStruct((m, n), BF16),
        scratch_shapes=[pltpu.VMEM((k, PROJ_TN), BF16), pltpu.VMEM((m, LANES), F32)],
        compiler_params=_params(),
        name="in_proj",
    )(sig_ids, xg, ssq, w, gains, cos, sin)


RES_SUB = 256
RES_WEIGHT_TILE_BYTES = 12 * 1024 * 1024
RES_WEIGHT_FULL_BYTES = 16 * 1024 * 1024


def _res_tiles(kt, n):
    if kt * n * 4 <= RES_WEIGHT_FULL_BYTES:
        return 512, n, True
    tn = n
    while kt * tn * 4 > RES_WEIGHT_TILE_BYTES and tn % (2 * MXU_N) == 0:
        tn //= 2
    return 512, tn, False


def _mm_res_body(*refs, n_in, tm, tn, emit_norm):
    a_refs = refs[:n_in]
    if emit_norm:
        w_ref, x_ref, g_ref, o_ref, xg_ref, ssq_ref, wbf_ref = refs[n_in:]
    else:
        w_ref, x_ref, o_ref, wbf_ref = refs[n_in:]

    @pl.when(pl.program_id(1) == 0)
    def _():
        wbf_ref[...] = w_ref[...].astype(BF16)

    for r in range(tm // RES_SUB):
        rows = slice(r * RES_SUB, (r + 1) * RES_SUB)
        a = a_refs[0][rows, :] if n_in == 1 else jnp.concatenate([a_ref[rows, :] for a_ref in a_refs], axis=1)
        part = None
        for p in range(tn // MXU_N):
            cols = slice(p * MXU_N, (p + 1) * MXU_N)
            acc = x_ref[rows, cols] + jnp.dot(a, wbf_ref[:, cols], preferred_element_type=F32)
            o_ref[rows, cols] = acc
            if emit_norm:
                xg_ref[rows, cols] = (acc * g_ref[:, cols]).astype(xg_ref.dtype)
                sq = _lane_partial_sums(acc * acc)
                part = sq if part is None else part + sq
        if emit_norm:
            ssq_ref[rows, :] = part


def matmul_residual(a_list, w, li, x, g_next=None):
    m, n = x.shape
    widths = tuple(a.shape[1] for a in a_list)
    kt = sum(widths)
    assert kt == w.shape[1]
    tm, tn, whole = _res_tiles(kt, n)
    emit_norm = g_next is not None
    tile = pl.BlockSpec((tm, tn), lambda j, i: (i, j))
    in_specs = [pl.BlockSpec((tm, ka), lambda j, i: (i, 0)) for ka in widths]
    w_mode = dict(pipeline_mode=pl.Buffered(1)) if whole else {}
    in_specs += [pl.BlockSpec((None, kt, tn), lambda j, i: (li, 0, j), **w_mode), tile]
    args = [*a_list, w, x]
    out_specs, out_shape = tile, jax.ShapeDtypeStruct((m, n), F32)
    if emit_norm:
        in_specs.append(pl.BlockSpec((1, tn), lambda j, i: (0, j)))
        args.append(g_next.reshape(1, n).astype(F32))
        out_specs = [tile, tile, pl.BlockSpec((None, tm, LANES), lambda j, i: (j, i, 0))]
        out_shape = [out_shape, jax.ShapeDtypeStruct((m, n), BF16), jax.ShapeDtypeStruct((n // tn, m, LANES), F32)]
    return pl.pallas_call(
        functools.partial(_mm_res_body, n_in=len(a_list), tm=tm, tn=tn, emit_norm=emit_norm),
        grid=(n // tn, m // tm),
        in_specs=in_specs,
        out_specs=out_specs,
        out_shape=out_shape,
        scratch_shapes=[pltpu.VMEM((kt, tn), BF16)],
        compiler_params=_params(),
        name="matmul_residual",
    )(*args)


FFN_TM = 2048
FFN_TN = 512
FFN_SUB = 256


def _gate_up_body(x_ref, ssq_ref, wg_ref, wu_ref, o_ref, wgb_ref, wub_ref, inv_ref):
    row0 = pl.multiple_of(pl.program_id(1) * FFN_TM, FFN_TM)

    @pl.when(pl.program_id(1) == 0)
    def _():
        wgb_ref[...] = wg_ref[...].astype(BF16)
        wub_ref[...] = wu_ref[...].astype(BF16)

    @pl.when(pl.program_id(0) == 0)
    def _():
        _store_row_scales(ssq_ref, inv_ref, row0, FFN_TM, x_ref.shape[1])

    for r in range(FFN_TM // FFN_SUB):
        rows = slice(r * FFN_SUB, (r + 1) * FFN_SUB)
        x = x_ref[rows, :]
        inv = jnp.concatenate([inv_ref[pl.ds(row0 + r * FFN_SUB, FFN_SUB), :]] * (MXU_N // LANES), axis=1)
        for p in range(FFN_TN // MXU_N):
            cols = slice(p * MXU_N, (p + 1) * MXU_N)
            g = jnp.dot(x, wgb_ref[:, cols], preferred_element_type=F32) * inv
            u = jnp.dot(x, wub_ref[:, cols], preferred_element_type=F32) * inv
            o_ref[rows, cols] = (g * (1.0 / (1.0 + jnp.exp(-g))) * u).astype(o_ref.dtype)


def gate_up(xg, ssq, wg, wu, li):
    m, k = xg.shape
    n = wg.shape[2]
    return pl.pallas_call(
        _gate_up_body,
        grid=(n // FFN_TN, m // FFN_TM),
        in_specs=[pl.BlockSpec((FFN_TM, k), lambda j, i: (i, 0)),
                  _ssq_spec(ssq, FFN_TM),
                  pl.BlockSpec((None, k, FFN_TN), lambda j, i: (li, 0, j)),
                  pl.BlockSpec((None, k, FFN_TN), lambda j, i: (li, 0, j))],
        out_specs=pl.BlockSpec((FFN_TM, FFN_TN), lambda j, i: (i, j)),
        out_shape=jax.ShapeDtypeStruct((m, n), BF16),
        scratch_shapes=[pltpu.VMEM((k, FFN_TN), BF16), pltpu.VMEM((k, FFN_TN), BF16),
                        pltpu.VMEM((m, LANES), F32)],
        compiler_params=_params(),
        name="gate_up",
    )(xg, ssq, wg, wu)


def _two_stage_pipeline(n_blocks, per_stage, logits, finish, buf_a, buf_b):
    n_stages = n_blocks // per_stage
    assert n_stages * per_stage == n_blocks and n_stages % 2 == 0 and n_stages >= 2

    def stage(first, cur, nxt, with_next=True):
        for u in range(per_stage):
            finish(first + u, cur, u)
            if with_next:
                logits(first + per_stage + u, nxt, u)

    for u in range(per_stage):
        logits(u, buf_a, u)

    def pair(j, carry):
        first = 2 * j * per_stage
        stage(first, buf_a, buf_b)
        stage(first + per_stage, buf_b, buf_a)
        return carry

    lax.fori_loop(0, n_stages // 2 - 1, pair, 0)
    last = (n_stages - 2) * per_stage
    stage(last, buf_a, buf_b)
    stage(last + per_stage, buf_b, buf_a, with_next=False)


AB_QA, AB_KA, AB_VA = 0, A_HEADS, 2 * A_HEADS
AB_QB = 3 * A_HEADS
AB_KB = AB_QB + B_Q_HEADS
AB_VB = AB_KB + B_KV_HEADS
AB_KINDS = ("norm",) * (2 * A_HEADS) + ("plain",) * A_HEADS + ("rope",) * (B_Q_HEADS + B_KV_HEADS) \
    + ("plain",) * B_KV_HEADS
CD_QC = 0
CD_KC = C_Q_HEADS
CD_VC = CD_KC + C_KV_HEADS
CD_QD = CD_VC + C_KV_HEADS
CD_KD = CD_QD + D_Q_HEADS
CD_VD = CD_KD + D_SLOTS
CD_KINDS = ("rope",) * (C_Q_HEADS + C_KV_HEADS) + ("plain",) * C_KV_HEADS \
    + ("rope",) * (D_Q_HEADS + D_SLOTS) + ("plain",) * D_SLOTS


NA_QROWS = 2
NA_WIN_ROWS = 10
NA_Q = NA_QROWS * GRID_W
NA_KEYS = NA_WIN_ROWS * GRID_W
NA_BLOCKS = GRID_ROWS // NA_QROWS
NA_VARIANT_BLOCKS = (0, 1, 2, NA_BLOCKS - 2, NA_BLOCKS - 1)


def _na_win_start(rb):
    return np.clip(NA_QROWS * rb - NA_ROWS // 2, 0, GRID_ROWS - NA_WIN_ROWS)


N_DR = 2 * NA_ROWS - 1


def _na_row_slabs():
    dr_idx = np.full((len(NA_VARIANT_BLOCKS), NA_QROWS, NA_WIN_ROWS), N_DR, np.int32)
    for v, rb in enumerate(NA_VARIANT_BLOCKS):
        ws = _na_win_start(rb)
        for qr in range(NA_QROWS):
            qrow = NA_QROWS * rb + qr
            r0 = np.clip(qrow - NA_ROWS // 2, 0, GRID_ROWS - NA_ROWS)
            for kr in range(NA_WIN_ROWS):
                krow = ws + kr
                if r0 <= krow < r0 + NA_ROWS:
                    dr_idx[v, qr, kr] = krow - qrow + NA_ROWS - 1
    return dr_idx


def _na_column_table(rpb):
    assert rpb.shape[1:] == (A_HEADS, N_DR, 2 * NA_COLS - 1)
    nh = rpb.shape[0] * A_HEADS
    rpb = rpb.reshape(nh, N_DR, 2 * NA_COLS - 1)
    lanes = 2 * GRID_W
    ext = jnp.concatenate([rpb.astype(F32) * LOG2E, jnp.full((nh, N_DR, lanes - rpb.shape[2]), NEG, F32)], axis=-1)
    ext = jnp.roll(ext, -(NA_COLS - 1), axis=-1)
    toep = jnp.tile(ext, (1, 1, GRID_W))[:, :, :GRID_W * (lanes - 1)].reshape(nh, N_DR, GRID_W, lanes - 1)
    toep = toep[..., :GRID_W]
    cols = np.arange(GRID_W)
    c0 = np.clip(cols - NA_COLS // 2, 0, GRID_W - NA_COLS)
    col_ok = (cols[None, :] >= c0[:, None]) & (cols[None, :] < c0[:, None] + NA_COLS)
    toep = jnp.where(col_ok[None, None], toep, NEG)
    toep = jnp.concatenate([toep, jnp.full((nh, 1, GRID_W, GRID_W), NEG, F32)], axis=1)
    return jnp.concatenate([toep, toep], axis=-1)


NA_UNROLL = 8


def _na_body(q_ref, k_ref, v_ref, c_ref, o_ref, t_ref, sa_ref, sb_ref):
    left_half = lax.broadcasted_iota(jnp.int32, (GRID_W, 2 * GRID_W), 1) < GRID_W
    for v, per_q in enumerate(_na_row_slabs()):
        for qr, slabs in enumerate(per_q):
            for pair in range(NA_WIN_ROWS // 2):
                tile = jnp.where(left_half, c_ref[int(slabs[2 * pair])], c_ref[int(slabs[2 * pair + 1])])
                t_ref[v, qr * GRID_W:(qr + 1) * GRID_W, pair * 2 * GRID_W:(pair + 1) * 2 * GRID_W] = tile

    def offsets(rb):
        ws = jnp.clip(NA_QROWS * rb - NA_ROWS // 2, 0, GRID_ROWS - NA_WIN_ROWS)
        return pl.multiple_of(rb * NA_Q, NA_Q), pl.multiple_of(ws * GRID_W, GRID_W)

    def logits(rb, s_ref, u):
        qoff, koff = offsets(rb)
        variant = jnp.where(rb < 2, rb, jnp.where(rb >= NA_BLOCKS - 2, rb - (NA_BLOCKS - 5), 2))
        s_ref[u] = lax.dot_general(q_ref[pl.ds(qoff, NA_Q), :], k_ref[pl.ds(koff, NA_KEYS), :], NT_DIMS,
                                   preferred_element_type=F32) + t_ref[variant]

    def finish(rb, s_ref, u):
        qoff, koff = offsets(rb)
        s = s_ref[u]
        m = jnp.max(s, axis=-1, keepdims=True)
        p = jnp.exp2(s - m)
        l = jnp.sum(p, axis=-1, keepdims=True)
        o = jnp.dot(p.astype(BF16), v_ref[pl.ds(koff, NA_KEYS), :], preferred_element_type=F32)
        o_ref[pl.ds(qoff, NA_Q), :] = (o / l).astype(o_ref.dtype)

    _two_stage_pipeline(NA_BLOCKS, NA_UNROLL, logits, finish, sa_ref, sb_ref)


def neighbourhood_attention(h, column_table, li):
    nv = len(NA_VARIANT_BLOCKS)
    blk = (SEQ, HEAD_DIM)
    return pl.pallas_call(
        _na_body,
        grid=(BATCH, A_HEADS),
        in_specs=[pl.BlockSpec(blk, lambda b, hd: (b, AB_QA + hd)),
                  pl.BlockSpec(blk, lambda b, hd: (b, AB_KA + hd)),
                  pl.BlockSpec(blk, lambda b, hd: (b, AB_VA + hd)),
                  pl.BlockSpec((None, N_DR + 1, GRID_W, 2 * GRID_W), lambda b, hd: (li * A_HEADS + hd, 0, 0, 0))],
        out_specs=pl.BlockSpec(blk, lambda b, hd: (b, hd)),
        out_shape=jax.ShapeDtypeStruct((M_ROWS, A_HEADS * HEAD_DIM), BF16),
        scratch_shapes=[pltpu.VMEM((nv, NA_Q, NA_KEYS), F32),
                        pltpu.VMEM((NA_UNROLL, NA_Q, NA_KEYS), F32),
                        pltpu.VMEM((NA_UNROLL, NA_Q, NA_KEYS), F32)],
        compiler_params=_params(),
        name="neighbourhood_attention",
    )(h, h, h, column_table)


B_GROUP = B_Q_HEADS // B_KV_HEADS


DENSE_TQ = 512
DENSE_TK = 256


def _dense_body(q_ref, k_ref, v_ref, o_ref):
    tq = DENSE_TQ
    q = jnp.concatenate([q_ref[:, g * HEAD_DIM:(g + 1) * HEAD_DIM] for g in range(B_GROUP)], axis=0)
    ones = jnp.ones((DENSE_TK, HEAD_DIM), BF16)
    m = acc = None
    for c in range(SEQ // DENSE_TK):
        keys = slice(c * DENSE_TK, (c + 1) * DENSE_TK)
        s = lax.dot_general(q, k_ref[keys, :], NT_DIMS, preferred_element_type=F32).astype(BF16)
        v_ones = jnp.concatenate([v_ref[keys, :], ones], axis=1)
        row_max = jnp.max(s, axis=-1, keepdims=True)
        if c == 0:
            m = row_max
            acc = jnp.dot(jnp.exp2(s - m), v_ones, preferred_element_type=F32)
        else:
            m_new = jnp.maximum(m, row_max)
            alpha = jnp.exp2(m.astype(F32) - m_new.astype(F32))
            acc = alpha * acc + jnp.dot(jnp.exp2(s - m_new), v_ones, preferred_element_type=F32)
            m = m_new
    o = acc[:, :HEAD_DIM] / acc[:, HEAD_DIM:]
    for g in range(B_GROUP):
        o_ref[:, g * HEAD_DIM:(g + 1) * HEAD_DIM] = o[g * tq:(g + 1) * tq].astype(o_ref.dtype)


def dense_gqa(h):
    tq = DENSE_TQ
    nq = SEQ // tq
    gw = B_GROUP * HEAD_DIM
    assert AB_QB % B_GROUP == 0
    q_blk0 = AB_QB // B_GROUP
    blk = (SEQ, HEAD_DIM)
    return pl.pallas_call(
        _dense_body,
        grid=(BATCH, B_KV_HEADS, nq),
        in_specs=[pl.BlockSpec((tq, gw), lambda b, kv, i: (b * nq + i, q_blk0 + kv)),
                  pl.BlockSpec(blk, lambda b, kv, i: (b, AB_KB + kv)),
                  pl.BlockSpec(blk, lambda b, kv, i: (b, AB_VB + kv))],
        out_specs=pl.BlockSpec((tq, gw), lambda b, kv, i: (b * nq + i, kv)),
        out_shape=jax.ShapeDtypeStruct((M_ROWS, B_Q_HEADS * HEAD_DIM), BF16),
        compiler_params=_params(),
        name="dense_gqa",
    )(h, h, h)


QBLK = 128
N_QBLK = SEQ // QBLK


def _band_mask_table(span):
    assert span <= QBLK
    col = np.arange(QBLK + 2 * span)[None, :]
    row = np.arange(QBLK)[:, None]
    t = np.stack([np.where(np.abs(col - row + shift) <= span, 0.0, NEG) for shift in (0, -span, -2 * span)])
    return jnp.asarray(t, F32)


def _band_window(ub, blocks, span):
    start = jnp.clip(ub * QBLK - span, 0, blocks * QBLK - (QBLK + 2 * span))
    variant = jnp.where(ub == 0, 0, jnp.where(ub == blocks - 1, 2, 1))
    return start, variant


C_GROUP = C_Q_HEADS // C_KV_HEADS


C_WIN = QBLK + 2 * C_WINDOW
C_UNROLL = 2


def _window_sink_body(sink_ref, q_ref, k_ref, v_ref, mask_ref, o_ref, sa_ref, sb_ref):
    kv = pl.program_id(1)

    def logits(i, s_ref, u):
        qoff = pl.multiple_of(i * QBLK, QBLK)
        start, _ = _band_window(i, N_QBLK, C_WINDOW)
        q = q_ref[pl.ds(qoff, QBLK), :]
        qs = jnp.concatenate([q[:, g * HEAD_DIM:(g + 1) * HEAD_DIM] for g in range(C_GROUP)], axis=0)
        s_ref[u] = lax.dot_general(qs, k_ref[pl.ds(pl.multiple_of(start, QBLK), C_WIN), :], NT_DIMS,
                                   preferred_element_type=F32)

    def finish(i, s_ref, u):
        qoff = pl.multiple_of(i * QBLK, QBLK)
        start, variant = _band_window(i, N_QBLK, C_WINDOW)
        mask = mask_ref[variant]
        ps, ls = [], []
        for g in range(C_GROUP):
            sg = s_ref[u, g * QBLK:(g + 1) * QBLK, :] + mask
            sink = sink_ref[kv * C_GROUP + g] * LOG2E
            m = jnp.maximum(jnp.max(sg, axis=-1, keepdims=True), sink)
            p = jnp.exp2(sg - m)
            ls.append(jnp.sum(p, axis=-1, keepdims=True) + jnp.exp2(sink - m))
            ps.append(p.astype(BF16))
        o = jnp.dot(jnp.concatenate(ps, axis=0), v_ref[pl.ds(pl.multiple_of(start, QBLK), C_WIN), :],
                    preferred_element_type=F32)
        for g in range(C_GROUP):
            o_ref[pl.ds(qoff, QBLK), g * HEAD_DIM:(g + 1) * HEAD_DIM] = (
                o[g * QBLK:(g + 1) * QBLK] / ls[g]).astype(o_ref.dtype)

    _two_stage_pipeline(N_QBLK, C_UNROLL, logits, finish, sa_ref, sb_ref)


def window_gqa_sink(h, sink):
    gw = C_GROUP * HEAD_DIM
    assert CD_QC % C_GROUP == 0
    q_blk0 = CD_QC // C_GROUP
    blk = (SEQ, HEAD_DIM)
    return pl.pallas_call(
        _window_sink_body,
        grid=(BATCH, C_KV_HEADS),
        in_specs=[pl.BlockSpec(memory_space=pltpu.SMEM),
                  pl.BlockSpec((SEQ, gw), lambda b, kv: (b, q_blk0 + kv)),
                  pl.BlockSpec(blk, lambda b, kv: (b, CD_KC + kv)),
                  pl.BlockSpec(blk, lambda b, kv: (b, CD_VC + kv)),
                  pl.BlockSpec((3, QBLK, C_WIN), lambda b, kv: (0, 0, 0))],
        out_specs=pl.BlockSpec((SEQ, gw), lambda b, kv: (b, kv)),
        out_shape=jax.ShapeDtypeStruct((M_ROWS, C_Q_HEADS * HEAD_DIM), BF16),
        scratch_shapes=[pltpu.VMEM((C_UNROLL, C_GROUP * QBLK, C_WIN), F32),
                        pltpu.VMEM((C_UNROLL, C_GROUP * QBLK, C_WIN), F32)],
        compiler_params=_params(),
        name="window_gqa_sink",
    )(sink.astype(F32), h, h, h, _band_mask_table(C_WINDOW))


D_DILS = tuple(d for _, d in D_DILATIONS)
D_SPAN = D_DILATIONS[0][0] // 2
assert all((w // 2) // d == D_SPAN and N_QBLK % d == 0 for w, d in D_DILATIONS) and D_DILS[0] == 1
D_WIN = QBLK + 2 * D_SPAN
D_UNROLL = 2


def _dilated_body(q0_ref, q1_ref, q2_ref, k_ref, v_ref, mask_ref, o_ref,
                  stage_ref, qc_ref, kc_ref, vc_ref, og_ref, lse_ref, sa_ref, sb_ref):
    def to_class_major(dst_ref, slot, d):
        run = SEQ // d
        for rho in range(d):
            dst_ref[slot, pl.ds(rho * run, run), :] = stage_ref[pl.ds(rho, run, stride=d), :].astype(BF16)

    for src_ref, dst_ref in ((k_ref, kc_ref), (v_ref, vc_ref)):
        stage_ref[...] = src_ref[...].astype(F32)
        for g in range(1, D_GROUPS):
            to_class_major(dst_ref, g - 1, D_DILS[g])
    for g, src_ref in ((1, q1_ref), (2, q2_ref)):
        stage_ref[...] = src_ref[...].astype(F32)
        to_class_major(qc_ref, g - 1, D_DILS[g])

    def geometry(g, i):
        d = D_DILS[g]
        run_blocks = N_QBLK // d
        rho = i // run_blocks
        ub = i % run_blocks
        start, variant = _band_window(ub, run_blocks, D_SPAN)
        koff = pl.multiple_of(rho * (run_blocks * QBLK) + start, D_SPAN)
        return pl.multiple_of(i * QBLK, QBLK), koff, variant, rho, ub

    def logits(i, s_ref, u):
        for g in range(D_GROUPS):
            qoff, koff, variant, _, _ = geometry(g, i)
            if g == 0:
                q, k = q0_ref[pl.ds(qoff, QBLK), :], k_ref[pl.ds(koff, D_WIN), :]
            else:
                q, k = qc_ref[g - 1, pl.ds(qoff, QBLK), :], kc_ref[g - 1, pl.ds(koff, D_WIN), :]
            s_ref[u * D_GROUPS + g] = lax.dot_general(q, k, NT_DIMS, preferred_element_type=F32) + mask_ref[variant]

    def finish(i, s_ref, u):
        for g in range(D_GROUPS):
            d = D_DILS[g]
            qoff, koff, _, rho, ub = geometry(g, i)
            v = v_ref[pl.ds(koff, D_WIN), :] if g == 0 else vc_ref[g - 1, pl.ds(koff, D_WIN), :]
            s = s_ref[u * D_GROUPS + g]
            m = jnp.max(s, axis=-1, keepdims=True)
            p = jnp.exp2(s - m)
            l = jnp.sum(p, axis=-1, keepdims=True)
            o = jnp.dot(p.astype(BF16), v, preferred_element_type=F32) / l
            lse = jnp.broadcast_to(m + jnp.log2(l), (QBLK, HEAD_DIM))
            if d == 1:
                rows = pl.ds(qoff, QBLK)
            else:
                rows = pl.ds(rho + d * ub * QBLK, QBLK, stride=d)
            og_ref[g, rows, :] = o
            lse_ref[g, rows, :] = lse

    _two_stage_pipeline(N_QBLK, D_UNROLL, logits, finish, sa_ref, sb_ref)

    def merge(c, carry):
        rows = pl.ds(pl.multiple_of(c * QBLK, QBLK), QBLK)
        lses = [lse_ref[g, rows, :] for g in range(D_GROUPS)]
        top = functools.reduce(jnp.maximum, lses)
        ws = [jnp.exp2(x - top) for x in lses]
        num = sum(w * og_ref[g, rows, :] for g, w in enumerate(ws))
        o_ref[rows, :] = (num / sum(ws)).astype(o_ref.dtype)
        return carry

    lax.fori_loop(0, N_QBLK, merge, 0)


def dilated_attention(h):
    blk = (SEQ, HEAD_DIM)
    return pl.pallas_call(
        _dilated_body,
        grid=(BATCH, D_SLOTS),
        in_specs=[pl.BlockSpec(blk, lambda b, s: (b, CD_QD + s)),
                  pl.BlockSpec(blk, lambda b, s: (b, CD_QD + D_SLOTS + s)),
                  pl.BlockSpec(blk, lambda b, s: (b, CD_QD + 2 * D_SLOTS + s)),
                  pl.BlockSpec(blk, lambda b, s: (b, CD_KD + s)),
                  pl.BlockSpec(blk, lambda b, s: (b, CD_VD + s)),
                  pl.BlockSpec((3, QBLK, D_WIN), lambda b, s: (0, 0, 0))],
        out_specs=pl.BlockSpec(blk, lambda b, s: (b, s)),
        out_shape=jax.ShapeDtypeStruct((M_ROWS, D_SLOTS * HEAD_DIM), BF16),
        scratch_shapes=[pltpu.VMEM((SEQ, HEAD_DIM), F32),
                        pltpu.VMEM((D_GROUPS - 1, SEQ, HEAD_DIM), BF16),
                        pltpu.VMEM((D_GROUPS - 1, SEQ, HEAD_DIM), BF16),
                        pltpu.VMEM((D_GROUPS - 1, SEQ, HEAD_DIM), BF16),
                        pltpu.VMEM((D_GROUPS, SEQ, HEAD_DIM), F32),
                        pltpu.VMEM((D_GROUPS, SEQ, HEAD_DIM), F32),
                        pltpu.VMEM((D_UNROLL * D_GROUPS, QBLK, D_WIN), F32),
                        pltpu.VMEM((D_UNROLL * D_GROUPS, QBLK, D_WIN), F32)],
        compiler_params=_params(),
        name="dilated_attention",
    )(h, h, h, h, h, _band_mask_table(D_SPAN))


def _rope_tables(kind):
    pos = np.arange(SEQ, dtype=np.float64)
    cos = np.ones((SEQ, HEAD_DIM), np.float64)
    sin = np.zeros((SEQ, HEAD_DIM), np.float64)

    def fill(start, r, p, theta):
        half = r // 2
        inv = np.exp(-math.log(theta) * np.arange(half, dtype=np.float64) * (2.0 / r))
        ang = p[:, None] * inv[None, :]
        cos[:, start:start + half] = np.cos(ang)
        cos[:, start + half:start + r] = np.cos(ang)
        sin[:, start:start + half] = -np.sin(ang)
        sin[:, start + half:start + r] = np.sin(ang)

    if kind == "axial":
        hw = HEAD_DIM // 2
        fill(0, hw, np.floor(pos / GRID_W), AXIAL_THETA)
        fill(hw, hw, pos % GRID_W, AXIAL_THETA)
        moves = AXIAL_MOVES
    else:
        fill(0, ROPE_DIMS, pos, ROPE_THETA)
        moves = PARTIAL_MOVES
    perm = _moves_to_perm(moves)
    return moves, jnp.asarray(cos[:, perm], dtype=F32), jnp.asarray(sin[:, perm], dtype=F32)


def _take_static(g, perm):
    runs, start = [], 0
    for i in range(1, len(perm) + 1):
        if i == len(perm) or perm[i] != perm[i - 1] + 1:
            runs.append(g[int(perm[start]):int(perm[i - 1]) + 1])
            start = i
    return jnp.concatenate(runs)


def _gain_row(parts):
    rows = []
    for g, heads, scale, perm in parts:
        if g is None:
            g = jnp.ones((HEAD_DIM,), F32)
        g = g.astype(F32) * scale
        if perm is not None:
            g = _take_static(g, perm)
        rows.append(jnp.tile(g, heads))
    return jnp.concatenate(rows).reshape(1, -1)


def _layer_ab(x, xg, ssq, w_in, w_out, li, g_next, a_qn, a_kn, na_table, b_qn, b_kn, axial):
    perm = _moves_to_perm(axial[0])
    gains = _gain_row([(a_qn, A_HEADS, SCALE, None), (a_kn, A_HEADS, 1.0, None), (None, A_HEADS, 1.0, None),
                       (b_qn, B_Q_HEADS, SCALE, perm), (b_kn, B_KV_HEADS, 1.0, perm),
                       (None, B_KV_HEADS, 1.0, None)])
    h = in_proj(xg, ssq, w_in, li, AB_KINDS, gains, axial)
    oa = neighbourhood_attention(h, na_table, li)
    ob = dense_gqa(h)
    return matmul_residual([oa, ob], w_out, li, x, g_next)


def _layer_cd(x, xg, ssq, w_in, w_out, li, g_next, c_qn, c_kn, c_sink, d_qn, d_kn, partial):
    perm = _moves_to_perm(partial[0])
    gains = _gain_row([(c_qn, C_Q_HEADS, SCALE, perm), (c_kn, C_KV_HEADS, 1.0, perm), (None, C_KV_HEADS, 1.0, None),
                       (d_qn, D_Q_HEADS, SCALE, perm), (d_kn, D_SLOTS, 1.0, perm), (None, D_SLOTS, 1.0, None)])
    h = in_proj(xg, ssq, w_in, li, CD_KINDS, gains, partial)
    oc = window_gqa_sink(h, c_sink)
    od = dilated_attention(h)
    return matmul_residual([oc, od], w_out, li, x, g_next)


def kernel(x, attn_norm, ffn_norm, ab_w_in, ab_w_out, a_q_norm, a_k_norm, a_rpb, b_q_norm, b_k_norm,
           cd_w_in, cd_w_out, c_q_norm, c_k_norm, c_sink, d_q_norm, d_k_norm, w_gate, w_up, w_down):
    axial = _rope_tables("axial")
    partial = _rope_tables("partial")
    x = x.reshape(M_ROWS, D_MODEL)
    xg, ssq = prenorm(x, attn_norm[0])
    na_table = _na_column_table(a_rpb)
    for layer in range(DEPTH):
        j = layer // 2
        if layer % 2 == 0:
            x, xg, ssq = _layer_ab(x, xg, ssq, ab_w_in, ab_w_out, j, ffn_norm[layer], a_q_norm[j], a_k_norm[j],
                                   na_table, b_q_norm[j], b_k_norm[j], axial)
        else:
            x, xg, ssq = _layer_cd(x, xg, ssq, cd_w_in, cd_w_out, j, ffn_norm[layer], c_q_norm[j], c_k_norm[j],
                                   c_sink[j], d_q_norm[j], d_k_norm[j], partial)
        hidden = gate_up(xg, ssq, w_gate, w_up, layer)
        if layer + 1 < DEPTH:
            x, xg, ssq = matmul_residual([hidden], w_down, layer, x, attn_norm[layer + 1])
        else:
            x = matmul_residual([hidden], w_down, layer, x)
    return x.reshape(BATCH, SEQ, D_MODEL)
```

```python
import functools
import math

import numpy as np
import jax
import jax.numpy as jnp
from jax import lax
from jax.experimental import pallas as pl
from jax.experimental.pallas import tpu as pltpu

D_MODEL = 2048
BATCH = 2
SEQ = 4096
DEPTH = 4
HEAD_DIM = 128
GRID_W = 64
GRID_ROWS = SEQ // GRID_W
EPS = 1e-6
A_HEADS = 8
NA_ROWS = 8
NA_COLS = 16
B_Q_HEADS = 8
B_KV_HEADS = 2
AXIAL_THETA = 10000.0
C_Q_HEADS = 12
C_KV_HEADS = 4
C_WINDOW = 128
D_DILATIONS = ((128, 1), (512, 4), (2048, 16))
D_GROUPS = len(D_DILATIONS)
D_SLOTS = 4
D_Q_HEADS = D_GROUPS * D_SLOTS
ROPE_THETA = 500000.0
ROPE_DIMS = HEAD_DIM // 4
FFN_HIDDEN = ((-(-8 * D_MODEL // 3)) + 255) // 256 * 256
M_ROWS = BATCH * SEQ
LOG2E = math.log2(math.e)
SCALE = HEAD_DIM ** -0.5 * LOG2E
NEG = -1e30

VMEM_LIMIT_BYTES = 52 * 1024 * 1024
MXU_N = 256

F32 = jnp.float32
BF16 = jnp.bfloat16
NT_DIMS = (((1,), (1,)), ((), ()))


def _params(**kw):
    return pltpu.CompilerParams(vmem_limit_bytes=VMEM_LIMIT_BYTES, **kw)


LANES = 128


def _lane_partial_sums(sq):
    out = sq[:, :LANES]
    for i in range(1, sq.shape[1] // LANES):
        out = out + sq[:, i * LANES:(i + 1) * LANES]
    return out


ROW_SCALE_CHUNK = 256


def _store_row_scales(ssq_ref, inv_ref, row0, n_rows, dim):
    for c in range(n_rows // ROW_SCALE_CHUNK):
        rows = slice(c * ROW_SCALE_CHUNK, (c + 1) * ROW_SCALE_CHUNK)
        part = ssq_ref[0, rows, :]
        for t in range(1, ssq_ref.shape[0]):
            part = part + ssq_ref[t, rows, :]
        inv = lax.rsqrt(jnp.sum(part, axis=-1, keepdims=True) * (1.0 / dim) + EPS)
        inv_ref[pl.ds(row0 + c * ROW_SCALE_CHUNK, ROW_SCALE_CHUNK), :] = jnp.broadcast_to(
            inv, (ROW_SCALE_CHUNK, LANES))


def _ssq_spec(ssq, tm):
    return pl.BlockSpec((ssq.shape[0], tm, LANES), lambda j, i: (0, jnp.where(j == 0, i, 0), 0))


def _prenorm_body(x_ref, g_ref, xg_ref, ssq_ref):
    x = x_ref[...]
    xg_ref[...] = (x * g_ref[...]).astype(xg_ref.dtype)
    ssq_ref[...] = _lane_partial_sums(x * x)


def prenorm(x, g, tm=512):
    m, d = x.shape
    return pl.pallas_call(
        _prenorm_body,
        grid=(m // tm,),
        in_specs=[pl.BlockSpec((tm, d), lambda i: (i, 0)), pl.BlockSpec((1, d), lambda i: (0, 0))],
        out_specs=[pl.BlockSpec((tm, d), lambda i: (i, 0)), pl.BlockSpec((None, tm, LANES), lambda i: (0, i, 0))],
        out_shape=[jax.ShapeDtypeStruct((m, d), BF16), jax.ShapeDtypeStruct((1, m, LANES), F32)],
        compiler_params=_params(),
        name="prenorm",
    )(x, g.reshape(1, d).astype(F32))


HALF_LANES = HEAD_DIM // 2
AXIAL_MOVES = ((96, 32, 64), (32, 64, 96))
PARTIAL_MOVES = ((112, 16, 64), (48, 64, 80))


def _moves_to_perm(moves):
    perm = np.arange(HEAD_DIM)
    for shift, lo, hi in moves:
        perm[lo:hi] = (np.arange(lo, hi) - shift) % HEAD_DIM
    assert sorted(perm.tolist()) == list(range(HEAD_DIM))
    return perm


def _permute_lanes(w, moves):
    lane = lax.broadcasted_iota(jnp.int32, w.shape, 1)
    out = w
    for shift, lo, hi in moves:
        out = jnp.where((lane >= lo) & (lane < hi), pltpu.roll(w, shift, 1), out)
    return out


PROJ_TM = 2048
PROJ_TN = 512
PROJ_SUB = 256


def _in_proj_body(sig_ref, x_ref, ssq_ref, w_ref, g_ref, cos_ref, sin_ref, o_ref, wbf_ref, inv_ref, *,
                  sigs, moves):
    n = pl.program_id(0)
    m = pl.program_id(1)
    heads = PROJ_TN // HEAD_DIM
    row0 = pl.multiple_of(m * PROJ_TM, PROJ_TM)

    @pl.when(n == 0)
    def _():
        _store_row_scales(ssq_ref, inv_ref, row0, PROJ_TM, x_ref.shape[1])

    def cast_weights(kinds):
        for h in range(heads):
            cols = slice(h * HEAD_DIM, (h + 1) * HEAD_DIM)
            w = w_ref[:, cols]
            if kinds[h] == "rope":
                w = _permute_lanes(w, moves)
            wbf_ref[:, cols] = w.astype(BF16)

    def compute(kinds):
        for r in range(PROJ_TM // PROJ_SUB):
            rows = slice(r * PROJ_SUB, (r + 1) * PROJ_SUB)
            x = x_ref[rows, :]
            inv = inv_ref[pl.ds(row0 + r * PROJ_SUB, PROJ_SUB), :]
            for p in range(PROJ_TN // MXU_N):
                acc = jnp.dot(x, wbf_ref[:, p * MXU_N:(p + 1) * MXU_N], preferred_element_type=F32)
                for hh in range(MXU_N // HEAD_DIM):
                    h = p * (MXU_N // HEAD_DIM) + hh
                    cols = slice(h * HEAD_DIM, (h + 1) * HEAD_DIM)
                    y = acc[:, hh * HEAD_DIM:(hh + 1) * HEAD_DIM]
                    if kinds[h] == "plain":
                        y = y * inv
                    else:
                        ms = jnp.mean(y * y, axis=-1, keepdims=True)
                        y = y * (inv * lax.rsqrt(ms * (inv * inv) + EPS)) * g_ref[:, cols]
                    if kinds[h] == "rope":
                        y = y * cos_ref[rows, :] + pltpu.roll(y, HALF_LANES, 1) * sin_ref[rows, :]
                    o_ref[rows, cols] = y.astype(o_ref.dtype)

    for sid, kinds in enumerate(sigs):
        @pl.when((sig_ref[n] == sid) & (m == 0))
        def _(kinds=kinds):
            cast_weights(kinds)
            compute(kinds)

        @pl.when((sig_ref[n] == sid) & (m != 0))
        def _(kinds=kinds):
            compute(kinds)


def in_proj(xg, ssq, w, li, kinds, gains, rope):
    m, k = xg.shape
    n = w.shape[2]
    heads = PROJ_TN // HEAD_DIM
    tiles = [tuple(kinds[i:i + heads]) for i in range(0, len(kinds), heads)]
    sigs = tuple(dict.fromkeys(tiles))
    sig_ids = jnp.asarray([sigs.index(t) for t in tiles], jnp.int32)
    moves, cos, sin = rope
    pos_blocks = SEQ // PROJ_TM
    return pl.pallas_call(
        functools.partial(_in_proj_body, sigs=sigs, moves=moves),
        grid=(n // PROJ_TN, m // PROJ_TM),
        in_specs=[pl.BlockSpec(memory_space=pltpu.SMEM),
                  pl.BlockSpec((PROJ_TM, k), lambda j, i: (i, 0)),
                  _ssq_spec(ssq, PROJ_TM),
                  pl.BlockSpec((None, k, PROJ_TN), lambda j, i: (li, 0, j)),
                  pl.BlockSpec((1, PROJ_TN), lambda j, i: (0, j)),
                  pl.BlockSpec((PROJ_TM, HEAD_DIM), lambda j, i: (i % pos_blocks, 0)),
                  pl.BlockSpec((PROJ_TM, HEAD_DIM), lambda j, i: (i % pos_blocks, 0))],
        out_specs=pl.BlockSpec((PROJ_TM, PROJ_TN), lambda j, i: (i, j)),
        out_shape=jax.ShapeDtypeStruct((m, n), BF16),
        scratch_shapes=[pltpu.VMEM((k, PROJ_TN), BF16), pltpu.VMEM((m, LANES), F32)],
        compiler_params=_params(),
        name="in_proj",
    )(sig_ids, xg, ssq, w, gains, cos, sin)


RES_SUB = 256
RES_WEIGHT_TILE_BYTES = 12 * 1024 * 1024
RES_WEIGHT_FULL_BYTES = 16 * 1024 * 1024


def _res_tiles(kt, n):
    if kt * n * 4 <= RES_WEIGHT_FULL_BYTES:
        return 512, n, True
    tn = n
    while kt * tn * 4 > RES_WEIGHT_TILE_BYTES and tn % (2 * MXU_N) == 0:
        tn //= 2
    return 512, tn, False


def _mm_res_body(*refs, n_in, tm, tn, emit_norm):
    a_refs = refs[:n_in]
    if emit_norm:
        w_ref, x_ref, g_ref, o_ref, xg_ref, ssq_ref, wbf_ref = refs[n_in:]
    else:
        w_ref, x_ref, o_ref, wbf_ref = refs[n_in:]

    def cast_weights():
        for p in range(tn // MXU_N):
            cols = slice(p * MXU_N, (p + 1) * MXU_N)
            wbf_ref[:, cols] = w_ref[:, cols].astype(BF16)

    def compute():
        for r in range(tm // RES_SUB):
            rows = slice(r * RES_SUB, (r + 1) * RES_SUB)
            a = a_refs[0][rows, :] if n_in == 1 else jnp.concatenate([a_ref[rows, :] for a_ref in a_refs], axis=1)
            part = None
            for p in range(tn // MXU_N):
                cols = slice(p * MXU_N, (p + 1) * MXU_N)
                acc = x_ref[rows, cols] + jnp.dot(a, wbf_ref[:, cols], preferred_element_type=F32)
                o_ref[rows, cols] = acc
                if emit_norm:
                    xg_ref[rows, cols] = (acc * g_ref[:, cols]).astype(xg_ref.dtype)
                    sq = _lane_partial_sums(acc * acc)
                    part = sq if part is None else part + sq
            if emit_norm:
                ssq_ref[rows, :] = part

    @pl.when(pl.program_id(1) == 0)
    def _():
        cast_weights()
        compute()

    @pl.when(pl.program_id(1) != 0)
    def _():
        compute()


def matmul_residual(a_list, w, li, x, g_next=None):
    m, n = x.shape
    widths = tuple(a.shape[1] for a in a_list)
    kt = sum(widths)
    assert kt == w.shape[1]
    tm, tn, whole = _res_tiles(kt, n)
    emit_norm = g_next is not None
    tile = pl.BlockSpec((tm, tn), lambda j, i: (i, j))
    in_specs = [pl.BlockSpec((tm, ka), lambda j, i: (i, 0)) for ka in widths]
    w_mode = dict(pipeline_mode=pl.Buffered(1)) if whole else {}
    in_specs += [pl.BlockSpec((None, kt, tn), lambda j, i: (li, 0, j), **w_mode), tile]
    args = [*a_list, w, x]
    out_specs, out_shape = tile, jax.ShapeDtypeStruct((m, n), F32)
    if emit_norm:
        in_specs.append(pl.BlockSpec((1, tn), lambda j, i: (0, j)))
        args.append(g_next.reshape(1, n).astype(F32))
        out_specs = [tile, tile, pl.BlockSpec((None, tm, LANES), lambda j, i: (j, i, 0))]
        out_shape = [out_shape, jax.ShapeDtypeStruct((m, n), BF16), jax.ShapeDtypeStruct((n // tn, m, LANES), F32)]
    return pl.pallas_call(
        functools.partial(_mm_res_body, n_in=len(a_list), tm=tm, tn=tn, emit_norm=emit_norm),
        grid=(n // tn, m // tm),
        in_specs=in_specs,
        out_specs=out_specs,
        out_shape=out_shape,
        scratch_shapes=[pltpu.VMEM((kt, tn), BF16)],
        compiler_params=_params(),
        name="matmul_residual",
    )(*args)


FFN_TM = 2048
FFN_TN = 512
FFN_SUB = 256


def _gate_up_body(x_ref, ssq_ref, wg_ref, wu_ref, o_ref, wgb_ref, wub_ref, inv_ref):
    row0 = pl.multiple_of(pl.program_id(1) * FFN_TM, FFN_TM)

    def cast_weights():
        for p in range(FFN_TN // MXU_N):
            cols = slice(p * MXU_N, (p + 1) * MXU_N)
            wgb_ref[:, cols] = wg_ref[:, cols].astype(BF16)
            wub_ref[:, cols] = wu_ref[:, cols].astype(BF16)

    def compute():
        for r in range(FFN_TM // FFN_SUB):
            rows = slice(r * FFN_SUB, (r + 1) * FFN_SUB)
            x = x_ref[rows, :]
            inv = jnp.concatenate([inv_ref[pl.ds(row0 + r * FFN_SUB, FFN_SUB), :]] * (MXU_N // LANES), axis=1)
            for p in range(FFN_TN // MXU_N):
                cols = slice(p * MXU_N, (p + 1) * MXU_N)
                g = jnp.dot(x, wgb_ref[:, cols], preferred_element_type=F32) * inv
                u = jnp.dot(x, wub_ref[:, cols], preferred_element_type=F32) * inv
                o_ref[rows, cols] = (g * (1.0 / (1.0 + jnp.exp(-g))) * u).astype(o_ref.dtype)

    @pl.when(pl.program_id(0) == 0)
    def _():
        _store_row_scales(ssq_ref, inv_ref, row0, FFN_TM, x_ref.shape[1])

    @pl.when(pl.program_id(1) == 0)
    def _():
        cast_weights()
        compute()

    @pl.when(pl.program_id(1) != 0)
    def _():
        compute()


def gate_up(xg, ssq, wg, wu, li):
    m, k = xg.shape
    n = wg.shape[2]
    return pl.pallas_call(
        _gate_up_body,
        grid=(n // FFN_TN, m // FFN_TM),
        in_specs=[pl.BlockSpec((FFN_TM, k), lambda j, i: (i, 0)),
                  _ssq_spec(ssq, FFN_TM),
                  pl.BlockSpec((None, k, FFN_TN), lambda j, i: (li, 0, j)),
                  pl.BlockSpec((None, k, FFN_TN), lambda j, i: (li, 0, j))],
        out_specs=pl.BlockSpec((FFN_TM, FFN_TN), lambda j, i: (i, j)),
        out_shape=jax.ShapeDtypeStruct((m, n), BF16),
        scratch_shapes=[pltpu.VMEM((k, FFN_TN), BF16), pltpu.VMEM((k, FFN_TN), BF16),
                        pltpu.VMEM((m, LANES), F32)],
        compiler_params=_params(),
        name="gate_up",
    )(xg, ssq, wg, wu)


def _two_stage_pipeline(n_blocks, per_stage, logits, finish, buf_a, buf_b):
    n_stages = n_blocks // per_stage
    assert n_stages * per_stage == n_blocks and n_stages % 2 == 0 and n_stages >= 2

    def stage(first, cur, nxt, with_next=True):
        for u in range(per_stage):
            finish(first + u, cur, u)
            if with_next:
                logits(first + per_stage + u, nxt, u)

    for u in range(per_stage):
        logits(u, buf_a, u)

    def pair(j, carry):
        first = 2 * j * per_stage
        stage(first, buf_a, buf_b)
        stage(first + per_stage, buf_b, buf_a)
        return carry

    lax.fori_loop(0, n_stages // 2 - 1, pair, 0)
    last = (n_stages - 2) * per_stage
    stage(last, buf_a, buf_b)
    stage(last + per_stage, buf_b, buf_a, with_next=False)


AB_QA, AB_KA, AB_VA = 0, A_HEADS, 2 * A_HEADS
AB_QB = 3 * A_HEADS
AB_KB = AB_QB + B_Q_HEADS
AB_VB = AB_KB + B_KV_HEADS
AB_KINDS = ("norm",) * (2 * A_HEADS) + ("plain",) * A_HEADS + ("rope",) * (B_Q_HEADS + B_KV_HEADS) \
    + ("plain",) * B_KV_HEADS
CD_QC = 0
CD_KC = C_Q_HEADS
CD_VC = CD_KC + C_KV_HEADS
CD_QD = CD_VC + C_KV_HEADS
CD_KD = CD_QD + D_Q_HEADS
CD_VD = CD_KD + D_SLOTS
CD_KINDS = ("rope",) * (C_Q_HEADS + C_KV_HEADS) + ("plain",) * C_KV_HEADS \
    + ("rope",) * (D_Q_HEADS + D_SLOTS) + ("plain",) * D_SLOTS


NA_QROWS = 2
NA_WIN_ROWS = 10
NA_Q = NA_QROWS * GRID_W
NA_KEYS = NA_WIN_ROWS * GRID_W
NA_BLOCKS = GRID_ROWS // NA_QROWS
NA_VARIANT_BLOCKS = (0, 1, 2, NA_BLOCKS - 2, NA_BLOCKS - 1)


def _na_win_start(rb):
    return np.clip(NA_QROWS * rb - NA_ROWS // 2, 0, GRID_ROWS - NA_WIN_ROWS)


N_DR = 2 * NA_ROWS - 1


def _na_row_slabs():
    dr_idx = np.full((len(NA_VARIANT_BLOCKS), NA_QROWS, NA_WIN_ROWS), N_DR, np.int32)
    for v, rb in enumerate(NA_VARIANT_BLOCKS):
        ws = _na_win_start(rb)
        for qr in range(NA_QROWS):
            qrow = NA_QROWS * rb + qr
            r0 = np.clip(qrow - NA_ROWS // 2, 0, GRID_ROWS - NA_ROWS)
            for kr in range(NA_WIN_ROWS):
                krow = ws + kr
                if r0 <= krow < r0 + NA_ROWS:
                    dr_idx[v, qr, kr] = krow - qrow + NA_ROWS - 1
    return dr_idx


def _na_column_table(rpb):
    assert rpb.shape[1:] == (A_HEADS, N_DR, 2 * NA_COLS - 1)
    nh = rpb.shape[0] * A_HEADS
    rpb = rpb.reshape(nh, N_DR, 2 * NA_COLS - 1)
    lanes = 2 * GRID_W
    ext = jnp.concatenate([rpb.astype(F32) * LOG2E, jnp.full((nh, N_DR, lanes - rpb.shape[2]), NEG, F32)], axis=-1)
    ext = jnp.roll(ext, -(NA_COLS - 1), axis=-1)
    toep = jnp.tile(ext, (1, 1, GRID_W))[:, :, :GRID_W * (lanes - 1)].reshape(nh, N_DR, GRID_W, lanes - 1)
    toep = toep[..., :GRID_W]
    cols = np.arange(GRID_W)
    c0 = np.clip(cols - NA_COLS // 2, 0, GRID_W - NA_COLS)
    col_ok = (cols[None, :] >= c0[:, None]) & (cols[None, :] < c0[:, None] + NA_COLS)
    toep = jnp.where(col_ok[None, None], toep, NEG)
    toep = jnp.concatenate([toep, jnp.full((nh, 1, GRID_W, GRID_W), NEG, F32)], axis=1)
    return jnp.concatenate([toep, toep], axis=-1)


NA_UNROLL = 8


def _na_body(q_ref, k_ref, v_ref, c_ref, o_ref, t_ref, sa_ref, sb_ref):
    left_half = lax.broadcasted_iota(jnp.int32, (GRID_W, 2 * GRID_W), 1) < GRID_W
    for v, per_q in enumerate(_na_row_slabs()):
        for qr, slabs in enumerate(per_q):
            for pair in range(NA_WIN_ROWS // 2):
                tile = jnp.where(left_half, c_ref[int(slabs[2 * pair])], c_ref[int(slabs[2 * pair + 1])])
                t_ref[v, qr * GRID_W:(qr + 1) * GRID_W, pair * 2 * GRID_W:(pair + 1) * 2 * GRID_W] = tile

    def offsets(rb):
        ws = jnp.clip(NA_QROWS * rb - NA_ROWS // 2, 0, GRID_ROWS - NA_WIN_ROWS)
        return pl.multiple_of(rb * NA_Q, NA_Q), pl.multiple_of(ws * GRID_W, GRID_W)

    def logits(rb, s_ref, u):
        qoff, koff = offsets(rb)
        variant = jnp.where(rb < 2, rb, jnp.where(rb >= NA_BLOCKS - 2, rb - (NA_BLOCKS - 5), 2))
        s_ref[u] = lax.dot_general(q_ref[pl.ds(qoff, NA_Q), :], k_ref[pl.ds(koff, NA_KEYS), :], NT_DIMS,
                                   preferred_element_type=F32) + t_ref[variant]

    def finish(rb, s_ref, u):
        qoff, koff = offsets(rb)
        s = s_ref[u]
        m = jnp.max(s, axis=-1, keepdims=True)
        p = jnp.exp2(s - m)
        l = jnp.sum(p, axis=-1, keepdims=True)
        o = jnp.dot(p.astype(BF16), v_ref[pl.ds(koff, NA_KEYS), :], preferred_element_type=F32)
        o_ref[pl.ds(qoff, NA_Q), :] = (o / l).astype(o_ref.dtype)

    _two_stage_pipeline(NA_BLOCKS, NA_UNROLL, logits, finish, sa_ref, sb_ref)


def neighbourhood_attention(h, column_table, li):
    nv = len(NA_VARIANT_BLOCKS)
    blk = (SEQ, HEAD_DIM)
    return pl.pallas_call(
        _na_body,
        grid=(BATCH, A_HEADS),
        in_specs=[pl.BlockSpec(blk, lambda b, hd: (b, AB_QA + hd)),
                  pl.BlockSpec(blk, lambda b, hd: (b, AB_KA + hd)),
                  pl.BlockSpec(blk, lambda b, hd: (b, AB_VA + hd)),
                  pl.BlockSpec((None, N_DR + 1, GRID_W, 2 * GRID_W), lambda b, hd: (li * A_HEADS + hd, 0, 0, 0))],
        out_specs=pl.BlockSpec(blk, lambda b, hd: (b, hd)),
        out_shape=jax.ShapeDtypeStruct((M_ROWS, A_HEADS * HEAD_DIM), BF16),
        scratch_shapes=[pltpu.VMEM((nv, NA_Q, NA_KEYS), F32),
                        pltpu.VMEM((NA_UNROLL, NA_Q, NA_KEYS), F32),
                        pltpu.VMEM((NA_UNROLL, NA_Q, NA_KEYS), F32)],
        compiler_params=_params(),
        name="neighbourhood_attention",
    )(h, h, h, column_table)


B_GROUP = B_Q_HEADS // B_KV_HEADS


DENSE_TQ = 512
DENSE_TK = 256


def _dense_body(q_ref, k_ref, v_ref, o_ref):
    tq = DENSE_TQ
    q = jnp.concatenate([q_ref[:, g * HEAD_DIM:(g + 1) * HEAD_DIM] for g in range(B_GROUP)], axis=0)
    ones = jnp.ones((DENSE_TK, HEAD_DIM), BF16)
    m = acc = None
    for c in range(SEQ // DENSE_TK):
        keys = slice(c * DENSE_TK, (c + 1) * DENSE_TK)
        s = lax.dot_general(q, k_ref[keys, :], NT_DIMS, preferred_element_type=F32).astype(BF16)
        v_ones = jnp.concatenate([v_ref[keys, :], ones], axis=1)
        row_max = jnp.max(s, axis=-1, keepdims=True)
        if c == 0:
            m = row_max
            acc = jnp.dot(jnp.exp2(s - m), v_ones, preferred_element_type=F32)
        else:
            m_new = jnp.maximum(m, row_max)
            alpha = jnp.exp2(m.astype(F32) - m_new.astype(F32))
            acc = alpha * acc + jnp.dot(jnp.exp2(s - m_new), v_ones, preferred_element_type=F32)
            m = m_new
    o = acc[:, :HEAD_DIM] / acc[:, HEAD_DIM:]
    for g in range(B_GROUP):
        o_ref[:, g * HEAD_DIM:(g + 1) * HEAD_DIM] = o[g * tq:(g + 1) * tq].astype(o_ref.dtype)


def dense_gqa(h):
    tq = DENSE_TQ
    nq = SEQ // tq
    gw = B_GROUP * HEAD_DIM
    assert AB_QB % B_GROUP == 0
    q_blk0 = AB_QB // B_GROUP
    blk = (SEQ, HEAD_DIM)
    return pl.pallas_call(
        _dense_body,
        grid=(BATCH, B_KV_HEADS, nq),
        in_specs=[pl.BlockSpec((tq, gw), lambda b, kv, i: (b * nq + i, q_blk0 + kv)),
                  pl.BlockSpec(blk, lambda b, kv, i: (b, AB_KB + kv)),
                  pl.BlockSpec(blk, lambda b, kv, i: (b, AB_VB + kv))],
        out_specs=pl.BlockSpec((tq, gw), lambda b, kv, i: (b * nq + i, kv)),
        out_shape=jax.ShapeDtypeStruct((M_ROWS, B_Q_HEADS * HEAD_DIM), BF16),
        compiler_params=_params(),
        name="dense_gqa",
    )(h, h, h)


QBLK = 128
N_QBLK = SEQ // QBLK


def _band_mask_table(span):
    assert span <= QBLK
    col = np.arange(QBLK + 2 * span)[None, :]
    row = np.arange(QBLK)[:, None]
    t = np.stack([np.where(np.abs(col - row + shift) <= span, 0.0, NEG) for shift in (0, -span, -2 * span)])
    return jnp.asarray(t, F32)


def _band_window(ub, blocks, span):
    start = jnp.clip(ub * QBLK - span, 0, blocks * QBLK - (QBLK + 2 * span))
    variant = jnp.where(ub == 0, 0, jnp.where(ub == blocks - 1, 2, 1))
    return start, variant


C_GROUP = C_Q_HEADS // C_KV_HEADS


C_WIN = QBLK + 2 * C_WINDOW
C_UNROLL = 2


def _window_sink_body(sink_ref, q_ref, k_ref, v_ref, mask_ref, o_ref, sa_ref, sb_ref):
    kv = pl.program_id(1)

    def logits(i, s_ref, u):
        qoff = pl.multiple_of(i * QBLK, QBLK)
        start, _ = _band_window(i, N_QBLK, C_WINDOW)
        q = q_ref[pl.ds(qoff, QBLK), :]
        qs = jnp.concatenate([q[:, g * HEAD_DIM:(g + 1) * HEAD_DIM] for g in range(C_GROUP)], axis=0)
        s_ref[u] = lax.dot_general(qs, k_ref[pl.ds(pl.multiple_of(start, QBLK), C_WIN), :], NT_DIMS,
                                   preferred_element_type=F32)

    def finish(i, s_ref, u):
        qoff = pl.multiple_of(i * QBLK, QBLK)
        start, variant = _band_window(i, N_QBLK, C_WINDOW)
        mask = mask_ref[variant]
        ps, ls = [], []
        for g in range(C_GROUP):
            sg = s_ref[u, g * QBLK:(g + 1) * QBLK, :] + mask
            sink = sink_ref[kv * C_GROUP + g] * LOG2E
            m = jnp.maximum(jnp.max(sg, axis=-1, keepdims=True), sink)
            p = jnp.exp2(sg - m)
            ls.append(jnp.sum(p, axis=-1, keepdims=True) + jnp.exp2(sink - m))
            ps.append(p.astype(BF16))
        o = jnp.dot(jnp.concatenate(ps, axis=0), v_ref[pl.ds(pl.multiple_of(start, QBLK), C_WIN), :],
                    preferred_element_type=F32)
        for g in range(C_GROUP):
            o_ref[pl.ds(qoff, QBLK), g * HEAD_DIM:(g + 1) * HEAD_DIM] = (
                o[g * QBLK:(g + 1) * QBLK] / ls[g]).astype(o_ref.dtype)

    _two_stage_pipeline(N_QBLK, C_UNROLL, logits, finish, sa_ref, sb_ref)


def window_gqa_sink(h, sink):
    gw = C_GROUP * HEAD_DIM
    assert CD_QC % C_GROUP == 0
    q_blk0 = CD_QC // C_GROUP
    blk = (SEQ, HEAD_DIM)
    return pl.pallas_call(
        _window_sink_body,
        grid=(BATCH, C_KV_HEADS),
        in_specs=[pl.BlockSpec(memory_space=pltpu.SMEM),
                  pl.BlockSpec((SEQ, gw), lambda b, kv: (b, q_blk0 + kv)),
                  pl.BlockSpec(blk, lambda b, kv: (b, CD_KC + kv)),
                  pl.BlockSpec(blk, lambda b, kv: (b, CD_VC + kv)),
                  pl.BlockSpec((3, QBLK, C_WIN), lambda b, kv: (0, 0, 0))],
        out_specs=pl.BlockSpec((SEQ, gw), lambda b, kv: (b, kv)),
        out_shape=jax.ShapeDtypeStruct((M_ROWS, C_Q_HEADS * HEAD_DIM), BF16),
        scratch_shapes=[pltpu.VMEM((C_UNROLL, C_GROUP * QBLK, C_WIN), F32),
                        pltpu.VMEM((C_UNROLL, C_GROUP * QBLK, C_WIN), F32)],
        compiler_params=_params(),
        name="window_gqa_sink",
    )(sink.astype(F32), h, h, h, _band_mask_table(C_WINDOW))


D_DILS = tuple(d for _, d in D_DILATIONS)
D_SPAN = D_DILATIONS[0][0] // 2
assert all((w // 2) // d == D_SPAN and N_QBLK % d == 0 for w, d in D_DILATIONS) and D_DILS[0] == 1
D_WIN = QBLK + 2 * D_SPAN
D_UNROLL = 2


def _dilated_body(q0_ref, q1_ref, q2_ref, k_ref, v_ref, mask_ref, o_ref,
                  stage_ref, qc_ref, kc_ref, vc_ref, og_ref, lse_ref, sa_ref, sb_ref):
    def to_class_major(dst_ref, slot, d):
        run = SEQ // d
        for rho in range(d):
            dst_ref[slot, pl.ds(rho * run, run), :] = stage_ref[pl.ds(rho, run, stride=d), :].astype(BF16)

    for src_ref, dst_ref in ((k_ref, kc_ref), (v_ref, vc_ref)):
        stage_ref[...] = src_ref[...].astype(F32)
        for g in range(1, D_GROUPS):
            to_class_major(dst_ref, g - 1, D_DILS[g])
    for g, src_ref in ((1, q1_ref), (2, q2_ref)):
        stage_ref[...] = src_ref[...].astype(F32)
        to_class_major(qc_ref, g - 1, D_DILS[g])

    def geometry(g, i):
        d = D_DILS[g]
        run_blocks = N_QBLK // d
        rho = i // run_blocks
        ub = i % run_blocks
        start, variant = _band_window(ub, run_blocks, D_SPAN)
        koff = pl.multiple_of(rho * (run_blocks * QBLK) + start, D_SPAN)
        return pl.multiple_of(i * QBLK, QBLK), koff, variant, rho, ub

    def logits(i, s_ref, u):
        for g in range(D_GROUPS):
            qoff, koff, variant, _, _ = geometry(g, i)
            if g == 0:
                q, k = q0_ref[pl.ds(qoff, QBLK), :], k_ref[pl.ds(koff, D_WIN), :]
            else:
                q, k = qc_ref[g - 1, pl.ds(qoff, QBLK), :], kc_ref[g - 1, pl.ds(koff, D_WIN), :]
            s_ref[u * D_GROUPS + g] = lax.dot_general(q, k, NT_DIMS, preferred_element_type=F32) + mask_ref[variant]

    def finish(i, s_ref, u):
        for g in range(D_GROUPS):
            d = D_DILS[g]
            qoff, koff, _, rho, ub = geometry(g, i)
            v = v_ref[pl.ds(koff, D_WIN), :] if g == 0 else vc_ref[g - 1, pl.ds(koff, D_WIN), :]
            s = s_ref[u * D_GROUPS + g]
            m = jnp.max(s, axis=-1, keepdims=True)
            p = jnp.exp2(s - m)
            l = jnp.sum(p, axis=-1, keepdims=True)
            o = jnp.dot(p.astype(BF16), v, preferred_element_type=F32) / l
            lse = jnp.broadcast_to(m + jnp.log2(l), (QBLK, HEAD_DIM))
            if d == 1:
                rows = pl.ds(qoff, QBLK)
            else:
                rows = pl.ds(rho + d * ub * QBLK, QBLK, stride=d)
            og_ref[g, rows, :] = o
            lse_ref[g, rows, :] = lse

    _two_stage_pipeline(N_QBLK, D_UNROLL, logits, finish, sa_ref, sb_ref)

    def merge(c, carry):
        rows = pl.ds(pl.multiple_of(c * QBLK, QBLK), QBLK)
        lses = [lse_ref[g, rows, :] for g in range(D_GROUPS)]
        top = functools.reduce(jnp.maximum, lses)
        ws = [jnp.exp2(x - top) for x in lses]
        num = sum(w * og_ref[g, rows, :] for g, w in enumerate(ws))
        o_ref[rows, :] = (num / sum(ws)).astype(o_ref.dtype)
        return carry

    lax.fori_loop(0, N_QBLK, merge, 0)


def dilated_attention(h):
    blk = (SEQ, HEAD_DIM)
    return pl.pallas_call(
        _dilated_body,
        grid=(BATCH, D_SLOTS),
        in_specs=[pl.BlockSpec(blk, lambda b, s: (b, CD_QD + s)),
                  pl.BlockSpec(blk, lambda b, s: (b, CD_QD + D_SLOTS + s)),
                  pl.BlockSpec(blk, lambda b, s: (b, CD_QD + 2 * D_SLOTS + s)),
                  pl.BlockSpec(blk, lambda b, s: (b, CD_KD + s)),
                  pl.BlockSpec(blk, lambda b, s: (b, CD_VD + s)),
                  pl.BlockSpec((3, QBLK, D_WIN), lambda b, s: (0, 0, 0))],
        out_specs=pl.BlockSpec(blk, lambda b, s: (b, s)),
        out_shape=jax.ShapeDtypeStruct((M_ROWS, D_SLOTS * HEAD_DIM), BF16),
        scratch_shapes=[pltpu.VMEM((SEQ, HEAD_DIM), F32),
                        pltpu.VMEM((D_GROUPS - 1, SEQ, HEAD_DIM), BF16),
                        pltpu.VMEM((D_GROUPS - 1, SEQ, HEAD_DIM), BF16),
                        pltpu.VMEM((D_GROUPS - 1, SEQ, HEAD_DIM), BF16),
                        pltpu.VMEM((D_GROUPS, SEQ, HEAD_DIM), F32),
                        pltpu.VMEM((D_GROUPS, SEQ, HEAD_DIM), F32),
                        pltpu.VMEM((D_UNROLL * D_GROUPS, QBLK, D_WIN), F32),
                        pltpu.VMEM((D_UNROLL * D_GROUPS, QBLK, D_WIN), F32)],
        compiler_params=_params(),
        name="dilated_attention",
    )(h, h, h, h, h, _band_mask_table(D_SPAN))


def _rope_tables(kind):
    pos = np.arange(SEQ, dtype=np.float64)
    cos = np.ones((SEQ, HEAD_DIM), np.float64)
    sin = np.zeros((SEQ, HEAD_DIM), np.float64)

    def fill(start, r, p, theta):
        half = r // 2
        inv = np.exp(-math.log(theta) * np.arange(half, dtype=np.float64) * (2.0 / r))
        ang = p[:, None] * inv[None, :]
        cos[:, start:start + half] = np.cos(ang)
        cos[:, start + half:start + r] = np.cos(ang)
        sin[:, start:start + half] = -np.sin(ang)
        sin[:, start + half:start + r] = np.sin(ang)

    if kind == "axial":
        hw = HEAD_DIM // 2
        fill(0, hw, np.floor(pos / GRID_W), AXIAL_THETA)
        fill(hw, hw, pos % GRID_W, AXIAL_THETA)
        moves = AXIAL_MOVES
    else:
        fill(0, ROPE_DIMS, pos, ROPE_THETA)
        moves = PARTIAL_MOVES
    perm = _moves_to_perm(moves)
    return moves, jnp.asarray(cos[:, perm], dtype=F32), jnp.asarray(sin[:, perm], dtype=F32)


def _take_static(g, perm):
    runs, start = [], 0
    for i in range(1, len(perm) + 1):
        if i == len(perm) or perm[i] != perm[i - 1] + 1:
            runs.append(g[int(perm[start]):int(perm[i - 1]) + 1])
            start = i
    return jnp.concatenate(runs)


def _gain_row(parts):
    rows = []
    for g, heads, scale, perm in parts:
        if g is None:
            g = jnp.ones((HEAD_DIM,), F32)
        g = g.astype(F32) * scale
        if perm is not None:
            g = _take_static(g, perm)
        rows.append(jnp.tile(g, heads))
    return jnp.concatenate(rows).reshape(1, -1)


def _layer_ab(x, xg, ssq, w_in, w_out, li, g_next, a_qn, a_kn, na_table, b_qn, b_kn, axial):
    perm = _moves_to_perm(axial[0])
    gains = _gain_row([(a_qn, A_HEADS, SCALE, None), (a_kn, A_HEADS, 1.0, None), (None, A_HEADS, 1.0, None),
                       (b_qn, B_Q_HEADS, SCALE, perm), (b_kn, B_KV_HEADS, 1.0, perm),
                       (None, B_KV_HEADS, 1.0, None)])
    h = in_proj(xg, ssq, w_in, li, AB_KINDS, gains, axial)
    oa = neighbourhood_attention(h, na_table, li)
    ob = dense_gqa(h)
    return matmul_residual([oa, ob], w_out, li, x, g_next)


def _layer_cd(x, xg, ssq, w_in, w_out, li, g_next, c_qn, c_kn, c_sink, d_qn, d_kn, partial):
    perm = _moves_to_perm(partial[0])
    gains = _gain_row([(c_qn, C_Q_HEADS, SCALE, perm), (c_kn, C_KV_HEADS, 1.0, perm), (None, C_KV_HEADS, 1.0, None),
                       (d_qn, D_Q_HEADS, SCALE, perm), (d_kn, D_SLOTS, 1.0, perm), (None, D_SLOTS, 1.0, None)])
    h = in_proj(xg, ssq, w_in, li, CD_KINDS, gains, partial)
    oc = window_gqa_sink(h, c_sink)
    od = dilated_attention(h)
    return matmul_residual([oc, od], w_out, li, x, g_next)


def kernel(x, attn_norm, ffn_norm, ab_w_in, ab_w_out, a_q_norm, a_k_norm, a_rpb, b_q_norm, b_k_norm,
           cd_w_in, cd_w_out, c_q_norm, c_k_norm, c_sink, d_q_norm, d_k_norm, w_gate, w_up, w_down):
    axial = _rope_tables("axial")
    partial = _rope_tables("partial")
    x = x.reshape(M_ROWS, D_MODEL)
    xg, ssq = prenorm(x, attn_norm[0])
    na_table = _na_column_table(a_rpb)
    for layer in range(DEPTH):
        j = layer // 2
        if layer % 2 == 0:
            x, xg, ssq = _layer_ab(x, xg, ssq, ab_w_in, ab_w_out, j, ffn_norm[layer], a_q_norm[j], a_k_norm[j],
                                   na_table, b_q_norm[j], b_k_norm[j], axial)
        else:
            x, xg, ssq = _layer_cd(x, xg, ssq, cd_w_in, cd_w_out, j, ffn_norm[layer], c_q_norm[j], c_k_norm[j],
                                   c_sink[j], d_q_norm[j], d_k_norm[j], partial)
        hidden = gate_up(xg, ssq, w_gate, w_up, layer)
        if layer + 1 < DEPTH:
            x, xg, ssq = matmul_residual([hidden], w_down, layer, x, attn_norm[layer + 1])
        else:
            x = matmul_residual([hidden], w_down, layer, x)
    return x.reshape(BATCH, SEQ, D_MODEL)
```

```python
import functools
import math

import numpy as np
import jax
import jax.numpy as jnp
from jax import lax
from jax.experimental import pallas as pl
from jax.experimental.pallas import tpu as pltpu

D_MODEL = 2048
BATCH = 2
SEQ = 4096
DEPTH = 4
HEAD_DIM = 128
GRID_W = 64
GRID_ROWS = SEQ // GRID_W
EPS = 1e-6
A_HEADS = 8
NA_ROWS = 8
NA_COLS = 16
B_Q_HEADS = 8
B_KV_HEADS = 2
AXIAL_THETA = 10000.0
C_Q_HEADS = 12
C_KV_HEADS = 4
C_WINDOW = 128
D_DILATIONS = ((128, 1), (512, 4), (2048, 16))
D_GROUPS = len(D_DILATIONS)
D_SLOTS = 4
D_Q_HEADS = D_GROUPS * D_SLOTS
ROPE_THETA = 500000.0
ROPE_DIMS = HEAD_DIM // 4
FFN_HIDDEN = ((-(-8 * D_MODEL // 3)) + 255) // 256 * 256
M_ROWS = BATCH * SEQ
LOG2E = math.log2(math.e)
SCALE = HEAD_DIM ** -0.5 * LOG2E
NEG = -1e30

VMEM_LIMIT_BYTES = 52 * 1024 * 1024
MXU_N = 256

F32 = jnp.float32
BF16 = jnp.bfloat16
NT_DIMS = (((1,), (1,)), ((), ()))


def _params(**kw):
    return pltpu.CompilerParams(vmem_limit_bytes=VMEM_LIMIT_BYTES, **kw)


LANES = 128


def _lane_partial_sums(sq):
    out = sq[:, :LANES]
    for i in range(1, sq.shape[1] // LANES):
        out = out + sq[:, i * LANES:(i + 1) * LANES]
    return out


ROW_SCALE_CHUNK = 256


def _store_row_scales(ssq_ref, inv_ref, row0, n_rows, dim):
    for c in range(n_rows // ROW_SCALE_CHUNK):
        rows = slice(c * ROW_SCALE_CHUNK, (c + 1) * ROW_SCALE_CHUNK)
        part = ssq_ref[0, rows, :]
        for t in range(1, ssq_ref.shape[0]):
            part = part + ssq_ref[t, rows, :]
        inv = lax.rsqrt(jnp.sum(part, axis=-1, keepdims=True) * (1.0 / dim) + EPS)
        inv_ref[pl.ds(row0 + c * ROW_SCALE_CHUNK, ROW_SCALE_CHUNK), :] = jnp.broadcast_to(
            inv, (ROW_SCALE_CHUNK, LANES))


def _ssq_spec(ssq, tm):
    return pl.BlockSpec((ssq.shape[0], tm, LANES), lambda j, i: (0, jnp.where(j == 0, i, 0), 0))


def _prenorm_body(x_ref, g_ref, xg_ref, ssq_ref):
    x = x_ref[...]
    xg_ref[...] = (x * g_ref[...]).astype(xg_ref.dtype)
    ssq_ref[...] = _lane_partial_sums(x * x)


def prenorm(x, g, tm=512):
    m, d = x.shape
    return pl.pallas_call(
        _prenorm_body,
        grid=(m // tm,),
        in_specs=[pl.BlockSpec((tm, d), lambda i: (i, 0)), pl.BlockSpec((1, d), lambda i: (0, 0))],
        out_specs=[pl.BlockSpec((tm, d), lambda i: (i, 0)), pl.BlockSpec((None, tm, LANES), lambda i: (0, i, 0))],
        out_shape=[jax.ShapeDtypeStruct((m, d), BF16), jax.ShapeDtypeStruct((1, m, LANES), F32)],
        compiler_params=_params(),
        name="prenorm",
    )(x, g.reshape(1, d).astype(F32))


HALF_LANES = HEAD_DIM // 2
AXIAL_MOVES = ((96, 32, 64), (32, 64, 96))
PARTIAL_MOVES = ((112, 16, 64), (48, 64, 80))


def _moves_to_perm(moves):
    perm = np.arange(HEAD_DIM)
    for shift, lo, hi in moves:
        perm[lo:hi] = (np.arange(lo, hi) - shift) % HEAD_DIM
    assert sorted(perm.tolist()) == list(range(HEAD_DIM))
    return perm


def _permute_lanes(w, moves):
    lane = lax.broadcasted_iota(jnp.int32, w.shape, 1)
    out = w
    for shift, lo, hi in moves:
        out = jnp.where((lane >= lo) & (lane < hi), pltpu.roll(w, shift, 1), out)
    return out


PROJ_TM = 2048
PROJ_TN = 512
PROJ_SUB = 256


def _in_proj_body(sig_ref, x_ref, ssq_ref, w_ref, g_ref, cos_ref, sin_ref, o_ref, wbf_ref, inv_ref, *,
                  sigs, moves):
    n = pl.program_id(0)
    m = pl.program_id(1)
    heads = PROJ_TN // HEAD_DIM
    row0 = pl.multiple_of(m * PROJ_TM, PROJ_TM)

    @pl.when(n == 0)
    def _():
        _store_row_scales(ssq_ref, inv_ref, row0, PROJ_TM, x_ref.shape[1])

    def cast_weights(kinds):
        for h in range(heads):
            cols = slice(h * HEAD_DIM, (h + 1) * HEAD_DIM)
            w = w_ref[:, cols]
            if kinds[h] == "rope":
                w = _permute_lanes(w, moves)
            wbf_ref[:, cols] = w.astype(BF16)

    def compute(kinds):
        for r in range(PROJ_TM // PROJ_SUB):
            rows = slice(r * PROJ_SUB, (r + 1) * PROJ_SUB)
            x = x_ref[rows, :]
            inv = inv_ref[pl.ds(row0 + r * PROJ_SUB, PROJ_SUB), :]
            for p in range(PROJ_TN // MXU_N):
                acc = jnp.dot(x, wbf_ref[:, p * MXU_N:(p + 1) * MXU_N], preferred_element_type=F32)
                for hh in range(MXU_N // HEAD_DIM):
                    h = p * (MXU_N // HEAD_DIM) + hh
                    cols = slice(h * HEAD_DIM, (h + 1) * HEAD_DIM)
                    y = acc[:, hh * HEAD_DIM:(hh + 1) * HEAD_DIM]
                    if kinds[h] == "plain":
                        y = y * inv
                    else:
                        ms = jnp.mean(y * y, axis=-1, keepdims=True)
                        y = y * (inv * lax.rsqrt(ms * (inv * inv) + EPS)) * g_ref[:, cols]
                    if kinds[h] == "rope":
                        y = y * cos_ref[rows, :] + pltpu.roll(y, HALF_LANES, 1) * sin_ref[rows, :]
                    o_ref[rows, cols] = y.astype(o_ref.dtype)

    for sid, kinds in enumerate(sigs):
        @pl.when(sig_ref[n] == sid)
        def _(kinds=kinds):
            @pl.when(m == 0)
            def _():
                cast_weights(kinds)

            compute(kinds)


def in_proj(xg, ssq, w, li, kinds, gains, rope):
    m, k = xg.shape
    n = w.shape[2]
    heads = PROJ_TN // HEAD_DIM
    tiles = [tuple(kinds[i:i + heads]) for i in range(0, len(kinds), heads)]
    sigs = tuple(dict.fromkeys(tiles))
    sig_ids = jnp.asarray([sigs.index(t) for t in tiles], jnp.int32)
    moves, cos, sin = rope
    pos_blocks = SEQ // PROJ_TM
    return pl.pallas_call(
        functools.partial(_in_proj_body, sigs=sigs, moves=moves),
        grid=(n // PROJ_TN, m // PROJ_TM),
        in_specs=[pl.BlockSpec(memory_space=pltpu.SMEM),
                  pl.BlockSpec((PROJ_TM, k), lambda j, i: (i, 0)),
                  _ssq_spec(ssq, PROJ_TM),
                  pl.BlockSpec((None, k, PROJ_TN), lambda j, i: (li, 0, j)),
                  pl.BlockSpec((1, PROJ_TN), lambda j, i: (0, j)),
                  pl.BlockSpec((PROJ_TM, HEAD_DIM), lambda j, i: (i % pos_blocks, 0)),
                  pl.BlockSpec((PROJ_TM, HEAD_DIM), lambda j, i: (i % pos_blocks, 0))],
        out_specs=pl.BlockSpec((PROJ_TM, PROJ_TN), lambda j, i: (i, j)),
        out_shape=jax.ShapeDtypeStruct((m, n), BF16),
        scratch_shapes=[pltpu.VMEM((k, PROJ_TN), BF16), pltpu.VMEM((m, LANES), F32)],
        compiler_params=_params(),
        name="in_proj",
    )(sig_ids, xg, ssq, w, gains, cos, sin)


RES_SUB = 256
RES_WEIGHT_TILE_BYTES = 12 * 1024 * 1024
RES_WEIGHT_FULL_BYTES = 16 * 1024 * 1024


def _res_tiles(kt, n):
    if kt * n * 4 <= RES_WEIGHT_FULL_BYTES:
        return 512, n, True
    tn = n
    while kt * tn * 4 > RES_WEIGHT_TILE_BYTES and tn % (2 * MXU_N) == 0:
        tn //= 2
    return 512, tn, False


def _mm_res_body(*refs, n_in, tm, tn, emit_norm):
    a_refs = refs[:n_in]
    if emit_norm:
        w_ref, x_ref, g_ref, o_ref, xg_ref, ssq_ref, wbf_ref = refs[n_in:]
    else:
        w_ref, x_ref, o_ref, wbf_ref = refs[n_in:]

    @pl.when(pl.program_id(1) == 0)
    def _():
        wbf_ref[...] = w_ref[...].astype(BF16)

    for r in range(tm // RES_SUB):
        rows = slice(r * RES_SUB, (r + 1) * RES_SUB)
        a = a_refs[0][rows, :] if n_in == 1 else jnp.concatenate([a_ref[rows, :] for a_ref in a_refs], axis=1)
        part = None
        for p in range(tn // MXU_N):
            cols = slice(p * MXU_N, (p + 1) * MXU_N)
            acc = x_ref[rows, cols] + jnp.dot(a, wbf_ref[:, cols], preferred_element_type=F32)
            o_ref[rows, cols] = acc
            if emit_norm:
                xg_ref[rows, cols] = (acc * g_ref[:, cols]).astype(xg_ref.dtype)
                sq = _lane_partial_sums(acc * acc)
                part = sq if part is None else part + sq
        if emit_norm:
            ssq_ref[rows, :] = part


RING_SLOTS = 3


def _mm_res_ring_body(*refs, tm, tn, emit_norm, n_m, n_steps):
    if emit_norm:
        a_hbm, w_ref, x_ref, g_ref, o_ref, xg_ref, ssq_ref, wbf_ref, abuf, sem = refs
    else:
        a_hbm, w_ref, x_ref, o_ref, wbf_ref, abuf, sem = refs
    step = pl.program_id(0) * n_m + pl.program_id(1)

    def tile_copy(t, slot):
        row0 = pl.multiple_of(lax.rem(t, n_m) * tm, tm)
        return pltpu.make_async_copy(a_hbm.at[pl.ds(row0, tm), :], abuf.at[slot], sem.at[slot])

    @pl.when(step == 0)
    def _():
        tile_copy(0, 0).start()
        tile_copy(1, 1).start()

    @pl.when(step + 2 < n_steps)
    def _():
        tile_copy(step + 2, lax.rem(step + 2, RING_SLOTS)).start()

    @pl.when(pl.program_id(1) == 0)
    def _():
        wbf_ref[...] = w_ref[...].astype(BF16)

    slot = lax.rem(step, RING_SLOTS)
    tile_copy(step, slot).wait()
    for r in range(tm // RES_SUB):
        rows = slice(r * RES_SUB, (r + 1) * RES_SUB)
        a = abuf[slot, rows, :]
        part = None
        for p in range(tn // MXU_N):
            cols = slice(p * MXU_N, (p + 1) * MXU_N)
            acc = x_ref[rows, cols] + jnp.dot(a, wbf_ref[:, cols], preferred_element_type=F32)
            o_ref[rows, cols] = acc
            if emit_norm:
                xg_ref[rows, cols] = (acc * g_ref[:, cols]).astype(xg_ref.dtype)
                sq = _lane_partial_sums(acc * acc)
                part = sq if part is None else part + sq
        if emit_norm:
            ssq_ref[rows, :] = part


def matmul_residual(a_list, w, li, x, g_next=None):
    m, n = x.shape
    widths = tuple(a.shape[1] for a in a_list)
    kt = sum(widths)
    assert kt == w.shape[1]
    tm, tn, whole = _res_tiles(kt, n)
    emit_norm = g_next is not None
    tile = pl.BlockSpec((tm, tn), lambda j, i: (i, j))
    in_specs = [pl.BlockSpec((tm, ka), lambda j, i: (i, 0)) for ka in widths]
    w_mode = dict(pipeline_mode=pl.Buffered(1)) if whole else {}
    in_specs += [pl.BlockSpec((None, kt, tn), lambda j, i: (li, 0, j), **w_mode), tile]
    args = [*a_list, w, x]
    out_specs, out_shape = tile, jax.ShapeDtypeStruct((m, n), F32)
    if emit_norm:
        in_specs.append(pl.BlockSpec((1, tn), lambda j, i: (0, j)))
        args.append(g_next.reshape(1, n).astype(F32))
        out_specs = [tile, tile, pl.BlockSpec((None, tm, LANES), lambda j, i: (j, i, 0))]
        out_shape = [out_shape, jax.ShapeDtypeStruct((m, n), BF16), jax.ShapeDtypeStruct((n // tn, m, LANES), F32)]
    if not whole and len(a_list) == 1:
        n_m = m // tm
        return pl.pallas_call(
            functools.partial(_mm_res_ring_body, tm=tm, tn=tn, emit_norm=emit_norm, n_m=n_m,
                              n_steps=(n // tn) * n_m),
            grid=(n // tn, n_m),
            in_specs=[pl.BlockSpec(memory_space=pl.ANY)] + in_specs[1:],
            out_specs=out_specs,
            out_shape=out_shape,
            scratch_shapes=[pltpu.VMEM((kt, tn), BF16), pltpu.VMEM((RING_SLOTS, tm, kt), BF16),
                            pltpu.SemaphoreType.DMA((RING_SLOTS,))],
            compiler_params=_params(dimension_semantics=("arbitrary", "arbitrary")),
            name="matmul_residual_ring",
        )(*args)
    return pl.pallas_call(
        functools.partial(_mm_res_body, n_in=len(a_list), tm=tm, tn=tn, emit_norm=emit_norm),
        grid=(n // tn, m // tm),
        in_specs=in_specs,
        out_specs=out_specs,
        out_shape=out_shape,
        scratch_shapes=[pltpu.VMEM((kt, tn), BF16)],
        compiler_params=_params(),
        name="matmul_residual",
    )(*args)


FFN_TM = 2048
FFN_TN = 512
FFN_SUB = 256


def _gate_up_body(x_ref, ssq_ref, wg_ref, wu_ref, o_ref, wgb_ref, wub_ref, inv_ref):
    row0 = pl.multiple_of(pl.program_id(1) * FFN_TM, FFN_TM)

    @pl.when(pl.program_id(1) == 0)
    def _():
        wgb_ref[...] = wg_ref[...].astype(BF16)
        wub_ref[...] = wu_ref[...].astype(BF16)

    @pl.when(pl.program_id(0) == 0)
    def _():
        _store_row_scales(ssq_ref, inv_ref, row0, FFN_TM, x_ref.shape[1])

    for r in range(FFN_TM // FFN_SUB):
        rows = slice(r * FFN_SUB, (r + 1) * FFN_SUB)
        x = x_ref[rows, :]
        inv = jnp.concatenate([inv_ref[pl.ds(row0 + r * FFN_SUB, FFN_SUB), :]] * (MXU_N // LANES), axis=1)
        for p in range(FFN_TN // MXU_N):
            cols = slice(p * MXU_N, (p + 1) * MXU_N)
            g = jnp.dot(x, wgb_ref[:, cols], preferred_element_type=F32) * inv
            u = jnp.dot(x, wub_ref[:, cols], preferred_element_type=F32) * inv
            o_ref[rows, cols] = (g * (1.0 / (1.0 + jnp.exp(-g))) * u).astype(o_ref.dtype)


def gate_up(xg, ssq, wg, wu, li):
    m, k = xg.shape
    n = wg.shape[2]
    return pl.pallas_call(
        _gate_up_body,
        grid=(n // FFN_TN, m // FFN_TM),
        in_specs=[pl.BlockSpec((FFN_TM, k), lambda j, i: (i, 0)),
                  _ssq_spec(ssq, FFN_TM),
                  pl.BlockSpec((None, k, FFN_TN), lambda j, i: (li, 0, j)),
                  pl.BlockSpec((None, k, FFN_TN), lambda j, i: (li, 0, j))],
        out_specs=pl.BlockSpec((FFN_TM, FFN_TN), lambda j, i: (i, j)),
        out_shape=jax.ShapeDtypeStruct((m, n), BF16),
        scratch_shapes=[pltpu.VMEM((k, FFN_TN), BF16), pltpu.VMEM((k, FFN_TN), BF16),
                        pltpu.VMEM((m, LANES), F32)],
        compiler_params=_params(),
        name="gate_up",
    )(xg, ssq, wg, wu)


def _two_stage_pipeline(n_blocks, per_stage, logits, finish, buf_a, buf_b):
    n_stages = n_blocks // per_stage
    assert n_stages * per_stage == n_blocks and n_stages % 2 == 0 and n_stages >= 2

    def stage(first, cur, nxt, with_next=True):
        for u in range(per_stage):
            finish(first + u, cur, u)
            if with_next:
                logits(first + per_stage + u, nxt, u)

    for u in range(per_stage):
        logits(u, buf_a, u)

    def pair(j, carry):
        first = 2 * j * per_stage
        stage(first, buf_a, buf_b)
        stage(first + per_stage, buf_b, buf_a)
        return carry

    lax.fori_loop(0, n_stages // 2 - 1, pair, 0)
    last = (n_stages - 2) * per_stage
    stage(last, buf_a, buf_b)
    stage(last + per_stage, buf_b, buf_a, with_next=False)


AB_QA, AB_KA, AB_VA = 0, A_HEADS, 2 * A_HEADS
AB_QB = 3 * A_HEADS
AB_KB = AB_QB + B_Q_HEADS
AB_VB = AB_KB + B_KV_HEADS
AB_KINDS = ("norm",) * (2 * A_HEADS) + ("plain",) * A_HEADS + ("rope",) * (B_Q_HEADS + B_KV_HEADS) \
    + ("plain",) * B_KV_HEADS
CD_QC = 0
CD_KC = C_Q_HEADS
CD_VC = CD_KC + C_KV_HEADS
CD_QD = CD_VC + C_KV_HEADS
CD_KD = CD_QD + D_Q_HEADS
CD_VD = CD_KD + D_SLOTS
CD_KINDS = ("rope",) * (C_Q_HEADS + C_KV_HEADS) + ("plain",) * C_KV_HEADS \
    + ("rope",) * (D_Q_HEADS + D_SLOTS) + ("plain",) * D_SLOTS


NA_QROWS = 2
NA_WIN_ROWS = 10
NA_Q = NA_QROWS * GRID_W
NA_KEYS = NA_WIN_ROWS * GRID_W
NA_BLOCKS = GRID_ROWS // NA_QROWS
NA_VARIANT_BLOCKS = (0, 1, 2, NA_BLOCKS - 2, NA_BLOCKS - 1)


def _na_win_start(rb):
    return np.clip(NA_QROWS * rb - NA_ROWS // 2, 0, GRID_ROWS - NA_WIN_ROWS)


N_DR = 2 * NA_ROWS - 1


def _na_row_slabs():
    dr_idx = np.full((len(NA_VARIANT_BLOCKS), NA_QROWS, NA_WIN_ROWS), N_DR, np.int32)
    for v, rb in enumerate(NA_VARIANT_BLOCKS):
        ws = _na_win_start(rb)
        for qr in range(NA_QROWS):
            qrow = NA_QROWS * rb + qr
            r0 = np.clip(qrow - NA_ROWS // 2, 0, GRID_ROWS - NA_ROWS)
            for kr in range(NA_WIN_ROWS):
                krow = ws + kr
                if r0 <= krow < r0 + NA_ROWS:
                    dr_idx[v, qr, kr] = krow - qrow + NA_ROWS - 1
    return dr_idx


def _na_column_table(rpb):
    assert rpb.shape[1:] == (A_HEADS, N_DR, 2 * NA_COLS - 1)
    nh = rpb.shape[0] * A_HEADS
    rpb = rpb.reshape(nh, N_DR, 2 * NA_COLS - 1)
    lanes = 2 * GRID_W
    ext = jnp.concatenate([rpb.astype(F32) * LOG2E, jnp.full((nh, N_DR, lanes - rpb.shape[2]), NEG, F32)], axis=-1)
    ext = jnp.roll(ext, -(NA_COLS - 1), axis=-1)
    toep = jnp.tile(ext, (1, 1, GRID_W))[:, :, :GRID_W * (lanes - 1)].reshape(nh, N_DR, GRID_W, lanes - 1)
    toep = toep[..., :GRID_W]
    cols = np.arange(GRID_W)
    c0 = np.clip(cols - NA_COLS // 2, 0, GRID_W - NA_COLS)
    col_ok = (cols[None, :] >= c0[:, None]) & (cols[None, :] < c0[:, None] + NA_COLS)
    toep = jnp.where(col_ok[None, None], toep, NEG)
    toep = jnp.concatenate([toep, jnp.full((nh, 1, GRID_W, GRID_W), NEG, F32)], axis=1)
    return jnp.concatenate([toep, toep], axis=-1)


NA_UNROLL = 8


def _na_body(q_ref, k_ref, v_ref, c_ref, o_ref, t_ref, sa_ref, sb_ref):
    left_half = lax.broadcasted_iota(jnp.int32, (GRID_W, 2 * GRID_W), 1) < GRID_W
    for v, per_q in enumerate(_na_row_slabs()):
        for qr, slabs in enumerate(per_q):
            for pair in range(NA_WIN_ROWS // 2):
                tile = jnp.where(left_half, c_ref[int(slabs[2 * pair])], c_ref[int(slabs[2 * pair + 1])])
                t_ref[v, qr * GRID_W:(qr + 1) * GRID_W, pair * 2 * GRID_W:(pair + 1) * 2 * GRID_W] = tile

    def offsets(rb):
        ws = jnp.clip(NA_QROWS * rb - NA_ROWS // 2, 0, GRID_ROWS - NA_WIN_ROWS)
        return pl.multiple_of(rb * NA_Q, NA_Q), pl.multiple_of(ws * GRID_W, GRID_W)

    def logits(rb, s_ref, u):
        qoff, koff = offsets(rb)
        variant = jnp.where(rb < 2, rb, jnp.where(rb >= NA_BLOCKS - 2, rb - (NA_BLOCKS - 5), 2))
        s_ref[u] = lax.dot_general(q_ref[pl.ds(qoff, NA_Q), :], k_ref[pl.ds(koff, NA_KEYS), :], NT_DIMS,
                                   preferred_element_type=F32) + t_ref[variant]

    def finish(rb, s_ref, u):
        qoff, koff = offsets(rb)
        s = s_ref[u]
        m = jnp.max(s, axis=-1, keepdims=True)
        p = jnp.exp2(s - m)
        l = jnp.sum(p, axis=-1, keepdims=True)
        o = jnp.dot(p.astype(BF16), v_ref[pl.ds(koff, NA_KEYS), :], preferred_element_type=F32)
        o_ref[pl.ds(qoff, NA_Q), :] = (o / l).astype(o_ref.dtype)

    _two_stage_pipeline(NA_BLOCKS, NA_UNROLL, logits, finish, sa_ref, sb_ref)


def neighbourhood_attention(h, column_table, li):
    nv = len(NA_VARIANT_BLOCKS)
    blk = (SEQ, HEAD_DIM)
    return pl.pallas_call(
        _na_body,
        grid=(BATCH, A_HEADS),
        in_specs=[pl.BlockSpec(blk, lambda b, hd: (b, AB_QA + hd)),
                  pl.BlockSpec(blk, lambda b, hd: (b, AB_KA + hd)),
                  pl.BlockSpec(blk, lambda b, hd: (b, AB_VA + hd)),
                  pl.BlockSpec((None, N_DR + 1, GRID_W, 2 * GRID_W), lambda b, hd: (li * A_HEADS + hd, 0, 0, 0))],
        out_specs=pl.BlockSpec(blk, lambda b, hd: (b, hd)),
        out_shape=jax.ShapeDtypeStruct((M_ROWS, A_HEADS * HEAD_DIM), BF16),
        scratch_shapes=[pltpu.VMEM((nv, NA_Q, NA_KEYS), F32),
                        pltpu.VMEM((NA_UNROLL, NA_Q, NA_KEYS), F32),
                        pltpu.VMEM((NA_UNROLL, NA_Q, NA_KEYS), F32)],
        compiler_params=_params(),
        name="neighbourhood_attention",
    )(h, h, h, column_table)


B_GROUP = B_Q_HEADS // B_KV_HEADS


DENSE_TQ = 512
DENSE_TK = 256


def _dense_body(q_ref, k_ref, v_ref, o_ref):
    tq = DENSE_TQ
    q = jnp.concatenate([q_ref[:, g * HEAD_DIM:(g + 1) * HEAD_DIM] for g in range(B_GROUP)], axis=0)
    ones = jnp.ones((DENSE_TK, HEAD_DIM), BF16)
    m = acc = None
    for c in range(SEQ // DENSE_TK):
        keys = slice(c * DENSE_TK, (c + 1) * DENSE_TK)
        s = lax.dot_general(q, k_ref[keys, :], NT_DIMS, preferred_element_type=F32).astype(BF16)
        v_ones = jnp.concatenate([v_ref[keys, :], ones], axis=1)
        row_max = jnp.max(s, axis=-1, keepdims=True)
        if c == 0:
            m = row_max
            acc = jnp.dot(jnp.exp2(s - m), v_ones, preferred_element_type=F32)
        else:
            m_new = jnp.maximum(m, row_max)
            alpha = jnp.exp2(m.astype(F32) - m_new.astype(F32))
            acc = alpha * acc + jnp.dot(jnp.exp2(s - m_new), v_ones, preferred_element_type=F32)
            m = m_new
    o = acc[:, :HEAD_DIM] / acc[:, HEAD_DIM:]
    for g in range(B_GROUP):
        o_ref[:, g * HEAD_DIM:(g + 1) * HEAD_DIM] = o[g * tq:(g + 1) * tq].astype(o_ref.dtype)


def dense_gqa(h):
    tq = DENSE_TQ
    nq = SEQ // tq
    gw = B_GROUP * HEAD_DIM
    assert AB_QB % B_GROUP == 0
    q_blk0 = AB_QB // B_GROUP
    blk = (SEQ, HEAD_DIM)
    return pl.pallas_call(
        _dense_body,
        grid=(BATCH, B_KV_HEADS, nq),
        in_specs=[pl.BlockSpec((tq, gw), lambda b, kv, i: (b * nq + i, q_blk0 + kv)),
                  pl.BlockSpec(blk, lambda b, kv, i: (b, AB_KB + kv)),
                  pl.BlockSpec(blk, lambda b, kv, i: (b, AB_VB + kv))],
        out_specs=pl.BlockSpec((tq, gw), lambda b, kv, i: (b * nq + i, kv)),
        out_shape=jax.ShapeDtypeStruct((M_ROWS, B_Q_HEADS * HEAD_DIM), BF16),
        compiler_params=_params(),
        name="dense_gqa",
    )(h, h, h)


QBLK = 128
N_QBLK = SEQ // QBLK


def _band_mask_table(span):
    assert span <= QBLK
    col = np.arange(QBLK + 2 * span)[None, :]
    row = np.arange(QBLK)[:, None]
    t = np.stack([np.where(np.abs(col - row + shift) <= span, 0.0, NEG) for shift in (0, -span, -2 * span)])
    return jnp.asarray(t, F32)


def _band_window(ub, blocks, span):
    start = jnp.clip(ub * QBLK - span, 0, blocks * QBLK - (QBLK + 2 * span))
    variant = jnp.where(ub == 0, 0, jnp.where(ub == blocks - 1, 2, 1))
    return start, variant


C_GROUP = C_Q_HEADS // C_KV_HEADS


C_WIN = QBLK + 2 * C_WINDOW
C_UNROLL = 2


def _window_sink_body(sink_ref, q_ref, k_ref, v_ref, mask_ref, o_ref, sa_ref, sb_ref):
    kv = pl.program_id(1)

    def logits(i, s_ref, u):
        qoff = pl.multiple_of(i * QBLK, QBLK)
        start, _ = _band_window(i, N_QBLK, C_WINDOW)
        q = q_ref[pl.ds(qoff, QBLK), :]
        qs = jnp.concatenate([q[:, g * HEAD_DIM:(g + 1) * HEAD_DIM] for g in range(C_GROUP)], axis=0)
        s_ref[u] = lax.dot_general(qs, k_ref[pl.ds(pl.multiple_of(start, QBLK), C_WIN), :], NT_DIMS,
                                   preferred_element_type=F32)

    def finish(i, s_ref, u):
        qoff = pl.multiple_of(i * QBLK, QBLK)
        start, variant = _band_window(i, N_QBLK, C_WINDOW)
        mask = mask_ref[variant]
        ps, ls = [], []
        for g in range(C_GROUP):
            sg = s_ref[u, g * QBLK:(g + 1) * QBLK, :] + mask
            sink = sink_ref[kv * C_GROUP + g] * LOG2E
            m = jnp.maximum(jnp.max(sg, axis=-1, keepdims=True), sink)
            p = jnp.exp2(sg - m)
            ls.append(jnp.sum(p, axis=-1, keepdims=True) + jnp.exp2(sink - m))
            ps.append(p.astype(BF16))
        o = jnp.dot(jnp.concatenate(ps, axis=0), v_ref[pl.ds(pl.multiple_of(start, QBLK), C_WIN), :],
                    preferred_element_type=F32)
        for g in range(C_GROUP):
            o_ref[pl.ds(qoff, QBLK), g * HEAD_DIM:(g + 1) * HEAD_DIM] = (
                o[g * QBLK:(g + 1) * QBLK] / ls[g]).astype(o_ref.dtype)

    _two_stage_pipeline(N_QBLK, C_UNROLL, logits, finish, sa_ref, sb_ref)


def window_gqa_sink(h, sink):
    gw = C_GROUP * HEAD_DIM
    assert CD_QC % C_GROUP == 0
    q_blk0 = CD_QC // C_GROUP
    blk = (SEQ, HEAD_DIM)
    return pl.pallas_call(
        _window_sink_body,
        grid=(BATCH, C_KV_HEADS),
        in_specs=[pl.BlockSpec(memory_space=pltpu.SMEM),
                  pl.BlockSpec((SEQ, gw), lambda b, kv: (b, q_blk0 + kv)),
                  pl.BlockSpec(blk, lambda b, kv: (b, CD_KC + kv)),
                  pl.BlockSpec(blk, lambda b, kv: (b, CD_VC + kv)),
                  pl.BlockSpec((3, QBLK, C_WIN), lambda b, kv: (0, 0, 0))],
        out_specs=pl.BlockSpec((SEQ, gw), lambda b, kv: (b, kv)),
        out_shape=jax.ShapeDtypeStruct((M_ROWS, C_Q_HEADS * HEAD_DIM), BF16),
        scratch_shapes=[pltpu.VMEM((C_UNROLL, C_GROUP * QBLK, C_WIN), F32),
                        pltpu.VMEM((C_UNROLL, C_GROUP * QBLK, C_WIN), F32)],
        compiler_params=_params(),
        name="window_gqa_sink",
    )(sink.astype(F32), h, h, h, _band_mask_table(C_WINDOW))


D_DILS = tuple(d for _, d in D_DILATIONS)
D_SPAN = D_DILATIONS[0][0] // 2
assert all((w // 2) // d == D_SPAN and N_QBLK % d == 0 for w, d in D_DILATIONS) and D_DILS[0] == 1
D_WIN = QBLK + 2 * D_SPAN
D_UNROLL = 2


def _dilated_body(q0_ref, q1_ref, q2_ref, k_ref, v_ref, mask_ref, o_ref,
                  stage_ref, qc_ref, kc_ref, vc_ref, og_ref, lse_ref, sa_ref, sb_ref):
    def to_class_major(dst_ref, slot, d):
        run = SEQ // d
        for rho in range(d):
            dst_ref[slot, pl.ds(rho * run, run), :] = stage_ref[pl.ds(rho, run, stride=d), :].astype(BF16)

    for src_ref, dst_ref in ((k_ref, kc_ref), (v_ref, vc_ref)):
        stage_ref[...] = src_ref[...].astype(F32)
        for g in range(1, D_GROUPS):
            to_class_major(dst_ref, g - 1, D_DILS[g])
    for g, src_ref in ((1, q1_ref), (2, q2_ref)):
        stage_ref[...] = src_ref[...].astype(F32)
        to_class_major(qc_ref, g - 1, D_DILS[g])

    def geometry(g, i):
        d = D_DILS[g]
        run_blocks = N_QBLK // d
        rho = i // run_blocks
        ub = i % run_blocks
        start, variant = _band_window(ub, run_blocks, D_SPAN)
        koff = pl.multiple_of(rho * (run_blocks * QBLK) + start, D_SPAN)
        return pl.multiple_of(i * QBLK, QBLK), koff, variant, rho, ub

    def logits(i, s_ref, u):
        for g in range(D_GROUPS):
            qoff, koff, variant, _, _ = geometry(g, i)
            if g == 0:
                q, k = q0_ref[pl.ds(qoff, QBLK), :], k_ref[pl.ds(koff, D_WIN), :]
            else:
                q, k = qc_ref[g - 1, pl.ds(qoff, QBLK), :], kc_ref[g - 1, pl.ds(koff, D_WIN), :]
            s_ref[u * D_GROUPS + g] = lax.dot_general(q, k, NT_DIMS, preferred_element_type=F32) + mask_ref[variant]

    def finish(i, s_ref, u):
        for g in range(D_GROUPS):
            d = D_DILS[g]
            qoff, koff, _, rho, ub = geometry(g, i)
            v = v_ref[pl.ds(koff, D_WIN), :] if g == 0 else vc_ref[g - 1, pl.ds(koff, D_WIN), :]
            s = s_ref[u * D_GROUPS + g]
            m = jnp.max(s, axis=-1, keepdims=True)
            p = jnp.exp2(s - m)
            l = jnp.sum(p, axis=-1, keepdims=True)
            o = jnp.dot(p.astype(BF16), v, preferred_element_type=F32) / l
            lse = jnp.broadcast_to(m + jnp.log2(l), (QBLK, HEAD_DIM))
            if d == 1:
                rows = pl.ds(qoff, QBLK)
            else:
                rows = pl.ds(rho + d * ub * QBLK, QBLK, stride=d)
            og_ref[g, rows, :] = o
            lse_ref[g, rows, :] = lse

    _two_stage_pipeline(N_QBLK, D_UNROLL, logits, finish, sa_ref, sb_ref)

    def merge(c, carry):
        rows = pl.ds(pl.multiple_of(c * QBLK, QBLK), QBLK)
        lses = [lse_ref[g, rows, :] for g in range(D_GROUPS)]
        top = functools.reduce(jnp.maximum, lses)
        ws = [jnp.exp2(x - top) for x in lses]
        num = sum(w * og_ref[g, rows, :] for g, w in enumerate(ws))
        o_ref[rows, :] = (num / sum(ws)).astype(o_ref.dtype)
        return carry

    lax.fori_loop(0, N_QBLK, merge, 0)


def dilated_attention(h):
    blk = (SEQ, HEAD_DIM)
    return pl.pallas_call(
        _dilated_body,
        grid=(BATCH, D_SLOTS),
        in_specs=[pl.BlockSpec(blk, lambda b, s: (b, CD_QD + s)),
                  pl.BlockSpec(blk, lambda b, s: (b, CD_QD + D_SLOTS + s)),
                  pl.BlockSpec(blk, lambda b, s: (b, CD_QD + 2 * D_SLOTS + s)),
                  pl.BlockSpec(blk, lambda b, s: (b, CD_KD + s)),
                  pl.BlockSpec(blk, lambda b, s: (b, CD_VD + s)),
                  pl.BlockSpec((3, QBLK, D_WIN), lambda b, s: (0, 0, 0))],
        out_specs=pl.BlockSpec(blk, lambda b, s: (b, s)),
        out_shape=jax.ShapeDtypeStruct((M_ROWS, D_SLOTS * HEAD_DIM), BF16),
        scratch_shapes=[pltpu.VMEM((SEQ, HEAD_DIM), F32),
                        pltpu.VMEM((D_GROUPS - 1, SEQ, HEAD_DIM), BF16),
                        pltpu.VMEM((D_GROUPS - 1, SEQ, HEAD_DIM), BF16),
                        pltpu.VMEM((D_GROUPS - 1, SEQ, HEAD_DIM), BF16),
                        pltpu.VMEM((D_GROUPS, SEQ, HEAD_DIM), F32),
                        pltpu.VMEM((D_GROUPS, SEQ, HEAD_DIM), F32),
                        pltpu.VMEM((D_UNROLL * D_GROUPS, QBLK, D_WIN), F32),
                        pltpu.VMEM((D_UNROLL * D_GROUPS, QBLK, D_WIN), F32)],
        compiler_params=_params(),
        name="dilated_attention",
    )(h, h, h, h, h, _band_mask_table(D_SPAN))


def _rope_tables(kind):
    pos = np.arange(SEQ, dtype=np.float64)
    cos = np.ones((SEQ, HEAD_DIM), np.float64)
    sin = np.zeros((SEQ, HEAD_DIM), np.float64)

    def fill(start, r, p, theta):
        half = r // 2
        inv = np.exp(-math.log(theta) * np.arange(half, dtype=np.float64) * (2.0 / r))
        ang = p[:, None] * inv[None, :]
        cos[:, start:start + half] = np.cos(ang)
        cos[:, start + half:start + r] = np.cos(ang)
        sin[:, start:start + half] = -np.sin(ang)
        sin[:, start + half:start + r] = np.sin(ang)

    if kind == "axial":
        hw = HEAD_DIM // 2
        fill(0, hw, np.floor(pos / GRID_W), AXIAL_THETA)
        fill(hw, hw, pos % GRID_W, AXIAL_THETA)
        moves = AXIAL_MOVES
    else:
        fill(0, ROPE_DIMS, pos, ROPE_THETA)
        moves = PARTIAL_MOVES
    perm = _moves_to_perm(moves)
    return moves, jnp.asarray(cos[:, perm], dtype=F32), jnp.asarray(sin[:, perm], dtype=F32)


def _take_static(g, perm):
    runs, start = [], 0
    for i in range(1, len(perm) + 1):
        if i == len(perm) or perm[i] != perm[i - 1] + 1:
            runs.append(g[int(perm[start]):int(perm[i - 1]) + 1])
            start = i
    return jnp.concatenate(runs)


def _gain_row(parts):
    rows = []
    for g, heads, scale, perm in parts:
        if g is None:
            g = jnp.ones((HEAD_DIM,), F32)
        g = g.astype(F32) * scale
        if perm is not None:
            g = _take_static(g, perm)
        rows.append(jnp.tile(g, heads))
    return jnp.concatenate(rows).reshape(1, -1)


def _layer_ab(x, xg, ssq, w_in, w_out, li, g_next, a_qn, a_kn, na_table, b_qn, b_kn, axial):
    perm = _moves_to_perm(axial[0])
    gains = _gain_row([(a_qn, A_HEADS, SCALE, None), (a_kn, A_HEADS, 1.0, None), (None, A_HEADS, 1.0, None),
                       (b_qn, B_Q_HEADS, SCALE, perm), (b_kn, B_KV_HEADS, 1.0, perm),
                       (None, B_KV_HEADS, 1.0, None)])
    h = in_proj(xg, ssq, w_in, li, AB_KINDS, gains, axial)
    oa = neighbourhood_attention(h, na_table, li)
    ob = dense_gqa(h)
    return matmul_residual([oa, ob], w_out, li, x, g_next)


def _layer_cd(x, xg, ssq, w_in, w_out, li, g_next, c_qn, c_kn, c_sink, d_qn, d_kn, partial):
    perm = _moves_to_perm(partial[0])
    gains = _gain_row([(c_qn, C_Q_HEADS, SCALE, perm), (c_kn, C_KV_HEADS, 1.0, perm), (None, C_KV_HEADS, 1.0, None),
                       (d_qn, D_Q_HEADS, SCALE, perm), (d_kn, D_SLOTS, 1.0, perm), (None, D_SLOTS, 1.0, None)])
    h = in_proj(xg, ssq, w_in, li, CD_KINDS, gains, partial)
    oc = window_gqa_sink(h, c_sink)
    od = dilated_attention(h)
    return matmul_residual([oc, od], w_out, li, x, g_next)


def kernel(x, attn_norm, ffn_norm, ab_w_in, ab_w_out, a_q_norm, a_k_norm, a_rpb, b_q_norm, b_k_norm,
           cd_w_in, cd_w_out, c_q_norm, c_k_norm, c_sink, d_q_norm, d_k_norm, w_gate, w_up, w_down):
    axial = _rope_tables("axial")
    partial = _rope_tables("partial")
    x = x.reshape(M_ROWS, D_MODEL)
    xg, ssq = prenorm(x, attn_norm[0])
    na_table = _na_column_table(a_rpb)
    for layer in range(DEPTH):
        j = layer // 2
        if layer % 2 == 0:
            x, xg, ssq = _layer_ab(x, xg, ssq, ab_w_in, ab_w_out, j, ffn_norm[layer], a_q_norm[j], a_k_norm[j],
                                   na_table, b_q_norm[j], b_k_norm[j], axial)
        else:
            x, xg, ssq = _layer_cd(x, xg, ssq, cd_w_in, cd_w_out, j, ffn_norm[layer], c_q_norm[j], c_k_norm[j],
                                   c_sink[j], d_q_norm[j], d_k_norm[j], partial)
        hidden = gate_up(xg, ssq, w_gate, w_up, layer)
        if layer + 1 < DEPTH:
            x, xg, ssq = matmul_residual([hidden], w_down, layer, x, attn_norm[layer + 1])
        else:
            x = matmul_residual([hidden], w_down, layer, x)
    return x.reshape(BATCH, SEQ, D_MODEL)
```
